```python
import jax, jax.numpy as jnp
from jax import lax
import numpy as np

D_MODEL = 1024
BATCH = 16
SEQ = 4096
DEPTH = 4

N_A_LAYERS = DEPTH // 2
N_B_LAYERS = DEPTH - N_A_LAYERS
MLSTM_HEADS = 8
MLSTM_QK_DIM = D_MODEL // 2 // MLSTM_HEADS
MLSTM_V_DIM = D_MODEL // MLSTM_HEADS
MLSTM_CHUNK = 64
MLSTM_SPLITS = [MLSTM_HEADS * MLSTM_QK_DIM, MLSTM_HEADS * MLSTM_QK_DIM, MLSTM_HEADS * MLSTM_V_DIM,
                MLSTM_HEADS * MLSTM_V_DIM, MLSTM_HEADS, MLSTM_HEADS]
MLSTM_IN = sum(MLSTM_SPLITS)
FOX_HEADS = 16
FOX_HEAD_DIM = D_MODEL // FOX_HEADS
FOX_BLOCK = 128
D_FF = 2816
CONV_WIDTH = 3
EPS = 1e-6

kernel_name = 'hybrid_mlstm_fox_yoco_trunk'


def rmsnorm(x, w):
    xf = x.astype(jnp.float32)
    y = xf * lax.rsqrt(jnp.mean(xf * xf, axis=-1, keepdims=True) + EPS)
    return (y * w.astype(jnp.float32)).astype(x.dtype)


def mlstm_mixer(h, w_in, b_gate, w_head_norm, w_out):
    B, S, _ = h.shape
    H, DK, DV, L = MLSTM_HEADS, MLSTM_QK_DIM, MLSTM_V_DIM, MLSTM_CHUNK
    NC = S // L
    proj = h @ w_in
    idx = list(np.cumsum(MLSTM_SPLITS)[:-1])
    q, k, v, o, ig, fg = jnp.split(proj, idx, axis=-1)
    f32 = jnp.float32
    ig = ig.astype(f32) + b_gate[:H].astype(f32)
    lf = jax.nn.log_sigmoid(fg.astype(f32) + b_gate[H:].astype(f32))

    def to_chunks(t, d):
        return t.astype(f32).reshape(B, NC, L, H, d).transpose(1, 0, 3, 2, 4)

    qc = to_chunks(q, DK) * (DK ** -0.5)
    kc = to_chunks(k, DK)
    vc = to_chunks(v, DV)
    ic = ig.reshape(B, NC, L, H).transpose(1, 0, 3, 2)
    fc = lf.reshape(B, NC, L, H).transpose(1, 0, 3, 2)
    causal = jnp.tril(jnp.ones((L, L), dtype=bool))

    def step(carry, inp):
        C, n, m = carry
        qt, kt, vt, it, ft = inp
        b = jnp.cumsum(ft, axis=-1)
        dmat = jnp.where(causal, b[..., :, None] - b[..., None, :] + it[..., None, :], -jnp.inf)
        m_inter = b + m[..., None]
        m_t = jnp.maximum(m_inter, jnp.max(dmat, axis=-1))
        s = jnp.einsum('bhtd,bhsd->bhts', qt, kt) * jnp.exp(dmat - m_t[..., None])
        w_inter = jnp.exp(m_inter - m_t)
        num = jnp.einsum('bhts,bhse->bhte', s, vt) + w_inter[..., None] * jnp.einsum('bhtd,bhde->bhte', qt, C)
        den = jnp.sum(s, axis=-1) + w_inter * jnp.einsum('bhtd,bhd->bht', qt, n)
        ht = num / jnp.maximum(jnp.abs(den), jnp.exp(-m_t))[..., None]
        b_last = b[..., -1]
        g = b_last[..., None] - b + it
        m_new = jnp.maximum(b_last + m, jnp.max(g, axis=-1))
        decay = jnp.exp(b_last + m - m_new)
        wk = jnp.exp(g - m_new[..., None])
        C_new = decay[..., None, None] * C + jnp.einsum('bhs,bhsd,bhse->bhde', wk, kt, vt)
        n_new = decay[..., None] * n + jnp.einsum('bhs,bhsd->bhd', wk, kt)
        return (C_new, n_new, m_new), ht

    init = (jnp.zeros((B, H, DK, DV), f32), jnp.zeros((B, H, DK), f32), jnp.zeros((B, H), f32))
    _, hs = lax.scan(step, init, (qc, kc, vc, ic, fc))
    hs = hs.transpose(1, 0, 3, 2, 4).reshape(B, S, H, DV)
    hs = hs * lax.rsqrt(jnp.mean(hs * hs, axis=-1, keepdims=True) + EPS)
    hs = hs.reshape(B, S, H * DV) * w_head_norm.astype(f32)
    hs = hs * jax.nn.sigmoid(o.astype(f32))
    return hs.astype(h.dtype) @ w_out


def shared_kv(x, kv_norm, kv_w, kv_b_f):
    B, S, _ = x.shape
    h = rmsnorm(x, kv_norm)
    proj = h @ kv_w
    k, v, f = jnp.split(proj, [D_MODEL, 2 * D_MODEL], axis=-1)
    k = k.reshape(B, S, FOX_HEADS, FOX_HEAD_DIM)
    v = v.reshape(B, S, FOX_HEADS, FOX_HEAD_DIM)
    log_f = jax.nn.log_sigmoid(f.astype(jnp.float32) + kv_b_f.astype(jnp.float32))
    c = jnp.cumsum(log_f, axis=1).transpose(0, 2, 1)
    return k, v, c


def fox_mixer(h, w_qg, w_out, k, v, c):
    B, S, _ = h.shape
    H, DH, BLK = FOX_HEADS, FOX_HEAD_DIM, FOX_BLOCK
    NQ = S // BLK
    proj = h @ w_qg
    q, g = jnp.split(proj, [D_MODEL], axis=-1)
    q = q.reshape(B, S, H, DH) * (DH ** -0.5)
    qb = q.reshape(B, NQ, BLK, H, DH).transpose(1, 0, 2, 3, 4)
    cb = c.reshape(B, H, NQ, BLK).transpose(2, 0, 1, 3)
    starts = jnp.arange(NQ) * BLK
    key_pos = jnp.arange(S)

    def block(args):
        qi, ci, st = args
        logits = jnp.einsum('bqhd,bkhd->bhqk', qi, k).astype(jnp.float32)
        logits = logits + (ci[..., :, None] - c[..., None, :])
        qpos = st + jnp.arange(BLK)
        mask = key_pos[None, :] <= qpos[:, None]
        logits = jnp.where(mask, logits, -jnp.inf)
        p = jax.nn.softmax(logits, axis=-1)
        return jnp.einsum('bhqk,bkhd->bqhd', p.astype(v.dtype), v)

    o = lax.map(block, (qb, cb, starts))
    o = o.transpose(1, 0, 2, 3, 4).reshape(B, S, D_MODEL)
    o = o * jax.nn.sigmoid(g)
    return o @ w_out


def conv_ffn(h, w_up, conv_w, conv_b, w_down):
    S = h.shape[1]
    u = h @ w_up
    up = jnp.pad(u, ((0, 0), (CONV_WIDTH - 1, 0), (0, 0)))
    y = conv_b
    for j in range(CONV_WIDTH):
        y = y + conv_w[j] * up[:, j:j + S]
    gate, val = jnp.split(y, 2, axis=-1)
    return (jax.nn.gelu(gate, approximate=True) * val) @ w_down


def setup_inputs(seed: int = 0) -> dict:
    key = jax.random.key(seed)
    ks = jax.random.split(key, 24)
    D, H = D_MODEL, MLSTM_HEADS
    f32 = jnp.float32

    def nrm(k, shape, scale):
        return jax.random.normal(k, shape, f32) * scale

    def gain(k, shape):
        return 1.0 + 0.05 * jax.random.normal(k, shape, f32)

    x = jax.random.normal(ks[0], (BATCH, SEQ, D), f32)
    norm_mix_pre = gain(ks[1], (DEPTH, D))
    norm_mix_post = gain(ks[2], (DEPTH, D))
    norm_ffn_pre = gain(ks[3], (DEPTH, D))
    norm_ffn_post = gain(ks[4], (DEPTH, D))
    mlstm_w_in = nrm(ks[5], (N_A_LAYERS, D, MLSTM_IN), D ** -0.5)
    mlstm_b_gate = jnp.concatenate([
        nrm(ks[6], (N_A_LAYERS, H), 0.1),
        jax.random.uniform(ks[7], (N_A_LAYERS, H), f32, minval=3.0, maxval=6.0)], axis=-1)
    mlstm_norm = gain(ks[8], (N_A_LAYERS, H * MLSTM_V_DIM))
    mlstm_w_out = nrm(ks[9], (N_A_LAYERS, H * MLSTM_V_DIM, D), (H * MLSTM_V_DIM) ** -0.5)
    kv_norm = gain(ks[10], (D,))
    kv_w = nrm(ks[11], (D, 2 * D + FOX_HEADS), D ** -0.5)
    kv_b_f = jax.random.uniform(ks[12], (FOX_HEADS,), f32, minval=1.0, maxval=6.0)
    fox_w_qg = nrm(ks[13], (N_B_LAYERS, D, 2 * D), D ** -0.5)
    fox_w_out = nrm(ks[14], (N_B_LAYERS, D, D), D ** -0.5)
    ffn_w_up = nrm(ks[15], (DEPTH, D, 2 * D_FF), D ** -0.5)
    ffn_conv_w = nrm(ks[16], (DEPTH, CONV_WIDTH, 2 * D_FF), CONV_WIDTH ** -0.5)
    ffn_conv_b = nrm(ks[17], (DEPTH, 2 * D_FF), 0.02)
    ffn_w_down = nrm(ks[18], (DEPTH, D_FF, D), D_FF ** -0.5)
    return {'x': x, 'norm_mix_pre': norm_mix_pre, 'norm_mix_post': norm_mix_post,
            'norm_ffn_pre': norm_ffn_pre, 'norm_ffn_post': norm_ffn_post,
            'mlstm_w_in': mlstm_w_in, 'mlstm_b_gate': mlstm_b_gate, 'mlstm_norm': mlstm_norm,
            'mlstm_w_out': mlstm_w_out, 'kv_norm': kv_norm, 'kv_w': kv_w, 'kv_b_f': kv_b_f,
            'fox_w_qg': fox_w_qg, 'fox_w_out': fox_w_out, 'ffn_w_up': ffn_w_up,
            'ffn_conv_w': ffn_conv_w, 'ffn_conv_b': ffn_conv_b, 'ffn_w_down': ffn_w_down}


def reference(x, norm_mix_pre, norm_mix_post, norm_ffn_pre, norm_ffn_post, mlstm_w_in, mlstm_b_gate,
              mlstm_norm, mlstm_w_out, kv_norm, kv_w, kv_b_f, fox_w_qg, fox_w_out, ffn_w_up,
              ffn_conv_w, ffn_conv_b, ffn_w_down):
    k_sh = v_sh = c_sh = None
    for layer in range(DEPTH):
        h = rmsnorm(x, norm_mix_pre[layer])
        if layer < N_A_LAYERS:
            y = mlstm_mixer(h, mlstm_w_in[layer], mlstm_b_gate[layer], mlstm_norm[layer], mlstm_w_out[layer])
        else:
            j = layer - N_A_LAYERS
            y = fox_mixer(h, fox_w_qg[j], fox_w_out[j], k_sh, v_sh, c_sh)
        x = x + rmsnorm(y, norm_mix_post[layer])
        h = rmsnorm(x, norm_ffn_pre[layer])
        y = conv_ffn(h, ffn_w_up[layer], ffn_conv_w[layer], ffn_conv_b[layer], ffn_w_down[layer])
        x = x + rmsnorm(y, norm_ffn_post[layer])
        if layer == N_A_LAYERS - 1:
            k_sh, v_sh, c_sh = shared_kv(x, kv_norm, kv_w, kv_b_f)
    return x
```

```python
import functools
import math

import jax
import jax.numpy as jnp
from jax import lax
from jax.experimental import pallas as pl
from jax.experimental.pallas import tpu as pltpu

EPS = 1e-6
MLSTM_HEADS = 8
MLSTM_QK_DIM = 64
MLSTM_V_DIM = 128
FOX_HEADS = 16
FOX_HEAD_DIM = 64
CONV_WIDTH = 3

LANES = 128
SUBLANES = 8
VMEM_LIMIT_BYTES = 56 * 1024 * 1024
ROW_TILE = 512
MLSTM_CHUNK = 128
ATTN_TILE = 512
FFN_CHUNK = 256
DECAY_PARTS = 3

_BF16 = jnp.bfloat16
_F32 = jnp.float32


def _params(*semantics):
    return pltpu.CompilerParams(dimension_semantics=semantics, vmem_limit_bytes=VMEM_LIMIT_BYTES)


def _resident(shape):
    return pl.BlockSpec(shape, lambda *_: (0,) * len(shape), pipeline_mode=pl.Buffered(1))


def _rms(x, gain):
    return x * lax.rsqrt(jnp.mean(x * x, axis=-1, keepdims=True) + EPS) * gain


def _norm_proj_kernel(x_ref, g_ref, *refs, n_out):
    w_refs, o_refs = refs[:n_out], refs[n_out:]
    h = _rms(x_ref[...], g_ref[...]).astype(_BF16)
    for w_ref, o_ref in zip(w_refs, o_refs):
        n = w_ref.shape[1]
        for c0 in range(0, n, 512):
            c1 = min(c0 + 512, n)
            o_ref[:, c0:c1] = jnp.dot(h, w_ref[:, c0:c1], preferred_element_type=_F32).astype(o_ref.dtype)


def _norm_proj(x, gain, weights, out_dtypes, name):
    t, d = x.shape
    tm = min(ROW_TILE, t)
    in_specs = [pl.BlockSpec((tm, d), lambda i: (i, 0)), _resident((1, d))]
    in_specs += [_resident(w.shape) for w in weights]
    out_specs = [pl.BlockSpec((tm, w.shape[1]), lambda i: (i, 0)) for w in weights]
    out_shape = [jax.ShapeDtypeStruct((t, w.shape[1]), dt) for w, dt in zip(weights, out_dtypes)]
    return pl.pallas_call(
        functools.partial(_norm_proj_kernel, n_out=len(weights)),
        grid=(t // tm,), in_specs=in_specs, out_specs=out_specs, out_shape=out_shape,
        compiler_params=_params("parallel"), name=name,
    )(x, gain.reshape(1, d), *weights)


def _proj_residual_kernel(a_ref, w_ref, x_ref, g_ref, o_ref):
    y = jnp.dot(a_ref[...], w_ref[...], preferred_element_type=_F32)
    o_ref[...] = x_ref[...] + _rms(y, g_ref[...])


def _proj_residual(a, w, x, gain, name):
    t, d = x.shape
    k = a.shape[1]
    tm = min(ROW_TILE, t)
    return pl.pallas_call(
        _proj_residual_kernel,
        grid=(t // tm,),
        in_specs=[pl.BlockSpec((tm, k), lambda i: (i, 0)), _resident((k, d)),
                  pl.BlockSpec((tm, d), lambda i: (i, 0)), _resident((1, d))],
        out_specs=pl.BlockSpec((tm, d), lambda i: (i, 0)),
        out_shape=jax.ShapeDtypeStruct((t, d), _F32),
        compiler_params=_params("parallel"), name=name,
    )(a, w, x, gain.reshape(1, d))


def _log_sigmoid(z):
    return jnp.minimum(z, 0.0) - jnp.log1p(jnp.exp(-jnp.abs(z)))


def _mlstm_kernel(q_ref, k_ref, v_ref, o_ref, gc_ref, gr_ref, bc_ref, br_ref, hn_ref,
                  out_ref, c_ref, m_ref):
    nh, dk, dv = MLSTM_HEADS, MLSTM_QK_DIM, MLSTM_V_DIM
    L = q_ref.shape[1]

    @pl.when(pl.program_id(1) == 0)
    def _():
        c_ref[...] = jnp.zeros_like(c_ref)
        m_ref[...] = jnp.zeros_like(m_ref)

    row = lax.broadcasted_iota(jnp.int32, (L, L), 0)
    col = lax.broadcasted_iota(jnp.int32, (L, L), 1)
    causal = row >= col
    tril = causal.astype(_F32)
    triu = (row <= col).astype(_F32)

    gc = gc_ref[0] + bc_ref[...]
    gr = gr_ref[0] + br_ref[:, 0:1]
    lane_g = lax.broadcasted_iota(jnp.int32, gc.shape, 1)
    sub_g = lax.broadcasted_iota(jnp.int32, gr.shape, 0)
    lf_c = jnp.where(lane_g >= nh, _log_sigmoid(gc), 0.0)
    lf_r = jnp.where(sub_g >= nh, _log_sigmoid(gr), 0.0)
    b_c_all = jnp.dot(tril, lf_c, preferred_element_type=_F32, precision=lax.Precision.HIGHEST)
    b_r_all = jnp.dot(lf_r, triu, preferred_element_type=_F32, precision=lax.Precision.HIGHEST)

    lane_qk = lax.broadcasted_iota(jnp.int32, (L, 2 * dk), 1)
    ones_v = jnp.ones((L, dv), _BF16)
    for p in range(nh // 2):
        qp = q_ref[0, :, p * 2 * dk:(p + 1) * 2 * dk]
        kp = k_ref[0, :, p * 2 * dk:(p + 1) * 2 * dk]
        for hh in range(2):
            h = 2 * p + hh
            mine = (lane_qk < dk) if hh == 0 else (lane_qk >= dk)
            qm = jnp.where(mine, qp, jnp.zeros_like(qp))
            s_qk = lax.dot_general(qm, kp, (((1,), (1,)), ((), ())), preferred_element_type=_F32)
            b_c = b_c_all[:, nh + h:nh + h + 1]
            i_c = gc[:, h:h + 1]
            b_r = b_r_all[nh + h:nh + h + 1, :]
            i_r = gr[h:h + 1, :]
            m_prev = m_ref[h:h + 1, 0:1]
            dmat = jnp.where(causal, b_c - b_r + i_r, -jnp.inf)
            m_inter = b_c + m_prev
            m_t = jnp.maximum(m_inter, jnp.max(dmat, axis=-1, keepdims=True))
            s = (s_qk * jnp.exp(dmat - m_t)).astype(_BF16)
            w_inter = jnp.exp(m_inter - m_t)
            v_aug = jnp.concatenate([v_ref[0, :, h * dv:(h + 1) * dv], ones_v], axis=1)
            c_pair = c_ref[p * 2 * dk:(p + 1) * 2 * dk, :].astype(_BF16)
            tot = (jnp.dot(s, v_aug, preferred_element_type=_F32)
                   + w_inter * jnp.dot(qm, c_pair, preferred_element_type=_F32))
            num, den = tot[:, :dv], tot[:, dv:]
            hv = num / jnp.maximum(jnp.abs(den), jnp.exp(-m_t))
            hv = _rms(hv, hn_ref[:, h * dv:(h + 1) * dv])
            hv = hv * jax.nn.sigmoid(o_ref[0, :, h * dv:(h + 1) * dv].astype(_F32))
            out_ref[0, :, h * dv:(h + 1) * dv] = hv.astype(out_ref.dtype)

            b_last = b_c[L - 1:L, :]
            g_c = b_last - b_c + i_c
            g_r = b_last - b_r + i_r
            m_new = jnp.maximum(b_last + m_prev, jnp.max(g_r, axis=-1, keepdims=True))
            decay = jnp.exp(b_last + m_prev - m_new)
            kw = (kp.astype(_F32) * jnp.exp(g_c - m_new)).astype(_BF16)
            upd = lax.dot_general(kw, v_aug, (((0,), (0,)), ((), ())), preferred_element_type=_F32)
            rows = slice(h * dk, (h + 1) * dk)
            c_ref[rows, :] = decay * c_ref[rows, :] + upd[hh * dk:(hh + 1) * dk, :]
            m_ref[h:h + 1, :] = jnp.broadcast_to(m_new, (1, m_ref.shape[1]))


def _mlstm(q, k, v, o, gates, b_gate, head_norm):
    b, s, _ = q.shape
    nh, dk, dv = MLSTM_HEADS, MLSTM_QK_DIM, MLSTM_V_DIM
    L = min(MLSTM_CHUNK, s)
    gates_row = jnp.swapaxes(gates[:, :, :2 * nh], 1, 2)
    bias_col = jnp.zeros((1, LANES), _F32).at[0, :2 * nh].set(b_gate)
    bias_row = b_gate.reshape(2 * nh, 1)
    tok = lambda width: pl.BlockSpec((1, L, width), lambda bi, ci: (bi, ci, 0))
    return pl.pallas_call(
        _mlstm_kernel,
        grid=(b, s // L),
        in_specs=[tok(nh * dk), tok(nh * dk), tok(nh * dv), tok(nh * dv), tok(LANES),
                  pl.BlockSpec((1, 2 * nh, L), lambda bi, ci: (bi, 0, ci)),
                  _resident((1, LANES)), _resident((2 * nh, 1)), _resident((1, nh * dv))],
        out_specs=tok(nh * dv),
        out_shape=jax.ShapeDtypeStruct((b, s, nh * dv), _BF16),
        scratch_shapes=[pltpu.VMEM((nh * dk, 2 * dv), _F32), pltpu.VMEM((nh, LANES), _F32)],
        compiler_params=_params("parallel", "arbitrary"), name="mlstm_chunk",
    )(q, k, v, o, gates, gates_row, bias_col, bias_row, head_norm.reshape(1, nh * dv))


def _gelu_tanh(x):
    return 0.5 * x * (1.0 + jnp.tanh(math.sqrt(2.0 / math.pi) * (x + 0.044715 * (x * x * x))))


def _ffn_kernel(x_ref, gpre_ref, wup_ref, cw_ref, cb_ref, wdn_ref, gpost_ref, o_ref,
                carry_ref, ubuf_ref, hid_ref, *, tiles_per_seq):
    tm = x_ref.shape[0]
    f = wdn_ref.shape[0]
    halo = SUBLANES

    @pl.when(pl.program_id(0) % tiles_per_seq == 0)
    def _():
        carry_ref[...] = jnp.zeros_like(carry_ref)

    x = x_ref[...]
    h = _rms(x, gpre_ref[...]).astype(_BF16)

    def conv(c0, width):
        cols = slice(c0, c0 + width)
        u = jnp.dot(h, wup_ref[:, cols], preferred_element_type=_F32)
        ubuf_ref[0:halo, 0:width] = carry_ref[:, cols]
        ubuf_ref[halo:halo + tm, 0:width] = u
        carry_ref[:, cols] = u[tm - halo:tm, :]
        y = cb_ref[:, cols] + cw_ref[2:3, cols] * u
        y = y + cw_ref[1:2, cols] * ubuf_ref[halo - 1:halo - 1 + tm, 0:width]
        y = y + cw_ref[0:1, cols] * ubuf_ref[halo - 2:halo - 2 + tm, 0:width]
        return y

    for c0 in range(0, f, FFN_CHUNK):
        width = min(FFN_CHUNK, f - c0)
        gate = conv(c0, width)
        val = conv(f + c0, width)
        hid_ref[:, c0:c0 + width] = (_gelu_tanh(gate) * val).astype(_BF16)

    y = jnp.dot(hid_ref[...], wdn_ref[...], preferred_element_type=_F32)
    o_ref[...] = x + _rms(y, gpost_ref[...])


def _conv_ffn(x, g_pre, w_up, conv_w, conv_b, w_down, g_post, seq_len):
    t, d = x.shape
    f = w_down.shape[0]
    tm = min(ROW_TILE, seq_len)
    return pl.pallas_call(
        functools.partial(_ffn_kernel, tiles_per_seq=seq_len // tm),
        grid=(t // tm,),
        in_specs=[pl.BlockSpec((tm, d), lambda i: (i, 0)), _resident((1, d)), _resident((d, 2 * f)),
                  _resident((CONV_WIDTH, 2 * f)), _resident((1, 2 * f)), _resident((f, d)),
                  _resident((1, d))],
        out_specs=pl.BlockSpec((tm, d), lambda i: (i, 0)),
        out_shape=jax.ShapeDtypeStruct((t, d), _F32),
        scratch_shapes=[pltpu.VMEM((SUBLANES, 2 * f), _F32),
                        pltpu.VMEM((tm + SUBLANES, FFN_CHUNK), _F32),
                        pltpu.VMEM((tm, f), _BF16)],
        compiler_params=_params("arbitrary"), name="conv_ffn",
    )(x, g_pre.reshape(1, d), w_up, conv_w, conv_b.reshape(1, 2 * f), w_down, g_post.reshape(1, d))


def _decay_kernel(f_ref, bias_ref, place_ref, cq_ref, ck_ref, carry_ref):
    tm = f_ref.shape[1]

    @pl.when(pl.program_id(1) == 0)
    def _():
        carry_ref[...] = jnp.zeros_like(carry_ref)

    row = lax.broadcasted_iota(jnp.int32, (tm, tm), 0)
    col = lax.broadcasted_iota(jnp.int32, (tm, tm), 1)
    tril = (row >= col).astype(_F32)
    lane = lax.broadcasted_iota(jnp.int32, (tm, LANES), 1)
    lf = jnp.where(lane < FOX_HEADS, _log_sigmoid(f_ref[0] + bias_ref[...]), 0.0)
    c = carry_ref[0:1, :] + jnp.dot(tril, lf, preferred_element_type=_F32, precision=lax.Precision.HIGHEST)
    carry_ref[...] = jnp.broadcast_to(c[tm - 1:tm, :], carry_ref.shape)

    ones = jnp.where(lane < FOX_HEADS, 1.0, 0.0).astype(_BF16)
    cq = jnp.zeros(cq_ref.shape[1:], _F32)
    ck = jnp.zeros(ck_ref.shape[1:], _F32)
    rest = c
    for j in range(DECAY_PARTS):
        part = rest.astype(_BF16)
        rest = rest - part.astype(_F32)
        cq = cq + jnp.dot(part, place_ref[j], preferred_element_type=_F32)
        cq = cq + jnp.dot(ones, place_ref[DECAY_PARTS + j], preferred_element_type=_F32)
        ck = ck + jnp.dot(ones, place_ref[j], preferred_element_type=_F32)
        ck = ck - jnp.dot(part, place_ref[DECAY_PARTS + j], preferred_element_type=_F32)
    cq_ref[0] = cq.astype(cq_ref.dtype)
    ck_ref[0] = ck.astype(ck_ref.dtype)


def _decay_placement():
    nh, dh = FOX_HEADS, FOX_HEAD_DIM
    place = jnp.zeros((2 * DECAY_PARTS, LANES, nh * dh), _F32)
    heads = jnp.arange(nh)
    other_half = (heads // 2) * 2 * dh + (1 - heads % 2) * dh
    for j in range(2 * DECAY_PARTS):
        place = place.at[j, heads, other_half + j].set(1.0)
    return place.astype(_BF16)


def _fox_decay(f_pre, bias):
    b, s, _ = f_pre.shape
    nh, dh = FOX_HEADS, FOX_HEAD_DIM
    tm = min(256, s)
    bias_row = jnp.zeros((1, LANES), _F32).at[0, :nh].set(bias)
    out = jax.ShapeDtypeStruct((b, s, nh * dh), _BF16)
    return pl.pallas_call(
        _decay_kernel,
        grid=(b, s // tm),
        in_specs=[pl.BlockSpec((1, tm, LANES), lambda bi, ti: (bi, ti, 0)), _resident((1, LANES)),
                  _resident((2 * DECAY_PARTS, LANES, nh * dh))],
        out_specs=[pl.BlockSpec((1, tm, nh * dh), lambda bi, ti: (bi, ti, 0))] * 2,
        out_shape=[out, out],
        scratch_shapes=[pltpu.VMEM((SUBLANES, LANES), _F32)],
        compiler_params=_params("parallel", "arbitrary"), name="fox_decay",
    )(f_pre, bias_row, _decay_placement())


def _attn_kernel(q_ref, cq_ref, g_ref, k_ref, ck_ref, v_ref, o_ref, m_ref, l_ref, acc_ref):
    tq = q_ref.shape[1]
    tk = tq
    dh = FOX_HEAD_DIM
    qi = pl.program_id(2)
    lane = lax.broadcasted_iota(jnp.int32, (tq, 2 * dh), 1)
    first = lane < dh
    q_pair, cq_pair = q_ref[0], cq_ref[0]
    q_aug = (jnp.where(first, q_pair, cq_pair), jnp.where(first, cq_pair, q_pair))

    m_ref[...] = jnp.full_like(m_ref, -jnp.inf)
    l_ref[...] = jnp.zeros_like(l_ref)
    acc_ref[...] = jnp.zeros_like(acc_ref)

    def block(j, masked):
        start = pl.multiple_of(j * tk, tk)
        k_pair = k_ref[0, pl.ds(start, tk), :]
        ck_pair = ck_ref[0, pl.ds(start, tk), :]
        v_pair = v_ref[0, pl.ds(start, tk), :]
        k_aug = (jnp.where(first, k_pair, ck_pair), jnp.where(first, ck_pair, k_pair))
        for hh in range(2):
            s = lax.dot_general(q_aug[hh], k_aug[hh], (((1,), (1,)), ((), ())),
                                preferred_element_type=_F32)
            if masked:
                r = lax.broadcasted_iota(jnp.int32, (tq, tk), 0)
                c = lax.broadcasted_iota(jnp.int32, (tq, tk), 1)
                s = jnp.where(r >= c, s, -jnp.inf)
            m_old = m_ref[hh]
            m_new = jnp.maximum(m_old, jnp.max(s, axis=-1, keepdims=True))
            alpha = jnp.exp(m_old - m_new)
            p = jnp.exp(s - m_new)
            l_ref[hh] = alpha * l_ref[hh] + jnp.sum(p, axis=-1, keepdims=True)
            acc_ref[hh] = alpha * acc_ref[hh] + jnp.dot(p.astype(_BF16), v_pair,
                                                        preferred_element_type=_F32)
            m_ref[hh] = m_new

    def body(j, carry):
        block(j, masked=False)
        return carry

    lax.fori_loop(0, qi, body, 0)
    block(qi, masked=True)

    out = jnp.where(first, acc_ref[0] / l_ref[0], acc_ref[1] / l_ref[1])
    o_ref[0] = (out * jax.nn.sigmoid(g_ref[0].astype(_F32))).astype(o_ref.dtype)


def _fox_attention(q, cq, g, k, ck, v):
    b, s, hd = q.shape
    pairs = hd // LANES
    tq = min(ATTN_TILE, s)
    qspec = pl.BlockSpec((1, tq, LANES), lambda bi, pi, qi: (bi, qi, pi))
    kspec = pl.BlockSpec((1, s, LANES), lambda bi, pi, qi: (bi, 0, pi))
    return pl.pallas_call(
        _attn_kernel,
        grid=(b, pairs, s // tq),
        in_specs=[qspec, qspec, qspec, kspec, kspec, kspec],
        out_specs=qspec,
        out_shape=jax.ShapeDtypeStruct((b, s, hd), _BF16),
        scratch_shapes=[pltpu.VMEM((2, tq, 1), _F32), pltpu.VMEM((2, tq, 1), _F32),
                        pltpu.VMEM((2, tq, LANES), _F32)],
        compiler_params=_params("parallel", "parallel", "arbitrary"), name="fox_attention",
    )(q, cq, g, k, ck, v)


def _pad_cols(w, width):
    return jnp.pad(w, ((0, 0), (0, width - w.shape[1])))


def kernel(x, norm_mix_pre, norm_mix_post, norm_ffn_pre, norm_ffn_post, mlstm_w_in, mlstm_b_gate,
           mlstm_norm, mlstm_w_out, kv_norm, kv_w, kv_b_f, fox_w_qg, fox_w_out, ffn_w_up,
           ffn_conv_w, ffn_conv_b, ffn_w_down):
    b, s, d = x.shape
    depth = norm_mix_pre.shape[0]
    n_a = mlstm_w_in.shape[0]
    xt = x.reshape(b * s, d)
    hk = MLSTM_HEADS * MLSTM_QK_DIM
    hv = MLSTM_HEADS * MLSTM_V_DIM
    k_sh = v_sh = cq_sh = ck_sh = None
    for layer in range(depth):
        if layer < n_a:
            w_in = mlstm_w_in[layer]
            weights = [(w_in[:, :hk] * MLSTM_QK_DIM ** -0.5).astype(_BF16),
                       w_in[:, hk:2 * hk].astype(_BF16),
                       w_in[:, 2 * hk:2 * hk + hv].astype(_BF16),
                       w_in[:, 2 * hk + hv:2 * hk + 2 * hv].astype(_BF16),
                       _pad_cols(w_in[:, 2 * hk + 2 * hv:], LANES).astype(_BF16)]
            q, k, v, o, gates = _norm_proj(xt, norm_mix_pre[layer], weights,
                                           [_BF16, _BF16, _BF16, _BF16, _F32], "mlstm_in_proj")
            seq = lambda a: a.reshape(b, s, a.shape[-1])
            hs = _mlstm(seq(q), seq(k), seq(v), seq(o), seq(gates), mlstm_b_gate[layer], mlstm_norm[layer])
            xt = _proj_residual(hs.reshape(b * s, hv), mlstm_w_out[layer].astype(_BF16), xt,
                                norm_mix_post[layer], "mlstm_out_proj")
        else:
            j = layer - n_a
            w_qg = fox_w_qg[j]
            weights = [(w_qg[:, :d] * FOX_HEAD_DIM ** -0.5).astype(_BF16), w_qg[:, d:].astype(_BF16)]
            q, g = _norm_proj(xt, norm_mix_pre[layer], weights, [_BF16, _BF16], "fox_in_proj")
            seq = lambda a: a.reshape(b, s, a.shape[-1])
            att = _fox_attention(seq(q), cq_sh, seq(g), k_sh, ck_sh, v_sh)
            xt = _proj_residual(att.reshape(b * s, d), fox_w_out[j].astype(_BF16), xt,
                                norm_mix_post[layer], "fox_out_proj")
        xt = _conv_ffn(xt, norm_ffn_pre[layer], ffn_w_up[layer].astype(_BF16), ffn_conv_w[layer],
                       ffn_conv_b[layer], ffn_w_down[layer].astype(_BF16), norm_ffn_post[layer], s)
        if layer == n_a - 1:
            weights = [kv_w[:, :d].astype(_BF16), kv_w[:, d:2 * d].astype(_BF16),
                       _pad_cols(kv_w[:, 2 * d:], LANES).astype(_BF16)]
            k2, v2, f_pre = _norm_proj(xt, kv_norm, weights, [_BF16, _BF16, _F32], "kv_proj")
            k_sh, v_sh = k2.reshape(b, s, d), v2.reshape(b, s, d)
            cq_sh, ck_sh = _fox_decay(f_pre.reshape(b, s, LANES), kv_b_f)
    return xt.reshape(b, s, d)
```

```python
import functools
import math

import jax
import jax.numpy as jnp
from jax import lax
from jax.experimental import pallas as pl
from jax.experimental.pallas import tpu as pltpu

EPS = 1e-6
MLSTM_HEADS = 8
MLSTM_QK_DIM = 64
MLSTM_V_DIM = 128
FOX_HEADS = 16
FOX_HEAD_DIM = 64
CONV_WIDTH = 3

LANES = 128
SUBLANES = 8
VMEM_LIMIT_BYTES = 56 * 1024 * 1024
ROW_TILE = 512
MLSTM_CHUNK = 128
ATTN_TILE = 512
FFN_CHUNK = 256
DECAY_PARTS = 3
LOG2E = math.log2(math.e)

_BF16 = jnp.bfloat16
_F32 = jnp.float32


def _params(*semantics):
    return pltpu.CompilerParams(dimension_semantics=semantics, vmem_limit_bytes=VMEM_LIMIT_BYTES)


def _resident(shape):
    return pl.BlockSpec(shape, lambda *_: (0,) * len(shape), pipeline_mode=pl.Buffered(1))


def _rms(x, gain):
    return x * lax.rsqrt(jnp.mean(x * x, axis=-1, keepdims=True) + EPS) * gain


def _norm_proj_kernel(x_ref, g_ref, *refs, n_out):
    w_refs, o_refs = refs[:n_out], refs[n_out:]
    h = _rms(x_ref[...], g_ref[...]).astype(_BF16)
    for w_ref, o_ref in zip(w_refs, o_refs):
        n = w_ref.shape[1]
        for c0 in range(0, n, 512):
            c1 = min(c0 + 512, n)
            o_ref[:, c0:c1] = jnp.dot(h, w_ref[:, c0:c1], preferred_element_type=_F32).astype(o_ref.dtype)


def _norm_proj(x, gain, weights, out_dtypes, name):
    t, d = x.shape
    tm = min(ROW_TILE, t)
    in_specs = [pl.BlockSpec((tm, d), lambda i: (i, 0)), _resident((1, d))]
    in_specs += [_resident(w.shape) for w in weights]
    out_specs = [pl.BlockSpec((tm, w.shape[1]), lambda i: (i, 0)) for w in weights]
    out_shape = [jax.ShapeDtypeStruct((t, w.shape[1]), dt) for w, dt in zip(weights, out_dtypes)]
    return pl.pallas_call(
        functools.partial(_norm_proj_kernel, n_out=len(weights)),
        grid=(t // tm,), in_specs=in_specs, out_specs=out_specs, out_shape=out_shape,
        compiler_params=_params("parallel"), name=name,
    )(x, gain.reshape(1, d), *weights)


def _proj_residual_kernel(a_ref, w_ref, x_ref, g_ref, o_ref):
    y = jnp.dot(a_ref[...], w_ref[...], preferred_element_type=_F32)
    o_ref[...] = x_ref[...] + _rms(y, g_ref[...])


def _proj_residual(a, w, x, gain, name):
    t, d = x.shape
    k = a.shape[1]
    tm = min(ROW_TILE, t)
    return pl.pallas_call(
        _proj_residual_kernel,
        grid=(t // tm,),
        in_specs=[pl.BlockSpec((tm, k), lambda i: (i, 0)), _resident((k, d)),
                  pl.BlockSpec((tm, d), lambda i: (i, 0)), _resident((1, d))],
        out_specs=pl.BlockSpec((tm, d), lambda i: (i, 0)),
        out_shape=jax.ShapeDtypeStruct((t, d), _F32),
        compiler_params=_params("parallel"), name=name,
    )(a, w, x, gain.reshape(1, d))


def _log_sigmoid(z):
    return jnp.minimum(z, 0.0) - jnp.log1p(jnp.exp(-jnp.abs(z)))


def _mlstm_kernel(q_ref, k_ref, v_ref, o_ref, gc_ref, gr_ref, bc_ref, br_ref, hn_ref,
                  out_ref, c_ref, m_ref):
    nh, dk, dv = MLSTM_HEADS, MLSTM_QK_DIM, MLSTM_V_DIM
    L = q_ref.shape[1]

    @pl.when(pl.program_id(1) == 0)
    def _():
        c_ref[...] = jnp.zeros_like(c_ref)
        m_ref[...] = jnp.zeros_like(m_ref)

    row = lax.broadcasted_iota(jnp.int32, (L, L), 0)
    col = lax.broadcasted_iota(jnp.int32, (L, L), 1)
    causal = row >= col
    tril = causal.astype(_F32)
    triu = (row <= col).astype(_F32)

    gc = gc_ref[0] + bc_ref[...]
    gr = gr_ref[0] + br_ref[:, 0:1]
    lane_g = lax.broadcasted_iota(jnp.int32, gc.shape, 1)
    sub_g = lax.broadcasted_iota(jnp.int32, gr.shape, 0)
    lf_c = jnp.where(lane_g >= nh, _log_sigmoid(gc), 0.0)
    lf_r = jnp.where(sub_g >= nh, _log_sigmoid(gr), 0.0)
    b_c_all = jnp.dot(tril, lf_c, preferred_element_type=_F32, precision=lax.Precision.HIGHEST)
    b_r_all = jnp.dot(lf_r, triu, preferred_element_type=_F32, precision=lax.Precision.HIGHEST)

    lane_qk = lax.broadcasted_iota(jnp.int32, (L, 2 * dk), 1)
    ones_v = jnp.ones((L, dv), _BF16)
    for p in range(nh // 2):
        qp = q_ref[0, :, p * 2 * dk:(p + 1) * 2 * dk]
        kp = k_ref[0, :, p * 2 * dk:(p + 1) * 2 * dk]
        for hh in range(2):
            h = 2 * p + hh
            mine = (lane_qk < dk) if hh == 0 else (lane_qk >= dk)
            qm = jnp.where(mine, qp, jnp.zeros_like(qp))
            s_qk = lax.dot_general(qm, kp, (((1,), (1,)), ((), ())), preferred_element_type=_F32)
            b_c = b_c_all[:, nh + h:nh + h + 1]
            i_c = gc[:, h:h + 1]
            b_r = b_r_all[nh + h:nh + h + 1, :]
            i_r = gr[h:h + 1, :]
            m_prev = m_ref[h:h + 1, 0:1]
            dmat = jnp.where(causal, b_c - b_r + i_r, -jnp.inf)
            m_inter = b_c + m_prev
            m_t = jnp.maximum(m_inter, jnp.max(dmat, axis=-1, keepdims=True))
            s = (s_qk * jnp.exp(dmat - m_t)).astype(_BF16)
            w_inter = jnp.exp(m_inter - m_t)
            v_aug = jnp.concatenate([v_ref[0, :, h * dv:(h + 1) * dv], ones_v], axis=1)
            c_pair = c_ref[p * 2 * dk:(p + 1) * 2 * dk, :].astype(_BF16)
            tot = (jnp.dot(s, v_aug, preferred_element_type=_F32)
                   + w_inter * jnp.dot(qm, c_pair, preferred_element_type=_F32))
            num, den = tot[:, :dv], tot[:, dv:]
            hv = num / jnp.maximum(jnp.abs(den), jnp.exp(-m_t))
            hv = _rms(hv, hn_ref[:, h * dv:(h + 1) * dv])
            hv = hv * jax.nn.sigmoid(o_ref[0, :, h * dv:(h + 1) * dv].astype(_F32))
            out_ref[0, :, h * dv:(h + 1) * dv] = hv.astype(out_ref.dtype)

            b_last = b_c[L - 1:L, :]
            g_c = b_last - b_c + i_c
            g_r = b_last - b_r + i_r
            m_new = jnp.maximum(b_last + m_prev, jnp.max(g_r, axis=-1, keepdims=True))
            decay = jnp.exp(b_last + m_prev - m_new)
            kw = (kp.astype(_F32) * jnp.exp(g_c - m_new)).astype(_BF16)
            upd = lax.dot_general(kw, v_aug, (((0,), (0,)), ((), ())), preferred_element_type=_F32)
            rows = slice(h * dk, (h + 1) * dk)
            c_ref[rows, :] = decay * c_ref[rows, :] + upd[hh * dk:(hh + 1) * dk, :]
            m_ref[h:h + 1, :] = jnp.broadcast_to(m_new, (1, m_ref.shape[1]))


def _mlstm(q, k, v, o, gates, b_gate, head_norm):
    b, s, _ = q.shape
    nh, dk, dv = MLSTM_HEADS, MLSTM_QK_DIM, MLSTM_V_DIM
    L = min(MLSTM_CHUNK, s)
    gates_row = jnp.swapaxes(gates[:, :, :2 * nh], 1, 2)
    bias_col = jnp.zeros((1, LANES), _F32).at[0, :2 * nh].set(b_gate)
    bias_row = b_gate.reshape(2 * nh, 1)
    tok = lambda width: pl.BlockSpec((1, L, width), lambda bi, ci: (bi, ci, 0))
    return pl.pallas_call(
        _mlstm_kernel,
        grid=(b, s // L),
        in_specs=[tok(nh * dk), tok(nh * dk), tok(nh * dv), tok(nh * dv), tok(LANES),
                  pl.BlockSpec((1, 2 * nh, L), lambda bi, ci: (bi, 0, ci)),
                  _resident((1, LANES)), _resident((2 * nh, 1)), _resident((1, nh * dv))],
        out_specs=tok(nh * dv),
        out_shape=jax.ShapeDtypeStruct((b, s, nh * dv), _BF16),
        scratch_shapes=[pltpu.VMEM((nh * dk, 2 * dv), _F32), pltpu.VMEM((nh, LANES), _F32)],
        compiler_params=_params("parallel", "arbitrary"), name="mlstm_chunk",
    )(q, k, v, o, gates, gates_row, bias_col, bias_row, head_norm.reshape(1, nh * dv))


def _gelu_tanh(x):
    return 0.5 * x * (1.0 + jnp.tanh(math.sqrt(2.0 / math.pi) * (x + 0.044715 * (x * x * x))))


def _ffn_kernel(x_ref, gpre_ref, wup_ref, cw_ref, cb_ref, wdn_ref, gpost_ref, o_ref,
                carry_ref, ubuf_ref, hid_ref, *, tiles_per_seq):
    tm = x_ref.shape[0]
    f = wdn_ref.shape[0]
    halo = SUBLANES

    @pl.when(pl.program_id(0) % tiles_per_seq == 0)
    def _():
        carry_ref[...] = jnp.zeros_like(carry_ref)

    x = x_ref[...]
    h = _rms(x, gpre_ref[...]).astype(_BF16)

    def conv(c0, width):
        cols = slice(c0, c0 + width)
        u = jnp.dot(h, wup_ref[:, cols], preferred_element_type=_F32)
        ubuf_ref[0:halo, 0:width] = carry_ref[:, cols]
        ubuf_ref[halo:halo + tm, 0:width] = u
        carry_ref[:, cols] = u[tm - halo:tm, :]
        y = cb_ref[:, cols] + cw_ref[2:3, cols] * u
        y = y + cw_ref[1:2, cols] * ubuf_ref[halo - 1:halo - 1 + tm, 0:width]
        y = y + cw_ref[0:1, cols] * ubuf_ref[halo - 2:halo - 2 + tm, 0:width]
        return y

    for c0 in range(0, f, FFN_CHUNK):
        width = min(FFN_CHUNK, f - c0)
        gate = conv(c0, width)
        val = conv(f + c0, width)
        hid_ref[:, c0:c0 + width] = (_gelu_tanh(gate) * val).astype(_BF16)

    y = jnp.dot(hid_ref[...], wdn_ref[...], preferred_element_type=_F32)
    o_ref[...] = x + _rms(y, gpost_ref[...])


def _conv_ffn(x, g_pre, w_up, conv_w, conv_b, w_down, g_post, seq_len):
    t, d = x.shape
    f = w_down.shape[0]
    tm = min(ROW_TILE, seq_len)
    return pl.pallas_call(
        functools.partial(_ffn_kernel, tiles_per_seq=seq_len // tm),
        grid=(t // tm,),
        in_specs=[pl.BlockSpec((tm, d), lambda i: (i, 0)), _resident((1, d)), _resident((d, 2 * f)),
                  _resident((CONV_WIDTH, 2 * f)), _resident((1, 2 * f)), _resident((f, d)),
                  _resident((1, d))],
        out_specs=pl.BlockSpec((tm, d), lambda i: (i, 0)),
        out_shape=jax.ShapeDtypeStruct((t, d), _F32),
        scratch_shapes=[pltpu.VMEM((SUBLANES, 2 * f), _F32),
                        pltpu.VMEM((tm + SUBLANES, FFN_CHUNK), _F32),
                        pltpu.VMEM((tm, f), _BF16)],
        compiler_params=_params("arbitrary"), name="conv_ffn",
    )(x, g_pre.reshape(1, d), w_up, conv_w, conv_b.reshape(1, 2 * f), w_down, g_post.reshape(1, d))


def _decay_kernel(f_ref, bias_ref, place_ref, cq_ref, ck_ref, carry_ref):
    tm = f_ref.shape[1]

    @pl.when(pl.program_id(1) == 0)
    def _():
        carry_ref[...] = jnp.zeros_like(carry_ref)

    row = lax.broadcasted_iota(jnp.int32, (tm, tm), 0)
    col = lax.broadcasted_iota(jnp.int32, (tm, tm), 1)
    tril = (row >= col).astype(_F32)
    lane = lax.broadcasted_iota(jnp.int32, (tm, LANES), 1)
    lf = jnp.where(lane < FOX_HEADS, _log_sigmoid(f_ref[0] + bias_ref[...]), 0.0)
    c = carry_ref[0:1, :] + jnp.dot(tril, lf, preferred_element_type=_F32, precision=lax.Precision.HIGHEST)
    carry_ref[...] = jnp.broadcast_to(c[tm - 1:tm, :], carry_ref.shape)
    c = c * LOG2E

    ones = jnp.where(lane < FOX_HEADS, 1.0, 0.0).astype(_BF16)
    cq = jnp.zeros(cq_ref.shape[1:], _F32)
    ck = jnp.zeros(ck_ref.shape[1:], _F32)
    rest = c
    for j in range(DECAY_PARTS):
        part = rest.astype(_BF16)
        rest = rest - part.astype(_F32)
        cq = cq + jnp.dot(part, place_ref[j], preferred_element_type=_F32)
        cq = cq + jnp.dot(ones, place_ref[DECAY_PARTS + j], preferred_element_type=_F32)
        ck = ck + jnp.dot(ones, place_ref[j], preferred_element_type=_F32)
        ck = ck - jnp.dot(part, place_ref[DECAY_PARTS + j], preferred_element_type=_F32)
    cq_ref[0] = cq.astype(cq_ref.dtype)
    ck_ref[0] = ck.astype(ck_ref.dtype)


def _decay_placement():
    nh, dh = FOX_HEADS, FOX_HEAD_DIM
    place = jnp.zeros((2 * DECAY_PARTS, LANES, nh * dh), _F32)
    heads = jnp.arange(nh)
    other_half = (heads // 2) * 2 * dh + (1 - heads % 2) * dh
    for j in range(2 * DECAY_PARTS):
        place = place.at[j, heads, other_half + j].set(1.0)
    return place.astype(_BF16)


def _fox_decay(f_pre, bias):
    b, s, _ = f_pre.shape
    nh, dh = FOX_HEADS, FOX_HEAD_DIM
    tm = min(256, s)
    bias_row = jnp.zeros((1, LANES), _F32).at[0, :nh].set(bias)
    out = jax.ShapeDtypeStruct((b, s, nh * dh), _BF16)
    return pl.pallas_call(
        _decay_kernel,
        grid=(b, s // tm),
        in_specs=[pl.BlockSpec((1, tm, LANES), lambda bi, ti: (bi, ti, 0)), _resident((1, LANES)),
                  _resident((2 * DECAY_PARTS, LANES, nh * dh))],
        out_specs=[pl.BlockSpec((1, tm, nh * dh), lambda bi, ti: (bi, ti, 0))] * 2,
        out_shape=[out, out],
        scratch_shapes=[pltpu.VMEM((SUBLANES, LANES), _F32)],
        compiler_params=_params("parallel", "arbitrary"), name="fox_decay",
    )(f_pre, bias_row, _decay_placement())


def _attn_kernel(q_ref, cq_ref, g_ref, k_ref, ck_ref, vt_ref, o_ref, sa_ref, sb_ref, m_ref, acc_ref):
    tq = q_ref.shape[1]
    tk = sa_ref.shape[1]
    dh = FOX_HEAD_DIM
    qi = pl.program_id(2)
    first_q = lax.broadcasted_iota(jnp.int32, (tq, 2 * dh), 1) < dh
    first_k = lax.broadcasted_iota(jnp.int32, (tk, 2 * dh), 1) < dh
    q_pair, cq_pair = q_ref[0], cq_ref[0]
    q_aug = (jnp.where(first_q, q_pair, cq_pair), jnp.where(first_q, cq_pair, q_pair))

    m_ref[...] = jnp.full_like(m_ref, -jnp.inf)
    acc_ref[...] = jnp.zeros_like(acc_ref)

    def scores(j, s_ref):
        start = pl.multiple_of(j * tk, tk)
        k_pair = k_ref[0, pl.ds(start, tk), :]
        ck_pair = ck_ref[0, pl.ds(start, tk), :]
        k_aug = (jnp.where(first_k, k_pair, ck_pair), jnp.where(first_k, ck_pair, k_pair))
        for hh in range(2):
            s_ref[hh] = lax.dot_general(k_aug[hh], q_aug[hh], (((1,), (1,)), ((), ())),
                                        preferred_element_type=_F32)

    def accumulate(j, s_ref, diag_offset):
        start = pl.multiple_of(j * tk, tk)
        for hh in range(2):
            s = s_ref[hh]
            if diag_offset is not None:
                key = lax.broadcasted_iota(jnp.int32, (tk, tq), 0) + diag_offset
                qry = lax.broadcasted_iota(jnp.int32, (tk, tq), 1)
                s = jnp.where(key <= qry, s, -jnp.inf)
            m_old = m_ref[hh]
            m_new = jnp.maximum(m_old, jnp.max(s, axis=0, keepdims=True))
            alpha = jnp.exp2(m_old - m_new)
            p = jnp.exp2(s - m_new).astype(_BF16)
            vt = vt_ref[0, hh, :, pl.ds(start, tk)]
            acc_ref[hh] = alpha * acc_ref[hh] + jnp.dot(vt, p, preferred_element_type=_F32)
            m_ref[hh] = m_new

    per_tile = tq // tk
    scores(0, sa_ref)

    def body(i, carry):
        for u in range(per_tile):
            j = i * per_tile + u
            cur, nxt = (sa_ref, sb_ref) if u % 2 == 0 else (sb_ref, sa_ref)
            scores(j + 1, nxt)
            accumulate(j, cur, None)
        return carry

    lax.fori_loop(0, qi, body, 0)
    for u in range(per_tile):
        j = qi * per_tile + u
        cur, nxt = (sa_ref, sb_ref) if u % 2 == 0 else (sb_ref, sa_ref)
        if u + 1 < per_tile:
            scores(j + 1, nxt)
        accumulate(j, cur, u * tk)

    halves = []
    for hh in range(2):
        acc = acc_ref[hh]
        halves.append(acc[0:dh, :] / acc[dh:dh + 1, :])
    out = jnp.concatenate(halves, axis=0).T
    o_ref[0] = (out * jax.nn.sigmoid(g_ref[0].astype(_F32))).astype(o_ref.dtype)


def _fox_attention(q, cq, g, k, ck, vt):
    b, s, hd = q.shape
    pairs = hd // LANES
    tq = min(ATTN_TILE, s)
    tk = tq // 2
    rows = vt.shape[2]
    qspec = pl.BlockSpec((1, tq, LANES), lambda bi, pi, qi: (bi, qi, pi))
    kspec = pl.BlockSpec((1, s, LANES), lambda bi, pi, qi: (bi, 0, pi))
    vspec = pl.BlockSpec((1, 2, rows, s), lambda bi, pi, qi: (bi, pi, 0, 0))
    return pl.pallas_call(
        _attn_kernel,
        grid=(b, pairs, s // tq),
        in_specs=[qspec, qspec, qspec, kspec, kspec, vspec],
        out_specs=qspec,
        out_shape=jax.ShapeDtypeStruct((b, s, hd), _BF16),
        scratch_shapes=[pltpu.VMEM((2, tk, tq), _F32), pltpu.VMEM((2, tk, tq), _F32),
                        pltpu.VMEM((2, 1, tq), _F32), pltpu.VMEM((2, rows, tq), _F32)],
        compiler_params=_params("parallel", "parallel", "arbitrary"), name="fox_attention",
    )(q, cq, g, k, ck, vt)


def _transposed_values(v):
    b, s, _ = v.shape
    vt = v.reshape(b, s, FOX_HEADS, FOX_HEAD_DIM).transpose(0, 2, 3, 1)
    return jnp.concatenate([vt, jnp.ones((b, FOX_HEADS, 16, s), v.dtype)], axis=2)


def _pad_cols(w, width):
    return jnp.pad(w, ((0, 0), (0, width - w.shape[1])))


def kernel(x, norm_mix_pre, norm_mix_post, norm_ffn_pre, norm_ffn_post, mlstm_w_in, mlstm_b_gate,
           mlstm_norm, mlstm_w_out, kv_norm, kv_w, kv_b_f, fox_w_qg, fox_w_out, ffn_w_up,
           ffn_conv_w, ffn_conv_b, ffn_w_down):
    b, s, d = x.shape
    depth = norm_mix_pre.shape[0]
    n_a = mlstm_w_in.shape[0]
    xt = x.reshape(b * s, d)
    hk = MLSTM_HEADS * MLSTM_QK_DIM
    hv = MLSTM_HEADS * MLSTM_V_DIM
    k_sh = vt_sh = cq_sh = ck_sh = None
    for layer in range(depth):
        if layer < n_a:
            w_in = mlstm_w_in[layer]
            weights = [(w_in[:, :hk] * MLSTM_QK_DIM ** -0.5).astype(_BF16),
                       w_in[:, hk:2 * hk].astype(_BF16),
                       w_in[:, 2 * hk:2 * hk + hv].astype(_BF16),
                       w_in[:, 2 * hk + hv:2 * hk + 2 * hv].astype(_BF16),
                       _pad_cols(w_in[:, 2 * hk + 2 * hv:], LANES).astype(_BF16)]
            q, k, v, o, gates = _norm_proj(xt, norm_mix_pre[layer], weights,
                                           [_BF16, _BF16, _BF16, _BF16, _F32], "mlstm_in_proj")
            seq = lambda a: a.reshape(b, s, a.shape[-1])
            hs = _mlstm(seq(q), seq(k), seq(v), seq(o), seq(gates), mlstm_b_gate[layer], mlstm_norm[layer])
            xt = _proj_residual(hs.reshape(b * s, hv), mlstm_w_out[layer].astype(_BF16), xt,
                                norm_mix_post[layer], "mlstm_out_proj")
        else:
            j = layer - n_a
            w_qg = fox_w_qg[j]
            weights = [(w_qg[:, :d] * (FOX_HEAD_DIM ** -0.5 * LOG2E)).astype(_BF16),
                       w_qg[:, d:].astype(_BF16)]
            q, g = _norm_proj(xt, norm_mix_pre[layer], weights, [_BF16, _BF16], "fox_in_proj")
            seq = lambda a: a.reshape(b, s, a.shape[-1])
            att = _fox_attention(seq(q), cq_sh, seq(g), k_sh, ck_sh, vt_sh)
            xt = _proj_residual(att.reshape(b * s, d), fox_w_out[j].astype(_BF16), xt,
                                norm_mix_post[layer], "fox_out_proj")
        xt = _conv_ffn(xt, norm_ffn_pre[layer], ffn_w_up[layer].astype(_BF16), ffn_conv_w[layer],
                       ffn_conv_b[layer], ffn_w_down[layer].astype(_BF16), norm_ffn_post[layer], s)
        if layer == n_a - 1:
            weights = [kv_w[:, :d].astype(_BF16), kv_w[:, d:2 * d].astype(_BF16),
                       _pad_cols(kv_w[:, 2 * d:], LANES).astype(_BF16)]
            k2, v2, f_pre = _norm_proj(xt, kv_norm, weights, [_BF16, _BF16, _F32], "kv_proj")
            k_sh, vt_sh = k2.reshape(b, s, d), _transposed_values(v2.reshape(b, s, d))
            cq_sh, ck_sh = _fox_decay(f_pre.reshape(b, s, LANES), kv_b_f)
    return xt.reshape(b, s, d)
```

```python
import functools
import math

import jax
import jax.numpy as jnp
from jax import lax
from jax.experimental import pallas as pl
from jax.experimental.pallas import tpu as pltpu

EPS = 1e-6
MLSTM_HEADS = 8
MLSTM_QK_DIM = 64
MLSTM_V_DIM = 128
FOX_HEADS = 16
FOX_HEAD_DIM = 64
CONV_WIDTH = 3

LANES = 128
SUBLANES = 8
VMEM_LIMIT_BYTES = 56 * 1024 * 1024
ROW_TILE = 512
MLSTM_CHUNK = 128
ATTN_TILE = 512
FFN_CHUNK = 256
DECAY_PARTS = 3
LOG2E = math.log2(math.e)

_BF16 = jnp.bfloat16
_F32 = jnp.float32


def _params(*semantics):
    return pltpu.CompilerParams(dimension_semantics=semantics, vmem_limit_bytes=VMEM_LIMIT_BYTES)


def _resident(shape):
    return pl.BlockSpec(shape, lambda *_: (0,) * len(shape), pipeline_mode=pl.Buffered(1))


def _rms(x, gain):
    return x * lax.rsqrt(jnp.mean(x * x, axis=-1, keepdims=True) + EPS) * gain


def _norm_proj_kernel(x_ref, g_ref, *refs, transposed):
    n_out = len(transposed)
    w_refs, o_refs = refs[:n_out], refs[n_out:]
    h = _rms(x_ref[...], g_ref[...]).astype(_BF16)
    for w_ref, o_ref, tr in zip(w_refs, o_refs, transposed):
        n = w_ref.shape[0] if tr else w_ref.shape[1]
        for c0 in range(0, n, 512):
            c1 = min(c0 + 512, n)
            if tr:
                y = lax.dot_general(w_ref[c0:c1, :], h, (((1,), (1,)), ((), ())),
                                    preferred_element_type=_F32)
                o_ref[c0:c1, :] = y.astype(o_ref.dtype)
            else:
                y = jnp.dot(h, w_ref[:, c0:c1], preferred_element_type=_F32)
                o_ref[:, c0:c1] = y.astype(o_ref.dtype)


def _norm_proj(x, gain, weights, out_dtypes, name, transposed=None):
    t, d = x.shape
    tm = min(ROW_TILE, t)
    transposed = tuple(transposed or (False,) * len(weights))
    in_specs = [pl.BlockSpec((tm, d), lambda i: (i, 0)), _resident((1, d))]
    in_specs += [_resident(w.shape) for w in weights]
    out_specs, out_shape = [], []
    for w, dt, tr in zip(weights, out_dtypes, transposed):
        if tr:
            out_specs.append(pl.BlockSpec((w.shape[0], tm), lambda i: (0, i)))
            out_shape.append(jax.ShapeDtypeStruct((w.shape[0], t), dt))
        else:
            out_specs.append(pl.BlockSpec((tm, w.shape[1]), lambda i: (i, 0)))
            out_shape.append(jax.ShapeDtypeStruct((t, w.shape[1]), dt))
    return pl.pallas_call(
        functools.partial(_norm_proj_kernel, transposed=transposed),
        grid=(t // tm,), in_specs=in_specs, out_specs=out_specs, out_shape=out_shape,
        compiler_params=_params("parallel"), name=name,
    )(x, gain.reshape(1, d), *weights)


def _proj_residual_kernel(a_ref, w_ref, x_ref, g_ref, o_ref):
    y = jnp.dot(a_ref[...], w_ref[...], preferred_element_type=_F32)
    o_ref[...] = x_ref[...] + _rms(y, g_ref[...])


def _proj_residual(a, w, x, gain, name):
    t, d = x.shape
    k = a.shape[1]
    tm = min(ROW_TILE, t)
    return pl.pallas_call(
        _proj_residual_kernel,
        grid=(t // tm,),
        in_specs=[pl.BlockSpec((tm, k), lambda i: (i, 0)), _resident((k, d)),
                  pl.BlockSpec((tm, d), lambda i: (i, 0)), _resident((1, d))],
        out_specs=pl.BlockSpec((tm, d), lambda i: (i, 0)),
        out_shape=jax.ShapeDtypeStruct((t, d), _F32),
        compiler_params=_params("parallel"), name=name,
    )(a, w, x, gain.reshape(1, d))


def _log_sigmoid(z):
    return jnp.minimum(z, 0.0) - jnp.log1p(jnp.exp(-jnp.abs(z)))


ONES_ROWS = 16


def _mlstm_kernel(q_ref, k_ref, vt_ref, o_ref, gc_ref, gr_ref, bc_ref, br_ref, hn_ref,
                  out_ref, ct_ref, m_ref):
    nh, dk, dv = MLSTM_HEADS, MLSTM_QK_DIM, MLSTM_V_DIM
    L = q_ref.shape[1]

    @pl.when(pl.program_id(1) == 0)
    def _():
        ct_ref[...] = jnp.zeros_like(ct_ref)
        m_ref[...] = jnp.zeros_like(m_ref)

    row = lax.broadcasted_iota(jnp.int32, (L, L), 0)
    col = lax.broadcasted_iota(jnp.int32, (L, L), 1)
    source_before_target = row <= col
    tril = (row >= col).astype(_F32)
    triu = source_before_target.astype(_F32)

    gc = gc_ref[0] + bc_ref[...]
    gr = gr_ref[0] + br_ref[:, 0:1]
    lane_g = lax.broadcasted_iota(jnp.int32, gc.shape, 1)
    sub_g = lax.broadcasted_iota(jnp.int32, gr.shape, 0)
    lf_c = jnp.where(lane_g >= nh, _log_sigmoid(gc), 0.0)
    lf_r = jnp.where(sub_g >= nh, _log_sigmoid(gr), 0.0)
    b_c_all = jnp.dot(tril, lf_c, preferred_element_type=_F32, precision=lax.Precision.HIGHEST)
    b_r_all = jnp.dot(lf_r, triu, preferred_element_type=_F32, precision=lax.Precision.HIGHEST)

    lane_qk = lax.broadcasted_iota(jnp.int32, (L, 2 * dk), 1)
    ones_blk = jnp.ones((ONES_ROWS, L), _BF16)
    for p in range(nh // 2):
        qp = q_ref[0, :, p * 2 * dk:(p + 1) * 2 * dk]
        kp = k_ref[0, :, p * 2 * dk:(p + 1) * 2 * dk]
        for hh in range(2):
            h = 2 * p + hh
            mine = (lane_qk < dk) if hh == 0 else (lane_qk >= dk)
            qm = jnp.where(mine, qp, jnp.zeros_like(qp))
            ct = ct_ref[h]
            both = lax.dot_general(jnp.concatenate([kp, ct.astype(_BF16)], axis=0), qm,
                                   (((1,), (1,)), ((), ())), preferred_element_type=_F32)
            s_qk, inter = both[:L], both[L:]
            g_col = gc[:, h:h + 1] - b_c_all[:, nh + h:nh + h + 1]
            b_row = b_r_all[nh + h:nh + h + 1, :]
            i_row = gr[h:h + 1, :]
            m_prev = m_ref[h:h + 1, 0:1]
            dmat = jnp.where(source_before_target, g_col + b_row, -jnp.inf)
            m_inter = b_row + m_prev
            m_t = jnp.maximum(m_inter, jnp.max(dmat, axis=0, keepdims=True))
            sd = (s_qk * jnp.exp(dmat - m_t)).astype(_BF16)
            vt_aug = jnp.concatenate([vt_ref[h * dv:(h + 1) * dv, :], ones_blk], axis=0)
            tot = jnp.dot(vt_aug, sd, preferred_element_type=_F32) + jnp.exp(m_inter - m_t) * inter
            num, den = tot[:dv], tot[dv:dv + 1]
            hv = num / jnp.maximum(jnp.abs(den), jnp.exp(-m_t))
            hv = hv * lax.rsqrt(jnp.mean(hv * hv, axis=0, keepdims=True) + EPS)
            hv = (hv * hn_ref[h * dv:(h + 1) * dv, :]).T
            hv = hv * jax.nn.sigmoid(o_ref[0, :, h * dv:(h + 1) * dv].astype(_F32))
            out_ref[0, :, h * dv:(h + 1) * dv] = hv.astype(out_ref.dtype)

            b_last = b_row[:, L - 1:L]
            g_row = b_last - b_row + i_row
            m_new = jnp.maximum(b_last + m_prev, jnp.max(g_row, axis=-1, keepdims=True))
            decay = jnp.exp(b_last + m_prev - m_new)
            vw = (vt_aug.astype(_F32) * jnp.exp(g_row - m_new)).astype(_BF16)
            ct_ref[h] = decay * ct + jnp.dot(vw, kp, preferred_element_type=_F32)
            m_ref[h:h + 1, :] = jnp.broadcast_to(m_new, (1, m_ref.shape[1]))


def _mlstm(q, k, vt, o, gates, b_gate, head_norm):
    b, s, _ = q.shape
    nh, dk, dv = MLSTM_HEADS, MLSTM_QK_DIM, MLSTM_V_DIM
    L = min(MLSTM_CHUNK, s)
    chunks = s // L
    gates_row = jnp.swapaxes(gates[:, :, :2 * nh], 1, 2)
    bias_col = jnp.zeros((1, LANES), _F32).at[0, :2 * nh].set(b_gate)
    bias_row = b_gate.reshape(2 * nh, 1)
    norm_cols = jnp.broadcast_to(head_norm.astype(_F32)[:, None], (nh * dv, L))
    tok = lambda width: pl.BlockSpec((1, L, width), lambda bi, ci: (bi, ci, 0))
    return pl.pallas_call(
        _mlstm_kernel,
        grid=(b, chunks),
        in_specs=[tok(nh * dk), tok(nh * dk),
                  pl.BlockSpec((nh * dv, L), lambda bi, ci: (0, bi * chunks + ci)),
                  tok(nh * dv), tok(LANES),
                  pl.BlockSpec((1, 2 * nh, L), lambda bi, ci: (bi, 0, ci)),
                  _resident((1, LANES)), _resident((2 * nh, 1)), _resident((nh * dv, L))],
        out_specs=tok(nh * dv),
        out_shape=jax.ShapeDtypeStruct((b, s, nh * dv), _BF16),
        scratch_shapes=[pltpu.VMEM((nh, dv + ONES_ROWS, 2 * dk), _F32), pltpu.VMEM((nh, LANES), _F32)],
        compiler_params=_params("parallel", "arbitrary"), name="mlstm_chunk",
    )(q, k, vt, o, gates, gates_row, bias_col, bias_row, norm_cols)


def _gelu_tanh(x):
    return 0.5 * x * (1.0 + jnp.tanh(math.sqrt(2.0 / math.pi) * (x + 0.044715 * (x * x * x))))


def _ffn_kernel(x_ref, gpre_ref, wup_ref, cw_ref, cb_ref, wdn_ref, gpost_ref, o_ref,
                carry_ref, ubuf_ref, hid_ref, *, tiles_per_seq):
    tm = x_ref.shape[0]
    f = wdn_ref.shape[0]
    halo = SUBLANES

    @pl.when(pl.program_id(0) % tiles_per_seq == 0)
    def _():
        carry_ref[...] = jnp.zeros_like(carry_ref)

    x = x_ref[...]
    h = _rms(x, gpre_ref[...]).astype(_BF16)

    def conv(c0, width):
        cols = slice(c0, c0 + width)
        u = jnp.dot(h, wup_ref[:, cols], preferred_element_type=_F32)
        ubuf_ref[0:halo, 0:width] = carry_ref[:, cols]
        ubuf_ref[halo:halo + tm, 0:width] = u
        carry_ref[:, cols] = u[tm - halo:tm, :]
        y = cb_ref[:, cols] + cw_ref[2:3, cols] * u
        y = y + cw_ref[1:2, cols] * ubuf_ref[halo - 1:halo - 1 + tm, 0:width]
        y = y + cw_ref[0:1, cols] * ubuf_ref[halo - 2:halo - 2 + tm, 0:width]
        return y

    for c0 in range(0, f, FFN_CHUNK):
        width = min(FFN_CHUNK, f - c0)
        gate = conv(c0, width)
        val = conv(f + c0, width)
        hid_ref[:, c0:c0 + width] = (_gelu_tanh(gate) * val).astype(_BF16)

    y = jnp.dot(hid_ref[...], wdn_ref[...], preferred_element_type=_F32)
    o_ref[...] = x + _rms(y, gpost_ref[...])


def _conv_ffn(x, g_pre, w_up, conv_w, conv_b, w_down, g_post, seq_len):
    t, d = x.shape
    f = w_down.shape[0]
    tm = min(ROW_TILE, seq_len)
    return pl.pallas_call(
        functools.partial(_ffn_kernel, tiles_per_seq=seq_len // tm),
        grid=(t // tm,),
        in_specs=[pl.BlockSpec((tm, d), lambda i: (i, 0)), _resident((1, d)), _resident((d, 2 * f)),
                  _resident((CONV_WIDTH, 2 * f)), _resident((1, 2 * f)), _resident((f, d)),
                  _resident((1, d))],
        out_specs=pl.BlockSpec((tm, d), lambda i: (i, 0)),
        out_shape=jax.ShapeDtypeStruct((t, d), _F32),
        scratch_shapes=[pltpu.VMEM((SUBLANES, 2 * f), _F32),
                        pltpu.VMEM((tm + SUBLANES, FFN_CHUNK), _F32),
                        pltpu.VMEM((tm, f), _BF16)],
        compiler_params=_params("arbitrary"), name="conv_ffn",
    )(x, g_pre.reshape(1, d), w_up, conv_w, conv_b.reshape(1, 2 * f), w_down, g_post.reshape(1, d))


def _decay_kernel(f_ref, bias_ref, place_ref, cq_ref, ck_ref, carry_ref):
    tm = f_ref.shape[1]

    @pl.when(pl.program_id(1) == 0)
    def _():
        carry_ref[...] = jnp.zeros_like(carry_ref)

    row = lax.broadcasted_iota(jnp.int32, (tm, tm), 0)
    col = lax.broadcasted_iota(jnp.int32, (tm, tm), 1)
    tril = (row >= col).astype(_F32)
    lane = lax.broadcasted_iota(jnp.int32, (tm, LANES), 1)
    lf = jnp.where(lane < FOX_HEADS, _log_sigmoid(f_ref[0] + bias_ref[...]), 0.0)
    c = carry_ref[0:1, :] + jnp.dot(tril, lf, preferred_element_type=_F32, precision=lax.Precision.HIGHEST)
    carry_ref[...] = jnp.broadcast_to(c[tm - 1:tm, :], carry_ref.shape)
    c = c * LOG2E

    packed = jnp.where(lane < FOX_HEADS, 1.0, 0.0)
    packed = pltpu.roll(packed, DECAY_PARTS * FOX_HEADS, axis=1)
    rest = c
    for j in range(DECAY_PARTS):
        part = rest.astype(_BF16).astype(_F32)
        rest = rest - part
        packed = packed + (part if j == 0 else pltpu.roll(part, j * FOX_HEADS, axis=1))
    placed = jnp.dot(packed.astype(_BF16), place_ref[...], preferred_element_type=_F32)
    width = cq_ref.shape[2]
    cq_ref[0] = placed[:, :width].astype(cq_ref.dtype)
    ck_ref[0] = placed[:, width:].astype(ck_ref.dtype)


def _decay_placement():
    nh, dh, parts = FOX_HEADS, FOX_HEAD_DIM, DECAY_PARTS
    width = nh * dh
    heads = jnp.arange(nh)
    base = (heads // 2) * 2 * dh + (1 - heads % 2) * dh
    ones_row = parts * nh + heads
    place = jnp.zeros((LANES, 2 * width), _F32)
    for j in range(parts):
        place = place.at[j * nh + heads, base + j].set(1.0)
        place = place.at[ones_row, base + parts + j].set(1.0)
        place = place.at[ones_row, width + base + j].set(1.0)
        place = place.at[j * nh + heads, width + base + parts + j].set(-1.0)
    return place.astype(_BF16)


def _fox_decay(f_pre, bias):
    b, s, _ = f_pre.shape
    nh, dh = FOX_HEADS, FOX_HEAD_DIM
    tm = min(ROW_TILE, s)
    bias_row = jnp.zeros((1, LANES), _F32).at[0, :nh].set(bias)
    out = jax.ShapeDtypeStruct((b, s, nh * dh), _BF16)
    return pl.pallas_call(
        _decay_kernel,
        grid=(b, s // tm),
        in_specs=[pl.BlockSpec((1, tm, LANES), lambda bi, ti: (bi, ti, 0)), _resident((1, LANES)),
                  _resident((LANES, 2 * nh * dh))],
        out_specs=[pl.BlockSpec((1, tm, nh * dh), lambda bi, ti: (bi, ti, 0))] * 2,
        out_shape=[out, out],
        scratch_shapes=[pltpu.VMEM((SUBLANES, LANES), _F32)],
        compiler_params=_params("parallel", "arbitrary"), name="fox_decay",
    )(f_pre, bias_row, _decay_placement())


def _attn_kernel(q_ref, cq_ref, g_ref, k_ref, ck_ref, vt_ref, o_ref, sa_ref, sb_ref, m_ref, acc_ref):
    tq = q_ref.shape[1]
    tk = sa_ref.shape[1]
    dh = FOX_HEAD_DIM
    qi = pl.program_id(2)
    first_q = lax.broadcasted_iota(jnp.int32, (tq, 2 * dh), 1) < dh
    first_k = lax.broadcasted_iota(jnp.int32, (tk, 2 * dh), 1) < dh
    q_pair, cq_pair = q_ref[0], cq_ref[0]
    q_aug = (jnp.where(first_q, q_pair, cq_pair), jnp.where(first_q, cq_pair, q_pair))

    ones_blk = jnp.ones((ONES_ROWS, tk), _BF16)

    m_ref[...] = jnp.full_like(m_ref, -jnp.inf)
    acc_ref[...] = jnp.zeros_like(acc_ref)

    def scores(j, s_ref):
        start = pl.multiple_of(j * tk, tk)
        k_pair = k_ref[0, pl.ds(start, tk), :]
        ck_pair = ck_ref[0, pl.ds(start, tk), :]
        k_aug = (jnp.where(first_k, k_pair, ck_pair), jnp.where(first_k, ck_pair, k_pair))
        for hh in range(2):
            s_ref[hh] = lax.dot_general(k_aug[hh], q_aug[hh], (((1,), (1,)), ((), ())),
                                        preferred_element_type=_F32)

    def accumulate(j, s_ref, diag_offset):
        start = pl.multiple_of(j * tk, tk)
        for hh in range(2):
            s = s_ref[hh]
            if diag_offset is not None:
                key = lax.broadcasted_iota(jnp.int32, (tk, tq), 0) + diag_offset
                qry = lax.broadcasted_iota(jnp.int32, (tk, tq), 1)
                s = jnp.where(key <= qry, s, -jnp.inf)
            m_old = m_ref[hh]
            m_new = jnp.maximum(m_old, jnp.max(s, axis=0, keepdims=True))
            alpha = jnp.exp2(m_old - m_new)
            p = jnp.exp2(s - m_new).astype(_BF16)
            vt = jnp.concatenate([vt_ref[hh * dh:(hh + 1) * dh, pl.ds(start, tk)], ones_blk], axis=0)
            acc_ref[hh] = alpha * acc_ref[hh] + jnp.dot(vt, p, preferred_element_type=_F32)
            m_ref[hh] = m_new

    per_tile = tq // tk
    scores(0, sa_ref)

    def body(i, carry):
        for u in range(per_tile):
            j = i * per_tile + u
            cur, nxt = (sa_ref, sb_ref) if u % 2 == 0 else (sb_ref, sa_ref)
            scores(j + 1, nxt)
            accumulate(j, cur, None)
        return carry

    lax.fori_loop(0, qi, body, 0)
    for u in range(per_tile):
        j = qi * per_tile + u
        cur, nxt = (sa_ref, sb_ref) if u % 2 == 0 else (sb_ref, sa_ref)
        if u + 1 < per_tile:
            scores(j + 1, nxt)
        accumulate(j, cur, u * tk)

    halves = []
    for hh in range(2):
        acc = acc_ref[hh]
        halves.append(acc[0:dh, :] / acc[dh:dh + 1, :])
    out = jnp.concatenate(halves, axis=0).T
    o_ref[0] = (out * jax.nn.sigmoid(g_ref[0].astype(_F32))).astype(o_ref.dtype)


def _fox_attention(q, cq, g, k, ck, vt):
    b, s, hd = q.shape
    pairs = hd // LANES
    tq = min(ATTN_TILE, s)
    tk = tq // 2
    rows = FOX_HEAD_DIM + ONES_ROWS
    qspec = pl.BlockSpec((1, tq, LANES), lambda bi, pi, qi: (bi, qi, pi))
    kspec = pl.BlockSpec((1, s, LANES), lambda bi, pi, qi: (bi, 0, pi))
    vspec = pl.BlockSpec((LANES, s), lambda bi, pi, qi: (pi, bi))
    return pl.pallas_call(
        _attn_kernel,
        grid=(b, pairs, s // tq),
        in_specs=[qspec, qspec, qspec, kspec, kspec, vspec],
        out_specs=qspec,
        out_shape=jax.ShapeDtypeStruct((b, s, hd), _BF16),
        scratch_shapes=[pltpu.VMEM((2, tk, tq), _F32), pltpu.VMEM((2, tk, tq), _F32),
                        pltpu.VMEM((2, 1, tq), _F32), pltpu.VMEM((2, rows, tq), _F32)],
        compiler_params=_params("parallel", "parallel", "arbitrary"), name="fox_attention",
    )(q, cq, g, k, ck, vt)


def _pad_cols(w, width):
    return jnp.pad(w, ((0, 0), (0, width - w.shape[1])))


def kernel(x, norm_mix_pre, norm_mix_post, norm_ffn_pre, norm_ffn_post, mlstm_w_in, mlstm_b_gate,
           mlstm_norm, mlstm_w_out, kv_norm, kv_w, kv_b_f, fox_w_qg, fox_w_out, ffn_w_up,
           ffn_conv_w, ffn_conv_b, ffn_w_down):
    b, s, d = x.shape
    depth = norm_mix_pre.shape[0]
    n_a = mlstm_w_in.shape[0]
    xt = x.reshape(b * s, d)
    hk = MLSTM_HEADS * MLSTM_QK_DIM
    hv = MLSTM_HEADS * MLSTM_V_DIM
    k_sh = vt_sh = cq_sh = ck_sh = None
    for layer in range(depth):
        if layer < n_a:
            w_in = mlstm_w_in[layer]
            weights = [(w_in[:, :hk] * MLSTM_QK_DIM ** -0.5).astype(_BF16),
                       w_in[:, hk:2 * hk].astype(_BF16),
                       w_in[:, 2 * hk:2 * hk + hv].T.astype(_BF16),
                       w_in[:, 2 * hk + hv:2 * hk + 2 * hv].astype(_BF16),
                       _pad_cols(w_in[:, 2 * hk + 2 * hv:], LANES).astype(_BF16)]
            q, k, vt, o, gates = _norm_proj(xt, norm_mix_pre[layer], weights,
                                            [_BF16, _BF16, _BF16, _BF16, _F32], "mlstm_in_proj",
                                            transposed=(False, False, True, False, False))
            seq = lambda a: a.reshape(b, s, a.shape[-1])
            hs = _mlstm(seq(q), seq(k), vt, seq(o), seq(gates), mlstm_b_gate[layer], mlstm_norm[layer])
            xt = _proj_residual(hs.reshape(b * s, hv), mlstm_w_out[layer].astype(_BF16), xt,
                                norm_mix_post[layer], "mlstm_out_proj")
        else:
            j = layer - n_a
            w_qg = fox_w_qg[j]
            weights = [(w_qg[:, :d] * (FOX_HEAD_DIM ** -0.5 * LOG2E)).astype(_BF16),
                       w_qg[:, d:].astype(_BF16)]
            q, g = _norm_proj(xt, norm_mix_pre[layer], weights, [_BF16, _BF16], "fox_in_proj")
            seq = lambda a: a.reshape(b, s, a.shape[-1])
            att = _fox_attention(seq(q), cq_sh, seq(g), k_sh, ck_sh, vt_sh)
            xt = _proj_residual(att.reshape(b * s, d), fox_w_out[j].astype(_BF16), xt,
                                norm_mix_post[layer], "fox_out_proj")
        xt = _conv_ffn(xt, norm_ffn_pre[layer], ffn_w_up[layer].astype(_BF16), ffn_conv_w[layer],
                       ffn_conv_b[layer], ffn_w_down[layer].astype(_BF16), norm_ffn_post[layer], s)
        if layer == n_a - 1:
            weights = [kv_w[:, :d].astype(_BF16), kv_w[:, d:2 * d].T.astype(_BF16),
                       _pad_cols(kv_w[:, 2 * d:], LANES).astype(_BF16)]
            k2, vt_sh, f_pre = _norm_proj(xt, kv_norm, weights, [_BF16, _BF16, _F32], "kv_proj",
                                          transposed=(False, True, False))
            k_sh = k2.reshape(b, s, d)
            cq_sh, ck_sh = _fox_decay(f_pre.reshape(b, s, LANES), kv_b_f)
    return xt.reshape(b, s, d)
```

```python
import functools
import math

import jax
import jax.numpy as jnp
from jax import lax
from jax.experimental import pallas as pl
from jax.experimental.pallas import tpu as pltpu

EPS = 1e-6
MLSTM_HEADS = 8
MLSTM_QK_DIM = 64
MLSTM_V_DIM = 128
FOX_HEADS = 16
FOX_HEAD_DIM = 64
CONV_WIDTH = 3

LANES = 128
SUBLANES = 8
VMEM_LIMIT_BYTES = 56 * 1024 * 1024
ROW_TILE = 512
MLSTM_CHUNK = 128
ATTN_TILE = 1024
ATTN_KEYS = 256
FFN_CHUNK = 256
DECAY_PARTS = 3
LOG2E = math.log2(math.e)

_BF16 = jnp.bfloat16
_F32 = jnp.float32


def _params(*semantics):
    return pltpu.CompilerParams(dimension_semantics=semantics, vmem_limit_bytes=VMEM_LIMIT_BYTES)


def _resident(shape):
    return pl.BlockSpec(shape, lambda *_: (0,) * len(shape), pipeline_mode=pl.Buffered(1))


def _rms(x, gain):
    return x * lax.rsqrt(jnp.mean(x * x, axis=-1, keepdims=True) + EPS) * gain


def _norm_proj_kernel(x_ref, g_ref, *refs, transposed):
    n_out = len(transposed)
    w_refs, o_refs = refs[:n_out], refs[n_out:]
    h = _rms(x_ref[...], g_ref[...]).astype(_BF16)
    for w_ref, o_ref, tr in zip(w_refs, o_refs, transposed):
        n = w_ref.shape[0] if tr else w_ref.shape[1]
        for c0 in range(0, n, 512):
            c1 = min(c0 + 512, n)
            if tr:
                y = lax.dot_general(w_ref[c0:c1, :], h, (((1,), (1,)), ((), ())),
                                    preferred_element_type=_F32)
                o_ref[c0:c1, :] = y.astype(o_ref.dtype)
            else:
                y = jnp.dot(h, w_ref[:, c0:c1], preferred_element_type=_F32)
                o_ref[:, c0:c1] = y.astype(o_ref.dtype)


def _norm_proj(x, gain, weights, out_dtypes, name, transposed=None):
    t, d = x.shape
    tm = min(ROW_TILE, t)
    transposed = tuple(transposed or (False,) * len(weights))
    in_specs = [pl.BlockSpec((tm, d), lambda i: (i, 0)), _resident((1, d))]
    in_specs += [_resident(w.shape) for w in weights]
    out_specs, out_shape = [], []
    for w, dt, tr in zip(weights, out_dtypes, transposed):
        if tr:
            out_specs.append(pl.BlockSpec((w.shape[0], tm), lambda i: (0, i)))
            out_shape.append(jax.ShapeDtypeStruct((w.shape[0], t), dt))
        else:
            out_specs.append(pl.BlockSpec((tm, w.shape[1]), lambda i: (i, 0)))
            out_shape.append(jax.ShapeDtypeStruct((t, w.shape[1]), dt))
    return pl.pallas_call(
        functools.partial(_norm_proj_kernel, transposed=transposed),
        grid=(t // tm,), in_specs=in_specs, out_specs=out_specs, out_shape=out_shape,
        compiler_params=_params("parallel"), name=name,
    )(x, gain.reshape(1, d), *weights)


def _proj_residual_kernel(a_ref, w_ref, x_ref, g_ref, o_ref):
    y = jnp.dot(a_ref[...], w_ref[...], preferred_element_type=_F32)
    o_ref[...] = x_ref[...] + _rms(y, g_ref[...])


def _proj_residual(a, w, x, gain, name):
    t, d = x.shape
    k = a.shape[1]
    tm = min(ROW_TILE, t)
    return pl.pallas_call(
        _proj_residual_kernel,
        grid=(t // tm,),
        in_specs=[pl.BlockSpec((tm, k), lambda i: (i, 0)), _resident((k, d)),
                  pl.BlockSpec((tm, d), lambda i: (i, 0)), _resident((1, d))],
        out_specs=pl.BlockSpec((tm, d), lambda i: (i, 0)),
        out_shape=jax.ShapeDtypeStruct((t, d), _F32),
        compiler_params=_params("parallel"), name=name,
    )(a, w, x, gain.reshape(1, d))


def _log_sigmoid(z):
    return jnp.minimum(z, 0.0) - jnp.log1p(jnp.exp(-jnp.abs(z)))


ONES_ROWS = 16


def _mlstm_kernel(q_ref, k_ref, vt_ref, o_ref, gc_ref, gr_ref, bc_ref, br_ref, hn_ref,
                  out_ref, ct_ref, m_ref):
    nh, dk, dv = MLSTM_HEADS, MLSTM_QK_DIM, MLSTM_V_DIM
    L = q_ref.shape[1]

    @pl.when(pl.program_id(1) == 0)
    def _():
        ct_ref[...] = jnp.zeros_like(ct_ref)
        m_ref[...] = jnp.zeros_like(m_ref)

    row = lax.broadcasted_iota(jnp.int32, (L, L), 0)
    col = lax.broadcasted_iota(jnp.int32, (L, L), 1)
    source_before_target = row <= col
    tril = (row >= col).astype(_F32)
    triu = source_before_target.astype(_F32)

    gc = gc_ref[0] + bc_ref[...]
    gr = gr_ref[0] + br_ref[:, 0:1]
    lane_g = lax.broadcasted_iota(jnp.int32, gc.shape, 1)
    sub_g = lax.broadcasted_iota(jnp.int32, gr.shape, 0)
    lf_c = jnp.where(lane_g >= nh, _log_sigmoid(gc), 0.0)
    lf_r = jnp.where(sub_g >= nh, _log_sigmoid(gr), 0.0)
    b_c_all = jnp.dot(tril, lf_c, preferred_element_type=_F32, precision=lax.Precision.HIGHEST)
    b_r_all = jnp.dot(lf_r, triu, preferred_element_type=_F32, precision=lax.Precision.HIGHEST)

    lane_qk = lax.broadcasted_iota(jnp.int32, (L, 2 * dk), 1)
    ones_blk = jnp.ones((ONES_ROWS, L), _BF16)
    for p in range(nh // 2):
        qp = q_ref[0, :, p * 2 * dk:(p + 1) * 2 * dk]
        kp = k_ref[0, :, p * 2 * dk:(p + 1) * 2 * dk]
        for hh in range(2):
            h = 2 * p + hh
            mine = (lane_qk < dk) if hh == 0 else (lane_qk >= dk)
            qm = jnp.where(mine, qp, jnp.zeros_like(qp))
            ct = ct_ref[h]
            both = lax.dot_general(jnp.concatenate([kp, ct.astype(_BF16)], axis=0), qm,
                                   (((1,), (1,)), ((), ())), preferred_element_type=_F32)
            s_qk, inter = both[:L], both[L:]
            g_col = gc[:, h:h + 1] - b_c_all[:, nh + h:nh + h + 1]
            b_row = b_r_all[nh + h:nh + h + 1, :]
            i_row = gr[h:h + 1, :]
            m_prev = m_ref[h:h + 1, 0:1]
            dmat = jnp.where(source_before_target, g_col + b_row, -jnp.inf)
            m_inter = b_row + m_prev
            m_t = jnp.maximum(m_inter, jnp.max(dmat, axis=0, keepdims=True))
            sd = (s_qk * jnp.exp(dmat - m_t)).astype(_BF16)
            vt_aug = jnp.concatenate([vt_ref[h * dv:(h + 1) * dv, :], ones_blk], axis=0)
            tot = jnp.dot(vt_aug, sd, preferred_element_type=_F32) + jnp.exp(m_inter - m_t) * inter
            num, den = tot[:dv], tot[dv:dv + 1]
            hv = num / jnp.maximum(jnp.abs(den), jnp.exp(-m_t))
            hv = hv * lax.rsqrt(jnp.mean(hv * hv, axis=0, keepdims=True) + EPS)
            hv = (hv * hn_ref[h * dv:(h + 1) * dv, :]).T
            hv = hv * jax.nn.sigmoid(o_ref[0, :, h * dv:(h + 1) * dv].astype(_F32))
            out_ref[0, :, h * dv:(h + 1) * dv] = hv.astype(out_ref.dtype)

            b_last = b_row[:, L - 1:L]
            g_row = b_last - b_row + i_row
            m_new = jnp.maximum(b_last + m_prev, jnp.max(g_row, axis=-1, keepdims=True))
            decay = jnp.exp(b_last + m_prev - m_new)
            vw = (vt_aug.astype(_F32) * jnp.exp(g_row - m_new)).astype(_BF16)
            ct_ref[h] = decay * ct + jnp.dot(vw, kp, preferred_element_type=_F32)
            m_ref[h:h + 1, :] = jnp.broadcast_to(m_new, (1, m_ref.shape[1]))


def _mlstm(q, k, vt, o, gates, b_gate, head_norm):
    b, s, _ = q.shape
    nh, dk, dv = MLSTM_HEADS, MLSTM_QK_DIM, MLSTM_V_DIM
    L = min(MLSTM_CHUNK, s)
    chunks = s // L
    gates_row = jnp.swapaxes(gates[:, :, :2 * nh], 1, 2)
    bias_col = jnp.zeros((1, LANES), _F32).at[0, :2 * nh].set(b_gate)
    bias_row = b_gate.reshape(2 * nh, 1)
    norm_cols = jnp.broadcast_to(head_norm.astype(_F32)[:, None], (nh * dv, L))
    tok = lambda width: pl.BlockSpec((1, L, width), lambda bi, ci: (bi, ci, 0))
    return pl.pallas_call(
        _mlstm_kernel,
        grid=(b, chunks),
        in_specs=[tok(nh * dk), tok(nh * dk),
                  pl.BlockSpec((nh * dv, L), lambda bi, ci: (0, bi * chunks + ci)),
                  tok(nh * dv), tok(LANES),
                  pl.BlockSpec((1, 2 * nh, L), lambda bi, ci: (bi, 0, ci)),
                  _resident((1, LANES)), _resident((2 * nh, 1)), _resident((nh * dv, L))],
        out_specs=tok(nh * dv),
        out_shape=jax.ShapeDtypeStruct((b, s, nh * dv), _BF16),
        scratch_shapes=[pltpu.VMEM((nh, dv + ONES_ROWS, 2 * dk), _F32), pltpu.VMEM((nh, LANES), _F32)],
        compiler_params=_params("parallel", "arbitrary"), name="mlstm_chunk",
    )(q, k, vt, o, gates, gates_row, bias_col, bias_row, norm_cols)


def _gelu_tanh(x):
    return 0.5 * x * (1.0 + jnp.tanh(math.sqrt(2.0 / math.pi) * (x + 0.044715 * (x * x * x))))


def _ffn_kernel(x_ref, gpre_ref, wup_ref, cw_ref, cb_ref, wdn_ref, gpost_ref, o_ref,
                carry_ref, ubuf_ref, hid_ref, *, tiles_per_seq):
    tm = x_ref.shape[0]
    f = wdn_ref.shape[0]
    halo = SUBLANES

    @pl.when(pl.program_id(0) % tiles_per_seq == 0)
    def _():
        carry_ref[...] = jnp.zeros_like(carry_ref)

    x = x_ref[...]
    h = _rms(x, gpre_ref[...]).astype(_BF16)

    def conv(c0, width):
        cols = slice(c0, c0 + width)
        u = jnp.dot(h, wup_ref[:, cols], preferred_element_type=_F32)
        ubuf_ref[0:halo, 0:width] = carry_ref[:, cols]
        ubuf_ref[halo:halo + tm, 0:width] = u
        carry_ref[:, cols] = u[tm - halo:tm, :]
        y = cb_ref[:, cols] + cw_ref[2:3, cols] * u
        y = y + cw_ref[1:2, cols] * ubuf_ref[halo - 1:halo - 1 + tm, 0:width]
        y = y + cw_ref[0:1, cols] * ubuf_ref[halo - 2:halo - 2 + tm, 0:width]
        return y

    for c0 in range(0, f, FFN_CHUNK):
        width = min(FFN_CHUNK, f - c0)
        gate = conv(c0, width)
        val = conv(f + c0, width)
        hid_ref[:, c0:c0 + width] = (_gelu_tanh(gate) * val).astype(_BF16)

    y = jnp.dot(hid_ref[...], wdn_ref[...], preferred_element_type=_F32)
    o_ref[...] = x + _rms(y, gpost_ref[...])


def _conv_ffn(x, g_pre, w_up, conv_w, conv_b, w_down, g_post, seq_len):
    t, d = x.shape
    f = w_down.shape[0]
    tm = min(ROW_TILE, seq_len)
    return pl.pallas_call(
        functools.partial(_ffn_kernel, tiles_per_seq=seq_len // tm),
        grid=(t // tm,),
        in_specs=[pl.BlockSpec((tm, d), lambda i: (i, 0)), _resident((1, d)), _resident((d, 2 * f)),
                  _resident((CONV_WIDTH, 2 * f)), _resident((1, 2 * f)), _resident((f, d)),
                  _resident((1, d))],
        out_specs=pl.BlockSpec((tm, d), lambda i: (i, 0)),
        out_shape=jax.ShapeDtypeStruct((t, d), _F32),
        scratch_shapes=[pltpu.VMEM((SUBLANES, 2 * f), _F32),
                        pltpu.VMEM((tm + SUBLANES, FFN_CHUNK), _F32),
                        pltpu.VMEM((tm, f), _BF16)],
        compiler_params=_params("arbitrary"), name="conv_ffn",
    )(x, g_pre.reshape(1, d), w_up, conv_w, conv_b.reshape(1, 2 * f), w_down, g_post.reshape(1, d))


def _decay_kernel(f_ref, bias_ref, place_ref, cq_ref, ck_ref, carry_ref):
    tm = f_ref.shape[1]

    @pl.when(pl.program_id(1) == 0)
    def _():
        carry_ref[...] = jnp.zeros_like(carry_ref)

    row = lax.broadcasted_iota(jnp.int32, (tm, tm), 0)
    col = lax.broadcasted_iota(jnp.int32, (tm, tm), 1)
    tril = (row >= col).astype(_F32)
    lane = lax.broadcasted_iota(jnp.int32, (tm, LANES), 1)
    lf = jnp.where(lane < FOX_HEADS, _log_sigmoid(f_ref[0] + bias_ref[...]), 0.0)
    c = carry_ref[0:1, :] + jnp.dot(tril, lf, preferred_element_type=_F32, precision=lax.Precision.HIGHEST)
    carry_ref[...] = jnp.broadcast_to(c[tm - 1:tm, :], carry_ref.shape)
    c = c * LOG2E

    packed = jnp.where(lane < FOX_HEADS, 1.0, 0.0)
    packed = pltpu.roll(packed, DECAY_PARTS * FOX_HEADS, axis=1)
    rest = c
    for j in range(DECAY_PARTS):
        part = rest.astype(_BF16).astype(_F32)
        rest = rest - part
        packed = packed + (part if j == 0 else pltpu.roll(part, j * FOX_HEADS, axis=1))
    placed = jnp.dot(packed.astype(_BF16), place_ref[...], preferred_element_type=_F32)
    width = cq_ref.shape[2]
    cq_ref[0] = placed[:, :width].astype(cq_ref.dtype)
    ck_ref[0] = placed[:, width:].astype(ck_ref.dtype)


def _decay_placement():
    nh, dh, parts = FOX_HEADS, FOX_HEAD_DIM, DECAY_PARTS
    width = nh * dh
    heads = jnp.arange(nh)
    base = (heads // 2) * 2 * dh + (1 - heads % 2) * dh
    ones_row = parts * nh + heads
    place = jnp.zeros((LANES, 2 * width), _F32)
    for j in range(parts):
        place = place.at[j * nh + heads, base + j].set(1.0)
        place = place.at[ones_row, base + parts + j].set(1.0)
        place = place.at[ones_row, width + base + j].set(1.0)
        place = place.at[j * nh + heads, width + base + parts + j].set(-1.0)
    return place.astype(_BF16)


def _fox_decay(f_pre, bias):
    b, s, _ = f_pre.shape
    nh, dh = FOX_HEADS, FOX_HEAD_DIM
    tm = min(ROW_TILE, s)
    bias_row = jnp.zeros((1, LANES), _F32).at[0, :nh].set(bias)
    out = jax.ShapeDtypeStruct((b, s, nh * dh), _BF16)
    return pl.pallas_call(
        _decay_kernel,
        grid=(b, s // tm),
        in_specs=[pl.BlockSpec((1, tm, LANES), lambda bi, ti: (bi, ti, 0)), _resident((1, LANES)),
                  _resident((LANES, 2 * nh * dh))],
        out_specs=[pl.BlockSpec((1, tm, nh * dh), lambda bi, ti: (bi, ti, 0))] * 2,
        out_shape=[out, out],
        scratch_shapes=[pltpu.VMEM((SUBLANES, LANES), _F32)],
        compiler_params=_params("parallel", "arbitrary"), name="fox_decay",
    )(f_pre, bias_row, _decay_placement())


def _attn_kernel(q_ref, cq_ref, g_ref, k_ref, ck_ref, vt_ref, o_ref,
                 sa_ref, sb_ref, sa_max_ref, sb_max_ref, m_ref, acc_ref):
    tq = q_ref.shape[1]
    tk = sa_ref.shape[1]
    dh = FOX_HEAD_DIM
    qi = pl.program_id(2)
    first_q = lax.broadcasted_iota(jnp.int32, (tq, 2 * dh), 1) < dh
    first_k = lax.broadcasted_iota(jnp.int32, (tk, 2 * dh), 1) < dh
    q_pair, cq_pair = q_ref[0], cq_ref[0]
    q_aug = (jnp.where(first_q, q_pair, cq_pair), jnp.where(first_q, cq_pair, q_pair))

    ones_blk = jnp.ones((ONES_ROWS, tk), _BF16)

    m_ref[...] = jnp.full_like(m_ref, -jnp.inf)
    acc_ref[...] = jnp.zeros_like(acc_ref)

    def causal_square(s):
        key = lax.broadcasted_iota(jnp.int32, (tk, tk), 0)
        qry = lax.broadcasted_iota(jnp.int32, (tk, tk), 1)
        square = jnp.where(key <= qry, s[:, :tk], -jnp.inf)
        return square if s.shape[1] == tk else jnp.concatenate([square, s[:, tk:]], axis=1)

    def scores(j, bufs, q0, diagonal):
        s_ref, smax_ref = bufs
        start = pl.multiple_of(j * tk, tk)
        k_pair = k_ref[0, pl.ds(start, tk), :]
        ck_pair = ck_ref[0, pl.ds(start, tk), :]
        k_aug = (jnp.where(first_k, k_pair, ck_pair), jnp.where(first_k, ck_pair, k_pair))
        for hh in range(2):
            s = lax.dot_general(k_aug[hh], q_aug[hh][q0:], (((1,), (1,)), ((), ())),
                                preferred_element_type=_F32)
            if diagonal:
                s = causal_square(s)
            s_ref[hh, :, q0:] = s
            smax_ref[hh, :, q0:] = jnp.max(s, axis=0, keepdims=True)

    def accumulate(j, bufs, q0):
        s_ref, smax_ref = bufs
        start = pl.multiple_of(j * tk, tk)
        for hh in range(2):
            m_old = m_ref[hh, :, q0:]
            m_new = jnp.maximum(m_old, smax_ref[hh, :, q0:])
            alpha = jnp.exp2(m_old - m_new)
            p = jnp.exp2(s_ref[hh, :, q0:] - m_new).astype(_BF16)
            vt = jnp.concatenate([vt_ref[hh * dh:(hh + 1) * dh, pl.ds(start, tk)], ones_blk], axis=0)
            acc_ref[hh, :, q0:] = (alpha * acc_ref[hh, :, q0:]
                                   + jnp.dot(vt, p, preferred_element_type=_F32))
            m_ref[hh, :, q0:] = m_new

    per_tile = tq // tk
    buf_a, buf_b = (sa_ref, sa_max_ref), (sb_ref, sb_max_ref)

    scores(0, buf_a, 0, False)

    def body(i, carry):
        for u in range(per_tile):
            j = i * per_tile + u
            cur, nxt = (buf_a, buf_b) if u % 2 == 0 else (buf_b, buf_a)
            scores(j + 1, nxt, 0, False)
            accumulate(j, cur, 0)
        return carry

    lax.fori_loop(0, qi, body, 0)
    for u in range(per_tile):
        j = qi * per_tile + u
        cur, nxt = (buf_a, buf_b) if u % 2 == 0 else (buf_b, buf_a)
        if u + 1 < per_tile:
            scores(j + 1, nxt, (u + 1) * tk, True)
        if u == 0:
            for hh in range(2):
                square = causal_square(sa_ref[hh, :, :tk])
                sa_ref[hh, :, :tk] = square
                sa_max_ref[hh, :, :tk] = jnp.max(square, axis=0, keepdims=True)
        accumulate(j, cur, u * tk)

    halves = []
    for hh in range(2):
        acc = acc_ref[hh]
        halves.append(acc[0:dh, :] / acc[dh:dh + 1, :])
    out = jnp.concatenate(halves, axis=0).T
    o_ref[0] = (out * jax.nn.sigmoid(g_ref[0].astype(_F32))).astype(o_ref.dtype)


def _fox_attention(q, cq, g, k, ck, vt):
    b, s, hd = q.shape
    pairs = hd // LANES
    tq = min(ATTN_TILE, s)
    tk = min(ATTN_KEYS, tq // 2)
    rows = FOX_HEAD_DIM + ONES_ROWS
    qspec = pl.BlockSpec((1, tq, LANES), lambda bi, pi, qi: (bi, qi, pi))
    kspec = pl.BlockSpec((1, s, LANES), lambda bi, pi, qi: (bi, 0, pi))
    vspec = pl.BlockSpec((LANES, s), lambda bi, pi, qi: (pi, bi))
    return pl.pallas_call(
        _attn_kernel,
        grid=(b, pairs, s // tq),
        in_specs=[qspec, qspec, qspec, kspec, kspec, vspec],
        out_specs=qspec,
        out_shape=jax.ShapeDtypeStruct((b, s, hd), _BF16),
        scratch_shapes=[pltpu.VMEM((2, tk, tq), _F32), pltpu.VMEM((2, tk, tq), _F32),
                        pltpu.VMEM((2, 1, tq), _F32), pltpu.VMEM((2, 1, tq), _F32),
                        pltpu.VMEM((2, 1, tq), _F32), pltpu.VMEM((2, rows, tq), _F32)],
        compiler_params=_params("parallel", "parallel", "arbitrary"), name="fox_attention",
    )(q, cq, g, k, ck, vt)


def _pad_cols(w, width):
    return jnp.pad(w, ((0, 0), (0, width - w.shape[1])))


def kernel(x, norm_mix_pre, norm_mix_post, norm_ffn_pre, norm_ffn_post, mlstm_w_in, mlstm_b_gate,
           mlstm_norm, mlstm_w_out, kv_norm, kv_w, kv_b_f, fox_w_qg, fox_w_out, ffn_w_up,
           ffn_conv_w, ffn_conv_b, ffn_w_down):
    b, s, d = x.shape
    depth = norm_mix_pre.shape[0]
    n_a = mlstm_w_in.shape[0]
    xt = x.reshape(b * s, d)
    hk = MLSTM_HEADS * MLSTM_QK_DIM
    hv = MLSTM_HEADS * MLSTM_V_DIM
    k_sh = vt_sh = cq_sh = ck_sh = None
    for layer in range(depth):
        if layer < n_a:
            w_in = mlstm_w_in[layer]
            weights = [(w_in[:, :hk] * MLSTM_QK_DIM ** -0.5).astype(_BF16),
                       w_in[:, hk:2 * hk].astype(_BF16),
                       w_in[:, 2 * hk:2 * hk + hv].T.astype(_BF16),
                       w_in[:, 2 * hk + hv:2 * hk + 2 * hv].astype(_BF16),
                       _pad_cols(w_in[:, 2 * hk + 2 * hv:], LANES).astype(_BF16)]
            q, k, vt, o, gates = _norm_proj(xt, norm_mix_pre[layer], weights,
                                            [_BF16, _BF16, _BF16, _BF16, _F32], "mlstm_in_proj",
                                            transposed=(False, False, True, False, False))
            seq = lambda a: a.reshape(b, s, a.shape[-1])
            hs = _mlstm(seq(q), seq(k), vt, seq(o), seq(gates), mlstm_b_gate[layer], mlstm_norm[layer])
            xt = _proj_residual(hs.reshape(b * s, hv), mlstm_w_out[layer].astype(_BF16), xt,
                                norm_mix_post[layer], "mlstm_out_proj")
        else:
            j = layer - n_a
            w_qg = fox_w_qg[j]
            weights = [(w_qg[:, :d] * (FOX_HEAD_DIM ** -0.5 * LOG2E)).astype(_BF16),
                       w_qg[:, d:].astype(_BF16)]
            q, g = _norm_proj(xt, norm_mix_pre[layer], weights, [_BF16, _BF16], "fox_in_proj")
            seq = lambda a: a.reshape(b, s, a.shape[-1])
            att = _fox_attention(seq(q), cq_sh, seq(g), k_sh, ck_sh, vt_sh)
            xt = _proj_residual(att.reshape(b * s, d), fox_w_out[j].astype(_BF16), xt,
                                norm_mix_post[layer], "fox_out_proj")
        xt = _conv_ffn(xt, norm_ffn_pre[layer], ffn_w_up[layer].astype(_BF16), ffn_conv_w[layer],
                       ffn_conv_b[layer], ffn_w_down[layer].astype(_BF16), norm_ffn_post[layer], s)
        if layer == n_a - 1:
            weights = [kv_w[:, :d].astype(_BF16), kv_w[:, d:2 * d].T.astype(_BF16),
                       _pad_cols(kv_w[:, 2 * d:], LANES).astype(_BF16)]
            k2, vt_sh, f_pre = _norm_proj(xt, kv_norm, weights, [_BF16, _BF16, _F32], "kv_proj",
                                          transposed=(False, True, False))
            k_sh = k2.reshape(b, s, d)
            cq_sh, ck_sh = _fox_decay(f_pre.reshape(b, s, LANES), kv_b_f)
    return xt.reshape(b, s, d)
```

```python
import functools
import math

import jax
import jax.numpy as jnp
from jax import lax
from jax.experimental import pallas as pl
from jax.experimental.pallas import tpu as pltpu

EPS = 1e-6
MLSTM_HEADS = 8
MLSTM_QK_DIM = 64
MLSTM_V_DIM = 128
FOX_HEADS = 16
FOX_HEAD_DIM = 64
CONV_WIDTH = 3

LANES = 128
SUBLANES = 8
VMEM_LIMIT_BYTES = 56 * 1024 * 1024
ROW_TILE = 512
MLSTM_CHUNK = 128
ATTN_TILE = 1024
ATTN_KEYS = 256
FFN_CHUNK = 256
DECAY_PARTS = 3
LOG2E = math.log2(math.e)

_BF16 = jnp.bfloat16
_F32 = jnp.float32


def _params(*semantics):
    return pltpu.CompilerParams(dimension_semantics=semantics, vmem_limit_bytes=VMEM_LIMIT_BYTES)


def _resident(shape):
    return pl.BlockSpec(shape, lambda *_: (0,) * len(shape), pipeline_mode=pl.Buffered(1))


def _rms(x, gain):
    return x * lax.rsqrt(jnp.mean(x * x, axis=-1, keepdims=True) + EPS) * gain


def _norm_proj_kernel(x_ref, g_ref, *refs, transposed):
    n_out = len(transposed)
    w_refs, o_refs = refs[:n_out], refs[n_out:]
    h = _rms(x_ref[...], g_ref[...]).astype(_BF16)
    for w_ref, o_ref, tr in zip(w_refs, o_refs, transposed):
        n = w_ref.shape[0] if tr else w_ref.shape[1]
        for c0 in range(0, n, 512):
            c1 = min(c0 + 512, n)
            if tr:
                y = lax.dot_general(w_ref[c0:c1, :], h, (((1,), (1,)), ((), ())),
                                    preferred_element_type=_F32)
                o_ref[c0:c1, :] = y.astype(o_ref.dtype)
            else:
                y = jnp.dot(h, w_ref[:, c0:c1], preferred_element_type=_F32)
                o_ref[:, c0:c1] = y.astype(o_ref.dtype)


def _norm_proj(x, gain, weights, out_dtypes, name, transposed=None):
    t, d = x.shape
    tm = min(ROW_TILE, t)
    transposed = tuple(transposed or (False,) * len(weights))
    in_specs = [pl.BlockSpec((tm, d), lambda i: (i, 0)), _resident((1, d))]
    in_specs += [_resident(w.shape) for w in weights]
    out_specs, out_shape = [], []
    for w, dt, tr in zip(weights, out_dtypes, transposed):
        if tr:
            out_specs.append(pl.BlockSpec((w.shape[0], tm), lambda i: (0, i)))
            out_shape.append(jax.ShapeDtypeStruct((w.shape[0], t), dt))
        else:
            out_specs.append(pl.BlockSpec((tm, w.shape[1]), lambda i: (i, 0)))
            out_shape.append(jax.ShapeDtypeStruct((t, w.shape[1]), dt))
    return pl.pallas_call(
        functools.partial(_norm_proj_kernel, transposed=transposed),
        grid=(t // tm,), in_specs=in_specs, out_specs=out_specs, out_shape=out_shape,
        compiler_params=_params("parallel"), name=name,
    )(x, gain.reshape(1, d), *weights)


def _proj_residual_kernel(a_ref, w_ref, x_ref, g_ref, o_ref):
    y = jnp.dot(a_ref[...], w_ref[...], preferred_element_type=_F32)
    o_ref[...] = x_ref[...] + _rms(y, g_ref[...])


def _proj_residual(a, w, x, gain, name):
    t, d = x.shape
    k = a.shape[1]
    tm = min(ROW_TILE, t)
    return pl.pallas_call(
        _proj_residual_kernel,
        grid=(t // tm,),
        in_specs=[pl.BlockSpec((tm, k), lambda i: (i, 0)), _resident((k, d)),
                  pl.BlockSpec((tm, d), lambda i: (i, 0)), _resident((1, d))],
        out_specs=pl.BlockSpec((tm, d), lambda i: (i, 0)),
        out_shape=jax.ShapeDtypeStruct((t, d), _F32),
        compiler_params=_params("parallel"), name=name,
    )(a, w, x, gain.reshape(1, d))


def _log_sigmoid(z):
    return jnp.minimum(z, 0.0) - jnp.log1p(jnp.exp(-jnp.abs(z)))


ONES_ROWS = 16


def _mlstm_kernel(q_ref, k_ref, vt_ref, o_ref, gc_ref, gr_ref, bc_ref, br_ref, hn_ref,
                  out_ref, ct_ref, m_ref):
    nh, dk, dv = MLSTM_HEADS, MLSTM_QK_DIM, MLSTM_V_DIM
    L = q_ref.shape[1]

    @pl.when(pl.program_id(1) == 0)
    def _():
        ct_ref[...] = jnp.zeros_like(ct_ref)
        m_ref[...] = jnp.zeros_like(m_ref)

    row = lax.broadcasted_iota(jnp.int32, (L, L), 0)
    col = lax.broadcasted_iota(jnp.int32, (L, L), 1)
    source_before_target = row <= col
    tril = (row >= col).astype(_F32)
    triu = source_before_target.astype(_F32)

    gc = gc_ref[0] + bc_ref[...]
    gr = gr_ref[0] + br_ref[:, 0:1]
    lane_g = lax.broadcasted_iota(jnp.int32, gc.shape, 1)
    sub_g = lax.broadcasted_iota(jnp.int32, gr.shape, 0)
    lf_c = jnp.where(lane_g >= nh, _log_sigmoid(gc), 0.0)
    lf_r = jnp.where(sub_g >= nh, _log_sigmoid(gr), 0.0)
    b_c_all = jnp.dot(tril, lf_c, preferred_element_type=_F32, precision=lax.Precision.HIGHEST)
    b_r_all = jnp.dot(lf_r, triu, preferred_element_type=_F32, precision=lax.Precision.HIGHEST)

    lane_qk = lax.broadcasted_iota(jnp.int32, (L, 2 * dk), 1)
    ones_blk = jnp.ones((ONES_ROWS, L), _BF16)
    for p in range(nh // 2):
        qp = q_ref[0, :, p * 2 * dk:(p + 1) * 2 * dk]
        kp = k_ref[0, :, p * 2 * dk:(p + 1) * 2 * dk]
        for hh in range(2):
            h = 2 * p + hh
            mine = (lane_qk < dk) if hh == 0 else (lane_qk >= dk)
            qm = jnp.where(mine, qp, jnp.zeros_like(qp))
            ct = ct_ref[h]
            both = lax.dot_general(jnp.concatenate([kp, ct.astype(_BF16)], axis=0), qm,
                                   (((1,), (1,)), ((), ())), preferred_element_type=_F32)
            s_qk, inter = both[:L], both[L:]
            g_col = gc[:, h:h + 1] - b_c_all[:, nh + h:nh + h + 1]
            b_row = b_r_all[nh + h:nh + h + 1, :]
            i_row = gr[h:h + 1, :]
            m_prev = m_ref[h:h + 1, 0:1]
            dmat = jnp.where(source_before_target, g_col + b_row, -jnp.inf)
            m_inter = b_row + m_prev
            m_t = jnp.maximum(m_inter, jnp.max(dmat, axis=0, keepdims=True))
            sd = (s_qk * jnp.exp(dmat - m_t)).astype(_BF16)
            vt_aug = jnp.concatenate([vt_ref[h * dv:(h + 1) * dv, :], ones_blk], axis=0)
            tot = jnp.dot(vt_aug, sd, preferred_element_type=_F32) + jnp.exp(m_inter - m_t) * inter
            num, den = tot[:dv], tot[dv:dv + 1]
            hv = num / jnp.maximum(jnp.abs(den), jnp.exp(-m_t))
            hv = hv * lax.rsqrt(jnp.mean(hv * hv, axis=0, keepdims=True) + EPS)
            hv = (hv * hn_ref[h * dv:(h + 1) * dv, :]).T
            hv = hv * jax.nn.sigmoid(o_ref[0, :, h * dv:(h + 1) * dv].astype(_F32))
            out_ref[0, :, h * dv:(h + 1) * dv] = hv.astype(out_ref.dtype)

            b_last = b_row[:, L - 1:L]
            g_row = b_last - b_row + i_row
            m_new = jnp.maximum(b_last + m_prev, jnp.max(g_row, axis=-1, keepdims=True))
            decay = jnp.exp(b_last + m_prev - m_new)
            vw = (vt_aug.astype(_F32) * jnp.exp(g_row - m_new)).astype(_BF16)
            ct_ref[h] = decay * ct + jnp.dot(vw, kp, preferred_element_type=_F32)
            m_ref[h:h + 1, :] = jnp.broadcast_to(m_new, (1, m_ref.shape[1]))


def _mlstm(q, k, vt, o, gates, b_gate, head_norm):
    b, s, _ = q.shape
    nh, dk, dv = MLSTM_HEADS, MLSTM_QK_DIM, MLSTM_V_DIM
    L = min(MLSTM_CHUNK, s)
    chunks = s // L
    gates_row = jnp.swapaxes(gates[:, :, :2 * nh], 1, 2)
    bias_col = jnp.zeros((1, LANES), _F32).at[0, :2 * nh].set(b_gate)
    bias_row = b_gate.reshape(2 * nh, 1)
    norm_cols = jnp.broadcast_to(head_norm.astype(_F32)[:, None], (nh * dv, L))
    tok = lambda width: pl.BlockSpec((1, L, width), lambda bi, ci: (bi, ci, 0))
    return pl.pallas_call(
        _mlstm_kernel,
        grid=(b, chunks),
        in_specs=[tok(nh * dk), tok(nh * dk),
                  pl.BlockSpec((nh * dv, L), lambda bi, ci: (0, bi * chunks + ci)),
                  tok(nh * dv), tok(LANES),
                  pl.BlockSpec((1, 2 * nh, L), lambda bi, ci: (bi, 0, ci)),
                  _resident((1, LANES)), _resident((2 * nh, 1)), _resident((nh * dv, L))],
        out_specs=tok(nh * dv),
        out_shape=jax.ShapeDtypeStruct((b, s, nh * dv), _BF16),
        scratch_shapes=[pltpu.VMEM((nh, dv + ONES_ROWS, 2 * dk), _F32), pltpu.VMEM((nh, LANES), _F32)],
        compiler_params=_params("parallel", "arbitrary"), name="mlstm_chunk",
    )(q, k, vt, o, gates, gates_row, bias_col, bias_row, norm_cols)


GELU_C = math.sqrt(2.0 / math.pi)
RUN_PITCH = 72


def _ffn_kernel(x_ref, gpre_ref, wup_ref, cw_ref, cb_ref, wdn_ref, gpost_ref, o_ref,
                carry_ref, slab_ref, hid_ref, *, tiles_per_seq):
    tm, d = x_ref.shape
    f = wdn_ref.shape[0]
    run = tm // SUBLANES
    n_slabs = d // LANES

    @pl.when(pl.program_id(0) % tiles_per_seq == 0)
    def _():
        carry_ref[...] = jnp.zeros_like(carry_ref)

    def gather_rows(src_ref):
        for c in range(n_slabs):
            for r in range(SUBLANES):
                slab_ref[c, r * RUN_PITCH:r * RUN_PITCH + run, :] = (
                    src_ref[r * run:(r + 1) * run, c * LANES:(c + 1) * LANES])
        return jnp.concatenate(
            [jnp.concatenate([slab_ref[c, pl.ds(k, SUBLANES, stride=RUN_PITCH), :] for k in range(run)],
                             axis=0) for c in range(n_slabs)], axis=1)

    def scatter_rows(value, dst_ref):
        for c in range(n_slabs):
            for k in range(run):
                slab_ref[c, pl.ds(k, SUBLANES, stride=RUN_PITCH), :] = (
                    value[k * SUBLANES:(k + 1) * SUBLANES, c * LANES:(c + 1) * LANES])
            for r in range(SUBLANES):
                dst_ref[r * run:(r + 1) * run, c * LANES:(c + 1) * LANES] = (
                    slab_ref[c, r * RUN_PITCH:r * RUN_PITCH + run, :])

    x = gather_rows(x_ref)
    h = _rms(x, gpre_ref[...]).astype(_BF16)
    first_sublane = lax.broadcasted_iota(jnp.int32, (SUBLANES, FFN_CHUNK), 0) == 0

    def conv(c0, width):
        cols = slice(c0, c0 + width)
        u = jnp.dot(h, wup_ref[:, cols], preferred_element_type=_F32)
        prev = carry_ref[:, cols]
        tail1, tail2 = u[tm - SUBLANES:], u[tm - 2 * SUBLANES:tm - SUBLANES]
        carry_ref[:, cols] = u[tm - 2 * SUBLANES:]
        mask = first_sublane[:, :width]
        wrap1 = jnp.where(mask, prev[2 * SUBLANES - 1:, :], pltpu.roll(tail1, 1, axis=0))
        wrap2 = jnp.where(mask, prev[SUBLANES - 1:SUBLANES, :], pltpu.roll(tail2, 1, axis=0))
        u1 = jnp.concatenate([wrap1, u[:tm - SUBLANES]], axis=0)
        u2 = jnp.concatenate([wrap2, wrap1, u[:tm - 2 * SUBLANES]], axis=0)
        return (cb_ref[:, cols] + cw_ref[2:3, cols] * u + cw_ref[1:2, cols] * u1
                + cw_ref[0:1, cols] * u2)

    for c0 in range(0, f, FFN_CHUNK):
        width = min(FFN_CHUNK, f - c0)
        gate = conv(c0, width)
        half_val = conv(f + c0, width)
        inner = gate * ((gate * gate) * (GELU_C * 0.044715) + GELU_C)
        w = gate * half_val
        hid_ref[:, c0:c0 + width] = (w * jnp.tanh(inner) + w).astype(_BF16)

    y = jnp.dot(hid_ref[...], wdn_ref[...], preferred_element_type=_F32)
    scatter_rows(x + _rms(y, gpost_ref[...]), o_ref)


def _conv_ffn(x, g_pre, w_up, conv_w, conv_b, w_down, g_post, seq_len):
    t, d = x.shape
    f = w_down.shape[0]
    tm = min(ROW_TILE, seq_len)
    half = jnp.concatenate([jnp.ones((f,), _F32), jnp.full((f,), 0.5, _F32)])
    return pl.pallas_call(
        functools.partial(_ffn_kernel, tiles_per_seq=seq_len // tm),
        grid=(t // tm,),
        in_specs=[pl.BlockSpec((tm, d), lambda i: (i, 0)), _resident((1, d)), _resident((d, 2 * f)),
                  _resident((CONV_WIDTH, 2 * f)), _resident((1, 2 * f)), _resident((f, d)),
                  _resident((1, d))],
        out_specs=pl.BlockSpec((tm, d), lambda i: (i, 0)),
        out_shape=jax.ShapeDtypeStruct((t, d), _F32),
        scratch_shapes=[pltpu.VMEM((2 * SUBLANES, 2 * f), _F32),
                        pltpu.VMEM((d // LANES, SUBLANES * RUN_PITCH, LANES), _F32),
                        pltpu.VMEM((tm, f), _BF16)],
        compiler_params=_params("arbitrary"), name="conv_ffn",
    )(x, g_pre.reshape(1, d), w_up, conv_w * half, (conv_b * half).reshape(1, 2 * f), w_down,
      g_post.reshape(1, d))


def _decay_kernel(f_ref, bias_ref, place_ref, cq_ref, ck_ref, carry_ref):
    tm = f_ref.shape[1]

    @pl.when(pl.program_id(1) == 0)
    def _():
        carry_ref[...] = jnp.zeros_like(carry_ref)

    row = lax.broadcasted_iota(jnp.int32, (tm, tm), 0)
    col = lax.broadcasted_iota(jnp.int32, (tm, tm), 1)
    tril = (row >= col).astype(_F32)
    lane = lax.broadcasted_iota(jnp.int32, (tm, LANES), 1)
    lf = jnp.where(lane < FOX_HEADS, _log_sigmoid(f_ref[0] + bias_ref[...]), 0.0)
    c = carry_ref[0:1, :] + jnp.dot(tril, lf, preferred_element_type=_F32, precision=lax.Precision.HIGHEST)
    carry_ref[...] = jnp.broadcast_to(c[tm - 1:tm, :], carry_ref.shape)
    c = c * LOG2E

    packed = jnp.where(lane < FOX_HEADS, 1.0, 0.0)
    packed = pltpu.roll(packed, DECAY_PARTS * FOX_HEADS, axis=1)
    rest = c
    for j in range(DECAY_PARTS):
        part = rest.astype(_BF16).astype(_F32)
        rest = rest - part
        packed = packed + (part if j == 0 else pltpu.roll(part, j * FOX_HEADS, axis=1))
    placed = jnp.dot(packed.astype(_BF16), place_ref[...], preferred_element_type=_F32)
    width = cq_ref.shape[2]
    cq_ref[0] = placed[:, :width].astype(cq_ref.dtype)
    ck_ref[0] = placed[:, width:].astype(ck_ref.dtype)


def _decay_placement():
    nh, dh, parts = FOX_HEADS, FOX_HEAD_DIM, DECAY_PARTS
    width = nh * dh
    heads = jnp.arange(nh)
    base = (heads // 2) * 2 * dh + (1 - heads % 2) * dh
    ones_row = parts * nh + heads
    place = jnp.zeros((LANES, 2 * width), _F32)
    for j in range(parts):
        place = place.at[j * nh + heads, base + j].set(1.0)
        place = place.at[ones_row, base + parts + j].set(1.0)
        place = place.at[ones_row, width + base + j].set(1.0)
        place = place.at[j * nh + heads, width + base + parts + j].set(-1.0)
    return place.astype(_BF16)


def _fox_decay(f_pre, bias):
    b, s, _ = f_pre.shape
    nh, dh = FOX_HEADS, FOX_HEAD_DIM
    tm = min(ROW_TILE, s)
    bias_row = jnp.zeros((1, LANES), _F32).at[0, :nh].set(bias)
    out = jax.ShapeDtypeStruct((b, s, nh * dh), _BF16)
    return pl.pallas_call(
        _decay_kernel,
        grid=(b, s // tm),
        in_specs=[pl.BlockSpec((1, tm, LANES), lambda bi, ti: (bi, ti, 0)), _resident((1, LANES)),
                  _resident((LANES, 2 * nh * dh))],
        out_specs=[pl.BlockSpec((1, tm, nh * dh), lambda bi, ti: (bi, ti, 0))] * 2,
        out_shape=[out, out],
        scratch_shapes=[pltpu.VMEM((SUBLANES, LANES), _F32)],
        compiler_params=_params("parallel", "arbitrary"), name="fox_decay",
    )(f_pre, bias_row, _decay_placement())


def _attn_kernel(q_ref, cq_ref, g_ref, k_ref, ck_ref, vt_ref, o_ref,
                 sa_ref, sb_ref, sa_max_ref, sb_max_ref, m_ref, acc_ref):
    tq = q_ref.shape[1]
    tk = sa_ref.shape[1]
    dh = FOX_HEAD_DIM
    qi = pl.program_id(2)
    first_q = lax.broadcasted_iota(jnp.int32, (tq, 2 * dh), 1) < dh
    first_k = lax.broadcasted_iota(jnp.int32, (tk, 2 * dh), 1) < dh
    q_pair, cq_pair = q_ref[0], cq_ref[0]
    q_aug = (jnp.where(first_q, q_pair, cq_pair), jnp.where(first_q, cq_pair, q_pair))

    ones_blk = jnp.ones((ONES_ROWS, tk), _BF16)

    m_ref[...] = jnp.full_like(m_ref, -jnp.inf)
    acc_ref[...] = jnp.zeros_like(acc_ref)

    def causal_square(s):
        key = lax.broadcasted_iota(jnp.int32, (tk, tk), 0)
        qry = lax.broadcasted_iota(jnp.int32, (tk, tk), 1)
        square = jnp.where(key <= qry, s[:, :tk], -jnp.inf)
        return square if s.shape[1] == tk else jnp.concatenate([square, s[:, tk:]], axis=1)

    def scores(j, bufs, q0, diagonal):
        s_ref, smax_ref = bufs
        start = pl.multiple_of(j * tk, tk)
        k_pair = k_ref[0, pl.ds(start, tk), :]
        ck_pair = ck_ref[0, pl.ds(start, tk), :]
        k_aug = (jnp.where(first_k, k_pair, ck_pair), jnp.where(first_k, ck_pair, k_pair))
        for hh in range(2):
            s = lax.dot_general(k_aug[hh], q_aug[hh][q0:], (((1,), (1,)), ((), ())),
                                preferred_element_type=_F32)
            if diagonal:
                s = causal_square(s)
            s_ref[hh, :, q0:] = s
            smax_ref[hh, :, q0:] = jnp.max(s, axis=0, keepdims=True)

    def accumulate(j, bufs, q0):
        s_ref, smax_ref = bufs
        start = pl.multiple_of(j * tk, tk)
        for hh in range(2):
            m_old = m_ref[hh, :, q0:]
            m_new = jnp.maximum(m_old, smax_ref[hh, :, q0:])
            alpha = jnp.exp2(m_old - m_new)
            p = jnp.exp2(s_ref[hh, :, q0:] - m_new).astype(_BF16)
            vt = jnp.concatenate([vt_ref[hh * dh:(hh + 1) * dh, pl.ds(start, tk)], ones_blk], axis=0)
            acc_ref[hh, :, q0:] = (alpha * acc_ref[hh, :, q0:]
                                   + jnp.dot(vt, p, preferred_element_type=_F32))
            m_ref[hh, :, q0:] = m_new

    per_tile = tq // tk
    buf_a, buf_b = (sa_ref, sa_max_ref), (sb_ref, sb_max_ref)

    scores(0, buf_a, 0, False)

    def body(i, carry):
        for u in range(per_tile):
            j = i * per_tile + u
            cur, nxt = (buf_a, buf_b) if u % 2 == 0 else (buf_b, buf_a)
            scores(j + 1, nxt, 0, False)
            accumulate(j, cur, 0)
        return carry

    lax.fori_loop(0, qi, body, 0)
    for u in range(per_tile):
        j = qi * per_tile + u
        cur, nxt = (buf_a, buf_b) if u % 2 == 0 else (buf_b, buf_a)
        if u + 1 < per_tile:
            scores(j + 1, nxt, (u + 1) * tk, True)
        if u == 0:
            for hh in range(2):
                square = causal_square(sa_ref[hh, :, :tk])
                sa_ref[hh, :, :tk] = square
                sa_max_ref[hh, :, :tk] = jnp.max(square, axis=0, keepdims=True)
        accumulate(j, cur, u * tk)

    halves = []
    for hh in range(2):
        acc = acc_ref[hh]
        halves.append(acc[0:dh, :] / acc[dh:dh + 1, :])
    out = jnp.concatenate(halves, axis=0).T
    o_ref[0] = (out * jax.nn.sigmoid(g_ref[0].astype(_F32))).astype(o_ref.dtype)


def _fox_attention(q, cq, g, k, ck, vt):
    b, s, hd = q.shape
    pairs = hd // LANES
    tq = min(ATTN_TILE, s)
    tk = min(ATTN_KEYS, tq // 2)
    rows = FOX_HEAD_DIM + ONES_ROWS
    qspec = pl.BlockSpec((1, tq, LANES), lambda bi, pi, qi: (bi, qi, pi))
    kspec = pl.BlockSpec((1, s, LANES), lambda bi, pi, qi: (bi, 0, pi))
    vspec = pl.BlockSpec((LANES, s), lambda bi, pi, qi: (pi, bi))
    return pl.pallas_call(
        _attn_kernel,
        grid=(b, pairs, s // tq),
        in_specs=[qspec, qspec, qspec, kspec, kspec, vspec],
        out_specs=qspec,
        out_shape=jax.ShapeDtypeStruct((b, s, hd), _BF16),
        scratch_shapes=[pltpu.VMEM((2, tk, tq), _F32), pltpu.VMEM((2, tk, tq), _F32),
                        pltpu.VMEM((2, 1, tq), _F32), pltpu.VMEM((2, 1, tq), _F32),
                        pltpu.VMEM((2, 1, tq), _F32), pltpu.VMEM((2, rows, tq), _F32)],
        compiler_params=_params("parallel", "parallel", "arbitrary"), name="fox_attention",
    )(q, cq, g, k, ck, vt)


def _pad_cols(w, width):
    return jnp.pad(w, ((0, 0), (0, width - w.shape[1])))


def kernel(x, norm_mix_pre, norm_mix_post, norm_ffn_pre, norm_ffn_post, mlstm_w_in, mlstm_b_gate,
           mlstm_norm, mlstm_w_out, kv_norm, kv_w, kv_b_f, fox_w_qg, fox_w_out, ffn_w_up,
           ffn_conv_w, ffn_conv_b, ffn_w_down):
    b, s, d = x.shape
    depth = norm_mix_pre.shape[0]
    n_a = mlstm_w_in.shape[0]
    xt = x.reshape(b * s, d)
    hk = MLSTM_HEADS * MLSTM_QK_DIM
    hv = MLSTM_HEADS * MLSTM_V_DIM
    k_sh = vt_sh = cq_sh = ck_sh = None
    for layer in range(depth):
        if layer < n_a:
            w_in = mlstm_w_in[layer]
            weights = [(w_in[:, :hk] * MLSTM_QK_DIM ** -0.5).astype(_BF16),
                       w_in[:, hk:2 * hk].astype(_BF16),
                       w_in[:, 2 * hk:2 * hk + hv].T.astype(_BF16),
                       w_in[:, 2 * hk + hv:2 * hk + 2 * hv].astype(_BF16),
                       _pad_cols(w_in[:, 2 * hk + 2 * hv:], LANES).astype(_BF16)]
            q, k, vt, o, gates = _norm_proj(xt, norm_mix_pre[layer], weights,
                                            [_BF16, _BF16, _BF16, _BF16, _F32], "mlstm_in_proj",
                                            transposed=(False, False, True, False, False))
            seq = lambda a: a.reshape(b, s, a.shape[-1])
            hs = _mlstm(seq(q), seq(k), vt, seq(o), seq(gates), mlstm_b_gate[layer], mlstm_norm[layer])
            xt = _proj_residual(hs.reshape(b * s, hv), mlstm_w_out[layer].astype(_BF16), xt,
                                norm_mix_post[layer], "mlstm_out_proj")
        else:
            j = layer - n_a
            w_qg = fox_w_qg[j]
            weights = [(w_qg[:, :d] * (FOX_HEAD_DIM ** -0.5 * LOG2E)).astype(_BF16),
                       w_qg[:, d:].astype(_BF16)]
            q, g = _norm_proj(xt, norm_mix_pre[layer], weights, [_BF16, _BF16], "fox_in_proj")
            seq = lambda a: a.reshape(b, s, a.shape[-1])
            att = _fox_attention(seq(q), cq_sh, seq(g), k_sh, ck_sh, vt_sh)
            xt = _proj_residual(att.reshape(b * s, d), fox_w_out[j].astype(_BF16), xt,
                                norm_mix_post[layer], "fox_out_proj")
        xt = _conv_ffn(xt, norm_ffn_pre[layer], ffn_w_up[layer].astype(_BF16), ffn_conv_w[layer],
                       ffn_conv_b[layer], ffn_w_down[layer].astype(_BF16), norm_ffn_post[layer], s)
        if layer == n_a - 1:
            weights = [kv_w[:, :d].astype(_BF16), kv_w[:, d:2 * d].T.astype(_BF16),
                       _pad_cols(kv_w[:, 2 * d:], LANES).astype(_BF16)]
            k2, vt_sh, f_pre = _norm_proj(xt, kv_norm, weights, [_BF16, _BF16, _F32], "kv_proj",
                                          transposed=(False, True, False))
            k_sh = k2.reshape(b, s, d)
            cq_sh, ck_sh = _fox_decay(f_pre.reshape(b, s, LANES), kv_b_f)
    return xt.reshape(b, s, d)
```

```python
import functools
import math

import jax
import jax.numpy as jnp
from jax import lax
from jax.experimental import pallas as pl
from jax.experimental.pallas import tpu as pltpu

EPS = 1e-6
MLSTM_HEADS = 8
MLSTM_QK_DIM = 64
MLSTM_V_DIM = 128
FOX_HEADS = 16
FOX_HEAD_DIM = 64
CONV_WIDTH = 3

LANES = 128
SUBLANES = 8
VMEM_LIMIT_BYTES = 56 * 1024 * 1024
ROW_TILE = 512
MLSTM_CHUNK = 128
ATTN_TILE = 1024
ATTN_KEYS = 256
FFN_CHUNK = 256
DECAY_PARTS = 3
LOG2E = math.log2(math.e)

_BF16 = jnp.bfloat16
_F32 = jnp.float32


def _params(*semantics):
    return pltpu.CompilerParams(dimension_semantics=semantics, vmem_limit_bytes=VMEM_LIMIT_BYTES)


def _resident(shape):
    return pl.BlockSpec(shape, lambda *_: (0,) * len(shape), pipeline_mode=pl.Buffered(1))


def _rms(x, gain):
    return x * lax.rsqrt(jnp.mean(x * x, axis=-1, keepdims=True) + EPS) * gain


def _norm_proj_kernel(x_ref, g_ref, *refs, transposed):
    n_out = len(transposed)
    w_refs, o_refs = refs[:n_out], refs[n_out:]
    h = _rms(x_ref[...], g_ref[...]).astype(_BF16)
    for w_ref, o_ref, tr in zip(w_refs, o_refs, transposed):
        n = w_ref.shape[0] if tr else w_ref.shape[1]
        for c0 in range(0, n, 512):
            c1 = min(c0 + 512, n)
            if tr:
                y = lax.dot_general(w_ref[c0:c1, :], h, (((1,), (1,)), ((), ())),
                                    preferred_element_type=_F32)
                o_ref[c0:c1, :] = y.astype(o_ref.dtype)
            else:
                y = jnp.dot(h, w_ref[:, c0:c1], preferred_element_type=_F32)
                o_ref[:, c0:c1] = y.astype(o_ref.dtype)


def _norm_proj(x, gain, weights, out_dtypes, name, transposed=None):
    t, d = x.shape
    tm = min(ROW_TILE, t)
    transposed = tuple(transposed or (False,) * len(weights))
    in_specs = [pl.BlockSpec((tm, d), lambda i: (i, 0)), _resident((1, d))]
    in_specs += [_resident(w.shape) for w in weights]
    out_specs, out_shape = [], []
    for w, dt, tr in zip(weights, out_dtypes, transposed):
        if tr:
            out_specs.append(pl.BlockSpec((w.shape[0], tm), lambda i: (0, i)))
            out_shape.append(jax.ShapeDtypeStruct((w.shape[0], t), dt))
        else:
            out_specs.append(pl.BlockSpec((tm, w.shape[1]), lambda i: (i, 0)))
            out_shape.append(jax.ShapeDtypeStruct((t, w.shape[1]), dt))
    return pl.pallas_call(
        functools.partial(_norm_proj_kernel, transposed=transposed),
        grid=(t // tm,), in_specs=in_specs, out_specs=out_specs, out_shape=out_shape,
        compiler_params=_params("parallel"), name=name,
    )(x, gain.reshape(1, d), *weights)


def _log_sigmoid(z):
    return jnp.minimum(z, 0.0) - jnp.log1p(jnp.exp(-jnp.abs(z)))


ONES_ROWS = 16


def _mlstm_kernel(q_ref, k_ref, vt_ref, o_ref, gc_ref, gr_ref, bc_ref, br_ref, hn_ref,
                  out_ref, ct_ref, m_ref):
    nh, dk, dv = MLSTM_HEADS, MLSTM_QK_DIM, MLSTM_V_DIM
    L = q_ref.shape[1]

    @pl.when(pl.program_id(1) == 0)
    def _():
        ct_ref[...] = jnp.zeros_like(ct_ref)
        m_ref[...] = jnp.zeros_like(m_ref)

    row = lax.broadcasted_iota(jnp.int32, (L, L), 0)
    col = lax.broadcasted_iota(jnp.int32, (L, L), 1)
    source_before_target = row <= col
    tril = (row >= col).astype(_F32)
    triu = source_before_target.astype(_F32)

    gc = gc_ref[0] + bc_ref[...]
    gr = gr_ref[0] + br_ref[:, 0:1]
    lane_g = lax.broadcasted_iota(jnp.int32, gc.shape, 1)
    sub_g = lax.broadcasted_iota(jnp.int32, gr.shape, 0)
    lf_c = jnp.where(lane_g >= nh, _log_sigmoid(gc), 0.0)
    lf_r = jnp.where(sub_g >= nh, _log_sigmoid(gr), 0.0)
    b_c_all = jnp.dot(tril, lf_c, preferred_element_type=_F32, precision=lax.Precision.HIGHEST)
    b_r_all = jnp.dot(lf_r, triu, preferred_element_type=_F32, precision=lax.Precision.HIGHEST)

    lane_qk = lax.broadcasted_iota(jnp.int32, (L, 2 * dk), 1)
    ones_blk = jnp.ones((ONES_ROWS, L), _BF16)
    for p in range(nh // 2):
        qp = q_ref[0, :, p * 2 * dk:(p + 1) * 2 * dk]
        kp = k_ref[0, :, p * 2 * dk:(p + 1) * 2 * dk]
        for hh in range(2):
            h = 2 * p + hh
            mine = (lane_qk < dk) if hh == 0 else (lane_qk >= dk)
            qm = jnp.where(mine, qp, jnp.zeros_like(qp))
            ct = ct_ref[h]
            both = lax.dot_general(jnp.concatenate([kp, ct.astype(_BF16)], axis=0), qm,
                                   (((1,), (1,)), ((), ())), preferred_element_type=_F32)
            s_qk, inter = both[:L], both[L:]
            g_col = gc[:, h:h + 1] - b_c_all[:, nh + h:nh + h + 1]
            b_row = b_r_all[nh + h:nh + h + 1, :]
            i_row = gr[h:h + 1, :]
            m_prev = m_ref[h:h + 1, 0:1]
            dmat = jnp.where(source_before_target, g_col + b_row, -jnp.inf)
            m_inter = b_row + m_prev
            m_t = jnp.maximum(m_inter, jnp.max(dmat, axis=0, keepdims=True))
            sd = (s_qk * jnp.exp(dmat - m_t)).astype(_BF16)
            vt_aug = jnp.concatenate([vt_ref[h * dv:(h + 1) * dv, :], ones_blk], axis=0)
            tot = jnp.dot(vt_aug, sd, preferred_element_type=_F32) + jnp.exp(m_inter - m_t) * inter
            num, den = tot[:dv], tot[dv:dv + 1]
            hv = num / jnp.maximum(jnp.abs(den), jnp.exp(-m_t))
            hv = hv * lax.rsqrt(jnp.mean(hv * hv, axis=0, keepdims=True) + EPS)
            hv = (hv * hn_ref[h * dv:(h + 1) * dv, :]).T
            hv = hv * jax.nn.sigmoid(o_ref[0, :, h * dv:(h + 1) * dv].astype(_F32))
            out_ref[0, :, h * dv:(h + 1) * dv] = hv.astype(out_ref.dtype)

            b_last = b_row[:, L - 1:L]
            g_row = b_last - b_row + i_row
            m_new = jnp.maximum(b_last + m_prev, jnp.max(g_row, axis=-1, keepdims=True))
            decay = jnp.exp(b_last + m_prev - m_new)
            vw = (vt_aug.astype(_F32) * jnp.exp(g_row - m_new)).astype(_BF16)
            ct_ref[h] = decay * ct + jnp.dot(vw, kp, preferred_element_type=_F32)
            m_ref[h:h + 1, :] = jnp.broadcast_to(m_new, (1, m_ref.shape[1]))


def _mlstm(q, k, vt, o, gates, b_gate, head_norm):
    b, s, _ = q.shape
    nh, dk, dv = MLSTM_HEADS, MLSTM_QK_DIM, MLSTM_V_DIM
    L = min(MLSTM_CHUNK, s)
    chunks = s // L
    gates_row = jnp.swapaxes(gates[:, :, :2 * nh], 1, 2)
    bias_col = jnp.zeros((1, LANES), _F32).at[0, :2 * nh].set(b_gate)
    bias_row = b_gate.reshape(2 * nh, 1)
    norm_cols = jnp.broadcast_to(head_norm.astype(_F32)[:, None], (nh * dv, L))
    tok = lambda width: pl.BlockSpec((1, L, width), lambda bi, ci: (bi, ci, 0))
    return pl.pallas_call(
        _mlstm_kernel,
        grid=(b, chunks),
        in_specs=[tok(nh * dk), tok(nh * dk),
                  pl.BlockSpec((nh * dv, L), lambda bi, ci: (0, bi * chunks + ci)),
                  tok(nh * dv), tok(LANES),
                  pl.BlockSpec((1, 2 * nh, L), lambda bi, ci: (bi, 0, ci)),
                  _resident((1, LANES)), _resident((2 * nh, 1)), _resident((nh * dv, L))],
        out_specs=tok(nh * dv),
        out_shape=jax.ShapeDtypeStruct((b, s, nh * dv), _BF16),
        scratch_shapes=[pltpu.VMEM((nh, dv + ONES_ROWS, 2 * dk), _F32), pltpu.VMEM((nh, LANES), _F32)],
        compiler_params=_params("parallel", "arbitrary"), name="mlstm_chunk",
    )(q, k, vt, o, gates, gates_row, bias_col, bias_row, norm_cols)


GELU_C = math.sqrt(2.0 / math.pi)
RUN_PITCH = 72


def _mix_ffn_kernel(a0_ref, x0_ref, a_ref, x_ref, wo_ref, gmix_ref, gpre_ref, wup_ref, cw_ref, cb_ref,
                    wdn_ref, gpost_ref, o_ref,
                    h_ref, xp_ref, y_ref, carry_ref, slab_in_ref, slab_out_ref, hid_ref, *, tiles_per_seq):
    tm, d = x_ref.shape
    f = wdn_ref.shape[0]
    run = tm // SUBLANES
    n_slabs = d // LANES
    step = pl.program_id(0)

    def gather_rows(value):
        for c in range(n_slabs):
            for r in range(SUBLANES):
                slab_in_ref[c, r * RUN_PITCH:r * RUN_PITCH + run, :] = (
                    value[r * run:(r + 1) * run, c * LANES:(c + 1) * LANES])
        return jnp.concatenate(
            [jnp.concatenate([slab_in_ref[c, pl.ds(k, SUBLANES, stride=RUN_PITCH), :]
                              for k in range(run)], axis=0) for c in range(n_slabs)], axis=1)

    def scatter_rows(value, dst_ref):
        for c in range(n_slabs):
            for k in range(run):
                slab_out_ref[c, pl.ds(k, SUBLANES, stride=RUN_PITCH), :] = (
                    value[k * SUBLANES:(k + 1) * SUBLANES, c * LANES:(c + 1) * LANES])
            for r in range(SUBLANES):
                dst_ref[r * run:(r + 1) * run, c * LANES:(c + 1) * LANES] = (
                    slab_out_ref[c, r * RUN_PITCH:r * RUN_PITCH + run, :])

    def head(a_blk_ref, x_blk_ref, slot):
        mixed = jnp.dot(a_blk_ref[...], wo_ref[...], preferred_element_type=_F32)
        xp = gather_rows(x_blk_ref[...] + _rms(mixed, gmix_ref[...]))
        xp_ref[slot] = xp
        h_ref[...] = _rms(xp, gpre_ref[...]).astype(_BF16)

    @pl.when(step == 0)
    def _():
        y_ref[...] = jnp.zeros_like(y_ref)
        xp_ref[...] = jnp.zeros_like(xp_ref)
        head(a0_ref, x0_ref, 0)

    @pl.when(step % tiles_per_seq == 0)
    def _():
        carry_ref[...] = jnp.zeros_like(carry_ref)

    other = (step + 1) % 2

    scatter_rows(xp_ref[other] + _rms(y_ref[...], gpost_ref[...]), o_ref)

    h = h_ref[...]
    first_sublane = lax.broadcasted_iota(jnp.int32, (SUBLANES, FFN_CHUNK), 0) == 0

    def conv(c0, width):
        cols = slice(c0, c0 + width)
        u = jnp.dot(h, wup_ref[:, cols], preferred_element_type=_F32)
        prev = carry_ref[:, cols]
        tail1, tail2 = u[tm - SUBLANES:], u[tm - 2 * SUBLANES:tm - SUBLANES]
        carry_ref[:, cols] = u[tm - 2 * SUBLANES:]
        mask = first_sublane[:, :width]
        wrap1 = jnp.where(mask, prev[2 * SUBLANES - 1:, :], pltpu.roll(tail1, 1, axis=0))
        wrap2 = jnp.where(mask, prev[SUBLANES - 1:SUBLANES, :], pltpu.roll(tail2, 1, axis=0))
        u1 = jnp.concatenate([wrap1, u[:tm - SUBLANES]], axis=0)
        u2 = jnp.concatenate([wrap2, wrap1, u[:tm - 2 * SUBLANES]], axis=0)
        return (cb_ref[:, cols] + cw_ref[2:3, cols] * u + cw_ref[1:2, cols] * u1
                + cw_ref[0:1, cols] * u2)

    for c0 in range(0, f, FFN_CHUNK):
        width = min(FFN_CHUNK, f - c0)
        gate = conv(c0, width)
        half_val = conv(f + c0, width)
        inner = gate * ((gate * gate) * (GELU_C * 0.044715) + GELU_C)
        w = gate * half_val
        hid_ref[:, c0:c0 + width] = (w * jnp.tanh(inner) + w).astype(_BF16)

    head(a_ref, x_ref, other)

    y_ref[...] = jnp.dot(hid_ref[...], wdn_ref[...], preferred_element_type=_F32)


def _mix_ffn(a, x, w_out, g_mix, g_pre, w_up, conv_w, conv_b, w_down, g_post, seq_len):
    t, d = x.shape
    k = a.shape[1]
    f = w_down.shape[0]
    tm = min(ROW_TILE, seq_len)
    n = t // tm
    half = jnp.concatenate([jnp.ones((f,), _F32), jnp.full((f,), 0.5, _F32)])
    first = lambda width: pl.BlockSpec((tm, width), lambda i: (0, 0), pipeline_mode=pl.Buffered(1))
    ahead = lambda width: pl.BlockSpec((tm, width), lambda i: (jnp.minimum(i + 1, n - 1), 0))
    row = lambda width: _resident((1, width))
    return pl.pallas_call(
        functools.partial(_mix_ffn_kernel, tiles_per_seq=seq_len // tm),
        grid=(n + 1,),
        in_specs=[first(k), first(d), ahead(k), ahead(d), _resident((k, d)), row(d), row(d),
                  _resident((d, 2 * f)), _resident((CONV_WIDTH, 2 * f)), row(2 * f), _resident((f, d)),
                  row(d)],
        out_specs=pl.BlockSpec((tm, d), lambda i: (jnp.maximum(i - 1, 0), 0)),
        out_shape=jax.ShapeDtypeStruct((t, d), _F32),
        scratch_shapes=[pltpu.VMEM((tm, d), _BF16), pltpu.VMEM((2, tm, d), _F32),
                        pltpu.VMEM((tm, d), _F32), pltpu.VMEM((2 * SUBLANES, 2 * f), _F32),
                        pltpu.VMEM((d // LANES, SUBLANES * RUN_PITCH, LANES), _F32),
                        pltpu.VMEM((d // LANES, SUBLANES * RUN_PITCH, LANES), _F32),
                        pltpu.VMEM((tm, f), _BF16)],
        compiler_params=_params("arbitrary"), name="mix_ffn",
    )(a, x, a, x, w_out, g_mix.reshape(1, d), g_pre.reshape(1, d), w_up, conv_w * half,
      (conv_b * half).reshape(1, 2 * f), w_down, g_post.reshape(1, d))


def _decay_kernel(f_ref, bias_ref, place_ref, cq_ref, ck_ref, carry_ref):
    tm = f_ref.shape[1]

    @pl.when(pl.program_id(1) == 0)
    def _():
        carry_ref[...] = jnp.zeros_like(carry_ref)

    row = lax.broadcasted_iota(jnp.int32, (tm, tm), 0)
    col = lax.broadcasted_iota(jnp.int32, (tm, tm), 1)
    tril = (row >= col).astype(_F32)
    lane = lax.broadcasted_iota(jnp.int32, (tm, LANES), 1)
    lf = jnp.where(lane < FOX_HEADS, _log_sigmoid(f_ref[0] + bias_ref[...]), 0.0)
    c = carry_ref[0:1, :] + jnp.dot(tril, lf, preferred_element_type=_F32, precision=lax.Precision.HIGHEST)
    carry_ref[...] = jnp.broadcast_to(c[tm - 1:tm, :], carry_ref.shape)
    c = c * LOG2E

    packed = jnp.where(lane < FOX_HEADS, 1.0, 0.0)
    packed = pltpu.roll(packed, DECAY_PARTS * FOX_HEADS, axis=1)
    rest = c
    for j in range(DECAY_PARTS):
        part = rest.astype(_BF16).astype(_F32)
        rest = rest - part
        packed = packed + (part if j == 0 else pltpu.roll(part, j * FOX_HEADS, axis=1))
    placed = jnp.dot(packed.astype(_BF16), place_ref[...], preferred_element_type=_F32)
    width = cq_ref.shape[2]
    cq_ref[0] = placed[:, :width].astype(cq_ref.dtype)
    ck_ref[0] = placed[:, width:].astype(ck_ref.dtype)


def _decay_placement():
    nh, dh, parts = FOX_HEADS, FOX_HEAD_DIM, DECAY_PARTS
    width = nh * dh
    heads = jnp.arange(nh)
    base = (heads // 2) * 2 * dh + (1 - heads % 2) * dh
    ones_row = parts * nh + heads
    place = jnp.zeros((LANES, 2 * width), _F32)
    for j in range(parts):
        place = place.at[j * nh + heads, base + j].set(1.0)
        place = place.at[ones_row, base + parts + j].set(1.0)
        place = place.at[ones_row, width + base + j].set(1.0)
        place = place.at[j * nh + heads, width + base + parts + j].set(-1.0)
    return place.astype(_BF16)


def _fox_decay(f_pre, bias):
    b, s, _ = f_pre.shape
    nh, dh = FOX_HEADS, FOX_HEAD_DIM
    tm = min(ROW_TILE, s)
    bias_row = jnp.zeros((1, LANES), _F32).at[0, :nh].set(bias)
    out = jax.ShapeDtypeStruct((b, s, nh * dh), _BF16)
    return pl.pallas_call(
        _decay_kernel,
        grid=(b, s // tm),
        in_specs=[pl.BlockSpec((1, tm, LANES), lambda bi, ti: (bi, ti, 0)), _resident((1, LANES)),
                  _resident((LANES, 2 * nh * dh))],
        out_specs=[pl.BlockSpec((1, tm, nh * dh), lambda bi, ti: (bi, ti, 0))] * 2,
        out_shape=[out, out],
        scratch_shapes=[pltpu.VMEM((SUBLANES, LANES), _F32)],
        compiler_params=_params("parallel", "arbitrary"), name="fox_decay",
    )(f_pre, bias_row, _decay_placement())


def _attn_kernel(q_ref, cq_ref, g_ref, k_ref, ck_ref, vt_ref, o_ref,
                 sa_ref, sb_ref, sa_max_ref, sb_max_ref, m_ref, acc_ref):
    tq = q_ref.shape[1]
    tk = sa_ref.shape[1]
    dh = FOX_HEAD_DIM
    qi = pl.program_id(2)
    first_q = lax.broadcasted_iota(jnp.int32, (tq, 2 * dh), 1) < dh
    first_k = lax.broadcasted_iota(jnp.int32, (tk, 2 * dh), 1) < dh
    q_pair, cq_pair = q_ref[0], cq_ref[0]
    q_aug = (jnp.where(first_q, q_pair, cq_pair), jnp.where(first_q, cq_pair, q_pair))

    ones_blk = jnp.ones((ONES_ROWS, tk), _BF16)

    m_ref[...] = jnp.full_like(m_ref, -jnp.inf)
    acc_ref[...] = jnp.zeros_like(acc_ref)

    def causal_square(s):
        key = lax.broadcasted_iota(jnp.int32, (tk, tk), 0)
        qry = lax.broadcasted_iota(jnp.int32, (tk, tk), 1)
        square = jnp.where(key <= qry, s[:, :tk], -jnp.inf)
        return square if s.shape[1] == tk else jnp.concatenate([square, s[:, tk:]], axis=1)

    def scores(j, bufs, q0, diagonal):
        s_ref, smax_ref = bufs
        start = pl.multiple_of(j * tk, tk)
        k_pair = k_ref[0, pl.ds(start, tk), :]
        ck_pair = ck_ref[0, pl.ds(start, tk), :]
        k_aug = (jnp.where(first_k, k_pair, ck_pair), jnp.where(first_k, ck_pair, k_pair))
        for hh in range(2):
            s = lax.dot_general(k_aug[hh], q_aug[hh][q0:], (((1,), (1,)), ((), ())),
                                preferred_element_type=_F32)
            if diagonal:
                s = causal_square(s)
            s_ref[hh, :, q0:] = s
            smax_ref[hh, :, q0:] = jnp.max(s, axis=0, keepdims=True)

    def accumulate(j, bufs, q0):
        s_ref, smax_ref = bufs
        start = pl.multiple_of(j * tk, tk)
        for hh in range(2):
            m_old = m_ref[hh, :, q0:]
            m_new = jnp.maximum(m_old, smax_ref[hh, :, q0:])
            alpha = jnp.exp2(m_old - m_new)
            p = jnp.exp2(s_ref[hh, :, q0:] - m_new).astype(_BF16)
            vt = jnp.concatenate([vt_ref[hh * dh:(hh + 1) * dh, pl.ds(start, tk)], ones_blk], axis=0)
            acc_ref[hh, :, q0:] = (alpha * acc_ref[hh, :, q0:]
                                   + jnp.dot(vt, p, preferred_element_type=_F32))
            m_ref[hh, :, q0:] = m_new

    per_tile = tq // tk
    buf_a, buf_b = (sa_ref, sa_max_ref), (sb_ref, sb_max_ref)

    scores(0, buf_a, 0, False)

    def body(i, carry):
        for u in range(per_tile):
            j = i * per_tile + u
            cur, nxt = (buf_a, buf_b) if u % 2 == 0 else (buf_b, buf_a)
            scores(j + 1, nxt, 0, False)
            accumulate(j, cur, 0)
        return carry

    lax.fori_loop(0, qi, body, 0)
    for u in range(per_tile):
        j = qi * per_tile + u
        cur, nxt = (buf_a, buf_b) if u % 2 == 0 else (buf_b, buf_a)
        if u + 1 < per_tile:
            scores(j + 1, nxt, (u + 1) * tk, True)
        if u == 0:
            for hh in range(2):
                square = causal_square(sa_ref[hh, :, :tk])
                sa_ref[hh, :, :tk] = square
                sa_max_ref[hh, :, :tk] = jnp.max(square, axis=0, keepdims=True)
        accumulate(j, cur, u * tk)

    halves = []
    for hh in range(2):
        acc = acc_ref[hh]
        halves.append(acc[0:dh, :] / acc[dh:dh + 1, :])
    out = jnp.concatenate(halves, axis=0).T
    o_ref[0] = (out * jax.nn.sigmoid(g_ref[0].astype(_F32))).astype(o_ref.dtype)


def _fox_attention(q, cq, g, k, ck, vt):
    b, s, hd = q.shape
    pairs = hd // LANES
    tq = min(ATTN_TILE, s)
    tk = min(ATTN_KEYS, tq // 2)
    rows = FOX_HEAD_DIM + ONES_ROWS
    qspec = pl.BlockSpec((1, tq, LANES), lambda bi, pi, qi: (bi, qi, pi))
    kspec = pl.BlockSpec((1, s, LANES), lambda bi, pi, qi: (bi, 0, pi))
    vspec = pl.BlockSpec((LANES, s), lambda bi, pi, qi: (pi, bi))
    return pl.pallas_call(
        _attn_kernel,
        grid=(b, pairs, s // tq),
        in_specs=[qspec, qspec, qspec, kspec, kspec, vspec],
        out_specs=qspec,
        out_shape=jax.ShapeDtypeStruct((b, s, hd), _BF16),
        scratch_shapes=[pltpu.VMEM((2, tk, tq), _F32), pltpu.VMEM((2, tk, tq), _F32),
                        pltpu.VMEM((2, 1, tq), _F32), pltpu.VMEM((2, 1, tq), _F32),
                        pltpu.VMEM((2, 1, tq), _F32), pltpu.VMEM((2, rows, tq), _F32)],
        compiler_params=_params("parallel", "parallel", "arbitrary"), name="fox_attention",
    )(q, cq, g, k, ck, vt)


def _pad_cols(w, width):
    return jnp.pad(w, ((0, 0), (0, width - w.shape[1])))


def kernel(x, norm_mix_pre, norm_mix_post, norm_ffn_pre, norm_ffn_post, mlstm_w_in, mlstm_b_gate,
           mlstm_norm, mlstm_w_out, kv_norm, kv_w, kv_b_f, fox_w_qg, fox_w_out, ffn_w_up,
           ffn_conv_w, ffn_conv_b, ffn_w_down):
    b, s, d = x.shape
    depth = norm_mix_pre.shape[0]
    n_a = mlstm_w_in.shape[0]
    xt = x.reshape(b * s, d)
    hk = MLSTM_HEADS * MLSTM_QK_DIM
    hv = MLSTM_HEADS * MLSTM_V_DIM
    k_sh = vt_sh = cq_sh = ck_sh = None
    for layer in range(depth):
        if layer < n_a:
            w_in = mlstm_w_in[layer]
            weights = [(w_in[:, :hk] * MLSTM_QK_DIM ** -0.5).astype(_BF16),
                       w_in[:, hk:2 * hk].astype(_BF16),
                       w_in[:, 2 * hk:2 * hk + hv].T.astype(_BF16),
                       w_in[:, 2 * hk + hv:2 * hk + 2 * hv].astype(_BF16),
                       _pad_cols(w_in[:, 2 * hk + 2 * hv:], LANES).astype(_BF16)]
            q, k, vt, o, gates = _norm_proj(xt, norm_mix_pre[layer], weights,
                                            [_BF16, _BF16, _BF16, _BF16, _F32], "mlstm_in_proj",
                                            transposed=(False, False, True, False, False))
            seq = lambda a: a.reshape(b, s, a.shape[-1])
            hs = _mlstm(seq(q), seq(k), vt, seq(o), seq(gates), mlstm_b_gate[layer], mlstm_norm[layer])
            mixed, w_out = hs.reshape(b * s, hv), mlstm_w_out[layer]
        else:
            j = layer - n_a
            w_qg = fox_w_qg[j]
            weights = [(w_qg[:, :d] * (FOX_HEAD_DIM ** -0.5 * LOG2E)).astype(_BF16),
                       w_qg[:, d:].astype(_BF16)]
            q, g = _norm_proj(xt, norm_mix_pre[layer], weights, [_BF16, _BF16], "fox_in_proj")
            seq = lambda a: a.reshape(b, s, a.shape[-1])
            att = _fox_attention(seq(q), cq_sh, seq(g), k_sh, ck_sh, vt_sh)
            mixed, w_out = att.reshape(b * s, d), fox_w_out[j]
        xt = _mix_ffn(mixed, xt, w_out.astype(_BF16), norm_mix_post[layer], norm_ffn_pre[layer],
                      ffn_w_up[layer].astype(_BF16), ffn_conv_w[layer], ffn_conv_b[layer],
                      ffn_w_down[layer].astype(_BF16), norm_ffn_post[layer], s)
        if layer == n_a - 1:
            weights = [kv_w[:, :d].astype(_BF16), kv_w[:, d:2 * d].T.astype(_BF16),
                       _pad_cols(kv_w[:, 2 * d:], LANES).astype(_BF16)]
            k2, vt_sh, f_pre = _norm_proj(xt, kv_norm, weights, [_BF16, _BF16, _F32], "kv_proj",
                                          transposed=(False, True, False))
            k_sh = k2.reshape(b, s, d)
            cq_sh, ck_sh = _fox_decay(f_pre.reshape(b, s, LANES), kv_b_f)
    return xt.reshape(b, s, d)
```

```python
import functools
import math

import jax
import jax.numpy as jnp
from jax import lax
from jax.experimental import pallas as pl
from jax.experimental.pallas import tpu as pltpu

EPS = 1e-6
MLSTM_HEADS = 8
MLSTM_QK_DIM = 64
MLSTM_V_DIM = 128
FOX_HEADS = 16
FOX_HEAD_DIM = 64
CONV_WIDTH = 3

LANES = 128
SUBLANES = 8
VMEM_LIMIT_BYTES = 56 * 1024 * 1024
ROW_TILE = 512
MLSTM_CHUNK = 128
MLSTM_ROWS = 2
ATTN_TILE = 2048
ATTN_KEYS = 256
FFN_CHUNK = 256
DECAY_PARTS = 3
LOG2E = math.log2(math.e)

_BF16 = jnp.bfloat16
_F32 = jnp.float32


def _params(*semantics):
    return pltpu.CompilerParams(dimension_semantics=semantics, vmem_limit_bytes=VMEM_LIMIT_BYTES)


def _resident(shape):
    return pl.BlockSpec(shape, lambda *_: (0,) * len(shape), pipeline_mode=pl.Buffered(1))


def _rms(x, gain):
    return x * lax.rsqrt(jnp.mean(x * x, axis=-1, keepdims=True) + EPS) * gain


def _norm_proj_kernel(x_ref, g_ref, *refs, transposed):
    n_out = len(transposed)
    w_refs, o_refs = refs[:n_out], refs[n_out:]
    h = _rms(x_ref[...], g_ref[...]).astype(_BF16)
    for w_ref, o_ref, tr in zip(w_refs, o_refs, transposed):
        n = w_ref.shape[0] if tr else w_ref.shape[1]
        for c0 in range(0, n, 512):
            c1 = min(c0 + 512, n)
            if tr:
                y = lax.dot_general(w_ref[c0:c1, :], h, (((1,), (1,)), ((), ())),
                                    preferred_element_type=_F32)
                o_ref[c0:c1, :] = y.astype(o_ref.dtype)
            else:
                y = jnp.dot(h, w_ref[:, c0:c1], preferred_element_type=_F32)
                o_ref[:, c0:c1] = y.astype(o_ref.dtype)


def _norm_proj(x, gain, weights, out_dtypes, name, transposed=None):
    t, d = x.shape
    tm = min(ROW_TILE, t)
    transposed = tuple(transposed or (False,) * len(weights))
    in_specs = [pl.BlockSpec((tm, d), lambda i: (i, 0)), _resident((1, d))]
    in_specs += [_resident(w.shape) for w in weights]
    out_specs, out_shape = [], []
    for w, dt, tr in zip(weights, out_dtypes, transposed):
        if tr:
            out_specs.append(pl.BlockSpec((w.shape[0], tm), lambda i: (0, i)))
            out_shape.append(jax.ShapeDtypeStruct((w.shape[0], t), dt))
        else:
            out_specs.append(pl.BlockSpec((tm, w.shape[1]), lambda i: (i, 0)))
            out_shape.append(jax.ShapeDtypeStruct((t, w.shape[1]), dt))
    return pl.pallas_call(
        functools.partial(_norm_proj_kernel, transposed=transposed),
        grid=(t // tm,), in_specs=in_specs, out_specs=out_specs, out_shape=out_shape,
        compiler_params=_params("parallel"), name=name,
    )(x, gain.reshape(1, d), *weights)


def _log_sigmoid(z):
    return jnp.minimum(z, 0.0) - jnp.log1p(jnp.exp(-jnp.abs(z)))


ONES_ROWS = 16


def _mlstm_kernel(q_ref, k_ref, o_ref, gc_ref, gr_ref, bc_ref, br_ref, hn_ref, *rest, rows):
    nh, dk, dv = MLSTM_HEADS, MLSTM_QK_DIM, MLSTM_V_DIM
    L = q_ref.shape[1]
    vt_refs, (out_ref, ct_ref, m_ref) = rest[:rows], rest[rows:]

    @pl.when(pl.program_id(1) == 0)
    def _():
        ct_ref[...] = jnp.zeros_like(ct_ref)
        m_ref[...] = jnp.zeros_like(m_ref)

    row = lax.broadcasted_iota(jnp.int32, (L, L), 0)
    col = lax.broadcasted_iota(jnp.int32, (L, L), 1)
    source_before_target = row <= col
    tril = (row >= col).astype(_F32)
    triu = source_before_target.astype(_F32)

    heads = [(bb, h) for bb in range(rows) for h in range(nh)]
    lane_qk = lax.broadcasted_iota(jnp.int32, (L, 2 * dk), 1)
    ones_blk = jnp.ones((ONES_ROWS, L), _BF16)
    first_half = lane_qk < dk

    gates = {}
    for bb in range(rows):
        gc = gc_ref[bb] + bc_ref[...]
        gr = gr_ref[bb] + br_ref[:, 0:1]
        lane_g = lax.broadcasted_iota(jnp.int32, gc.shape, 1)
        sub_g = lax.broadcasted_iota(jnp.int32, gr.shape, 0)
        lf_c = jnp.where(lane_g >= nh, _log_sigmoid(gc), 0.0)
        lf_r = jnp.where(sub_g >= nh, _log_sigmoid(gr), 0.0)
        b_c_all = jnp.dot(tril, lf_c, preferred_element_type=_F32, precision=lax.Precision.HIGHEST)
        b_r_all = jnp.dot(lf_r, triu, preferred_element_type=_F32, precision=lax.Precision.HIGHEST)
        gates[bb] = (gc, gr, b_c_all, b_r_all)

    st = {}
    for bb, h in heads:
        pair = slice((h // 2) * 2 * dk, (h // 2 + 1) * 2 * dk)
        qp, kp = q_ref[bb, :, pair], k_ref[bb, :, pair]
        qm = jnp.where(first_half if h % 2 == 0 else ~first_half, qp, jnp.zeros_like(qp))
        ct = ct_ref[bb * nh + h]
        both = lax.dot_general(jnp.concatenate([kp, ct.astype(_BF16)], axis=0), qm,
                               (((1,), (1,)), ((), ())), preferred_element_type=_F32)
        st[bb, h] = dict(kp=kp, ct=ct, s_qk=both[:L], inter=both[L:])

    for bb, h in heads:
        gc, gr, b_c_all, b_r_all = gates[bb]
        e = st[bb, h]
        g_col = gc[:, h:h + 1] - b_c_all[:, nh + h:nh + h + 1]
        b_row = b_r_all[nh + h:nh + h + 1, :]
        i_row = gr[h:h + 1, :]
        m_prev = m_ref[bb * nh + h:bb * nh + h + 1, 0:1]
        dmat = jnp.where(source_before_target, g_col + b_row, -jnp.inf)
        m_inter = b_row + m_prev
        m_t = jnp.maximum(m_inter, jnp.max(dmat, axis=0, keepdims=True))
        sd = (e["s_qk"] * jnp.exp(dmat - m_t)).astype(_BF16)
        vt_aug = jnp.concatenate([vt_refs[bb][h * dv:(h + 1) * dv, :], ones_blk], axis=0)
        tot = jnp.dot(vt_aug, sd, preferred_element_type=_F32) + jnp.exp(m_inter - m_t) * e["inter"]
        num, den = tot[:dv], tot[dv:dv + 1]
        hv = num / jnp.maximum(jnp.abs(den), jnp.exp(-m_t))
        hv = hv * lax.rsqrt(jnp.mean(hv * hv, axis=0, keepdims=True) + EPS)
        hv = (hv * hn_ref[h * dv:(h + 1) * dv, :]).T
        hv = hv * jax.nn.sigmoid(o_ref[bb, :, h * dv:(h + 1) * dv].astype(_F32))
        out_ref[bb, :, h * dv:(h + 1) * dv] = hv.astype(out_ref.dtype)
        e.update(vt_aug=vt_aug, b_row=b_row, i_row=i_row, m_prev=m_prev)

    for bb, h in heads:
        e = st[bb, h]
        b_row, m_prev = e["b_row"], e["m_prev"]
        b_last = b_row[:, L - 1:L]
        g_row = b_last - b_row + e["i_row"]
        m_new = jnp.maximum(b_last + m_prev, jnp.max(g_row, axis=-1, keepdims=True))
        decay = jnp.exp(b_last + m_prev - m_new)
        vw = (e["vt_aug"].astype(_F32) * jnp.exp(g_row - m_new)).astype(_BF16)
        ct_ref[bb * nh + h] = decay * e["ct"] + jnp.dot(vw, e["kp"], preferred_element_type=_F32)
        m_ref[bb * nh + h:bb * nh + h + 1, :] = jnp.broadcast_to(m_new, (1, m_ref.shape[1]))


def _mlstm(q, k, vt, o, gates, b_gate, head_norm):
    b, s, _ = q.shape
    nh, dk, dv = MLSTM_HEADS, MLSTM_QK_DIM, MLSTM_V_DIM
    L = min(MLSTM_CHUNK, s)
    chunks = s // L
    gates_row = jnp.swapaxes(gates[:, :, :2 * nh], 1, 2)
    bias_col = jnp.zeros((1, LANES), _F32).at[0, :2 * nh].set(b_gate)
    bias_row = b_gate.reshape(2 * nh, 1)
    norm_cols = jnp.broadcast_to(head_norm.astype(_F32)[:, None], (nh * dv, L))
    rows = MLSTM_ROWS if b % MLSTM_ROWS == 0 else 1
    tok = lambda width: pl.BlockSpec((rows, L, width), lambda bi, ci: (bi, ci, 0))
    vt_spec = lambda r: pl.BlockSpec((nh * dv, L), lambda bi, ci: (0, (bi * rows + r) * chunks + ci))
    return pl.pallas_call(
        functools.partial(_mlstm_kernel, rows=rows),
        grid=(b // rows, chunks),
        in_specs=[tok(nh * dk), tok(nh * dk), tok(nh * dv), tok(LANES),
                  pl.BlockSpec((rows, 2 * nh, L), lambda bi, ci: (bi, 0, ci)),
                  _resident((1, LANES)), _resident((2 * nh, 1)), _resident((nh * dv, L))]
                 + [vt_spec(r) for r in range(rows)],
        out_specs=tok(nh * dv),
        out_shape=jax.ShapeDtypeStruct((b, s, nh * dv), _BF16),
        scratch_shapes=[pltpu.VMEM((rows * nh, dv + ONES_ROWS, 2 * dk), _F32),
                        pltpu.VMEM((rows * nh, LANES), _F32)],
        compiler_params=_params("parallel", "arbitrary"), name="mlstm_chunk",
    )(q, k, o, gates, gates_row, bias_col, bias_row, norm_cols, *([vt] * rows))


GELU_C = math.sqrt(2.0 / math.pi)
RUN_PITCH = 72


def _mix_ffn_kernel(a0_ref, x0_ref, a_ref, x_ref, wo_ref, gmix_ref, gpre_ref, wup_ref, cw_ref, cb_ref,
                    wdn_ref, gpost_ref, o_ref,
                    h_ref, xp_ref, y_ref, carry_ref, slab_in_ref, slab_out_ref, hid_ref, *, tiles_per_seq):
    tm, d = x_ref.shape
    f = wdn_ref.shape[0]
    run = tm // SUBLANES
    n_slabs = d // LANES
    step = pl.program_id(0)

    def gather_rows(value):
        for c in range(n_slabs):
            for r in range(SUBLANES):
                slab_in_ref[c, r * RUN_PITCH:r * RUN_PITCH + run, :] = (
                    value[r * run:(r + 1) * run, c * LANES:(c + 1) * LANES])
        return jnp.concatenate(
            [jnp.concatenate([slab_in_ref[c, pl.ds(k, SUBLANES, stride=RUN_PITCH), :]
                              for k in range(run)], axis=0) for c in range(n_slabs)], axis=1)

    def scatter_rows(value, dst_ref):
        for c in range(n_slabs):
            for k in range(run):
                slab_out_ref[c, pl.ds(k, SUBLANES, stride=RUN_PITCH), :] = (
                    value[k * SUBLANES:(k + 1) * SUBLANES, c * LANES:(c + 1) * LANES])
            for r in range(SUBLANES):
                dst_ref[r * run:(r + 1) * run, c * LANES:(c + 1) * LANES] = (
                    slab_out_ref[c, r * RUN_PITCH:r * RUN_PITCH + run, :])

    def head(a_blk_ref, x_blk_ref, slot):
        mixed = jnp.dot(a_blk_ref[...], wo_ref[...], preferred_element_type=_F32)
        xp = gather_rows(x_blk_ref[...] + _rms(mixed, gmix_ref[...]))
        xp_ref[slot] = xp
        h_ref[...] = _rms(xp, gpre_ref[...]).astype(_BF16)

    @pl.when(step == 0)
    def _():
        y_ref[...] = jnp.zeros_like(y_ref)
        xp_ref[...] = jnp.zeros_like(xp_ref)
        head(a0_ref, x0_ref, 0)

    @pl.when(step % tiles_per_seq == 0)
    def _():
        carry_ref[...] = jnp.zeros_like(carry_ref)

    other = (step + 1) % 2

    scatter_rows(xp_ref[other] + _rms(y_ref[...], gpost_ref[...]), o_ref)

    h = h_ref[...]
    first_sublane = lax.broadcasted_iota(jnp.int32, (SUBLANES, FFN_CHUNK), 0) == 0

    def conv(c0, width):
        cols = slice(c0, c0 + width)
        u = jnp.dot(h, wup_ref[:, cols], preferred_element_type=_F32)
        prev = carry_ref[:, cols]
        tail1, tail2 = u[tm - SUBLANES:], u[tm - 2 * SUBLANES:tm - SUBLANES]
        carry_ref[:, cols] = u[tm - 2 * SUBLANES:]
        mask = first_sublane[:, :width]
        wrap1 = jnp.where(mask, prev[2 * SUBLANES - 1:, :], pltpu.roll(tail1, 1, axis=0))
        wrap2 = jnp.where(mask, prev[SUBLANES - 1:SUBLANES, :], pltpu.roll(tail2, 1, axis=0))
        u1 = jnp.concatenate([wrap1, u[:tm - SUBLANES]], axis=0)
        u2 = jnp.concatenate([wrap2, wrap1, u[:tm - 2 * SUBLANES]], axis=0)
        return (cb_ref[:, cols] + cw_ref[2:3, cols] * u + cw_ref[1:2, cols] * u1
                + cw_ref[0:1, cols] * u2)

    for c0 in range(0, f, FFN_CHUNK):
        width = min(FFN_CHUNK, f - c0)
        gate = conv(c0, width)
        half_val = conv(f + c0, width)
        inner = gate * ((gate * gate) * (GELU_C * 0.044715) + GELU_C)
        w = gate * half_val
        hid_ref[:, c0:c0 + width] = (w * jnp.tanh(inner) + w).astype(_BF16)

    head(a_ref, x_ref, other)

    y_ref[...] = jnp.dot(hid_ref[...], wdn_ref[...], preferred_element_type=_F32)


def _mix_ffn(a, x, w_out, g_mix, g_pre, w_up, conv_w, conv_b, w_down, g_post, seq_len):
    t, d = x.shape
    k = a.shape[1]
    f = w_down.shape[0]
    tm = min(ROW_TILE, seq_len)
    n = t // tm
    half = jnp.concatenate([jnp.ones((f,), _F32), jnp.full((f,), 0.5, _F32)])
    first = lambda width: pl.BlockSpec((tm, width), lambda i: (0, 0), pipeline_mode=pl.Buffered(1))
    ahead = lambda width: pl.BlockSpec((tm, width), lambda i: (jnp.minimum(i + 1, n - 1), 0))
    row = lambda width: _resident((1, width))
    return pl.pallas_call(
        functools.partial(_mix_ffn_kernel, tiles_per_seq=seq_len // tm),
        grid=(n + 1,),
        in_specs=[first(k), first(d), ahead(k), ahead(d), _resident((k, d)), row(d), row(d),
                  _resident((d, 2 * f)), _resident((CONV_WIDTH, 2 * f)), row(2 * f), _resident((f, d)),
                  row(d)],
        out_specs=pl.BlockSpec((tm, d), lambda i: (jnp.maximum(i - 1, 0), 0)),
        out_shape=jax.ShapeDtypeStruct((t, d), _F32),
        scratch_shapes=[pltpu.VMEM((tm, d), _BF16), pltpu.VMEM((2, tm, d), _F32),
                        pltpu.VMEM((tm, d), _F32), pltpu.VMEM((2 * SUBLANES, 2 * f), _F32),
                        pltpu.VMEM((d // LANES, SUBLANES * RUN_PITCH, LANES), _F32),
                        pltpu.VMEM((d // LANES, SUBLANES * RUN_PITCH, LANES), _F32),
                        pltpu.VMEM((tm, f), _BF16)],
        compiler_params=_params("arbitrary"), name="mix_ffn",
    )(a, x, a, x, w_out, g_mix.reshape(1, d), g_pre.reshape(1, d), w_up, conv_w * half,
      (conv_b * half).reshape(1, 2 * f), w_down, g_post.reshape(1, d))


def _decay_kernel(f_ref, bias_ref, place_ref, cq_ref, ck_ref, carry_ref):
    tm = f_ref.shape[1]

    @pl.when(pl.program_id(1) == 0)
    def _():
        carry_ref[...] = jnp.zeros_like(carry_ref)

    row = lax.broadcasted_iota(jnp.int32, (tm, tm), 0)
    col = lax.broadcasted_iota(jnp.int32, (tm, tm), 1)
    tril = (row >= col).astype(_F32)
    lane = lax.broadcasted_iota(jnp.int32, (tm, LANES), 1)
    lf = jnp.where(lane < FOX_HEADS, _log_sigmoid(f_ref[0] + bias_ref[...]), 0.0)
    c = carry_ref[0:1, :] + jnp.dot(tril, lf, preferred_element_type=_F32, precision=lax.Precision.HIGHEST)
    carry_ref[...] = jnp.broadcast_to(c[tm - 1:tm, :], carry_ref.shape)
    c = c * LOG2E

    packed = jnp.where(lane < FOX_HEADS, 1.0, 0.0)
    packed = pltpu.roll(packed, DECAY_PARTS * FOX_HEADS, axis=1)
    rest = c
    for j in range(DECAY_PARTS):
        part = rest.astype(_BF16).astype(_F32)
        rest = rest - part
        packed = packed + (part if j == 0 else pltpu.roll(part, j * FOX_HEADS, axis=1))
    placed = jnp.dot(packed.astype(_BF16), place_ref[...], preferred_element_type=_F32)
    width = cq_ref.shape[2]
    cq_ref[0] = placed[:, :width].astype(cq_ref.dtype)
    ck_ref[0] = placed[:, width:].astype(ck_ref.dtype)


def _decay_placement():
    nh, dh, parts = FOX_HEADS, FOX_HEAD_DIM, DECAY_PARTS
    width = nh * dh
    heads = jnp.arange(nh)
    base = (heads // 2) * 2 * dh + (1 - heads % 2) * dh
    ones_row = parts * nh + heads
    place = jnp.zeros((LANES, 2 * width), _F32)
    for j in range(parts):
        place = place.at[j * nh + heads, base + j].set(1.0)
        place = place.at[ones_row, base + parts + j].set(1.0)
        place = place.at[ones_row, width + base + j].set(1.0)
        place = place.at[j * nh + heads, width + base + parts + j].set(-1.0)
    return place.astype(_BF16)


def _fox_decay(f_pre, bias):
    b, s, _ = f_pre.shape
    nh, dh = FOX_HEADS, FOX_HEAD_DIM
    tm = min(ROW_TILE, s)
    bias_row = jnp.zeros((1, LANES), _F32).at[0, :nh].set(bias)
    out = jax.ShapeDtypeStruct((b, s, nh * dh), _BF16)
    return pl.pallas_call(
        _decay_kernel,
        grid=(b, s // tm),
        in_specs=[pl.BlockSpec((1, tm, LANES), lambda bi, ti: (bi, ti, 0)), _resident((1, LANES)),
                  _resident((LANES, 2 * nh * dh))],
        out_specs=[pl.BlockSpec((1, tm, nh * dh), lambda bi, ti: (bi, ti, 0))] * 2,
        out_shape=[out, out],
        scratch_shapes=[pltpu.VMEM((SUBLANES, LANES), _F32)],
        compiler_params=_params("parallel", "arbitrary"), name="fox_decay",
    )(f_pre, bias_row, _decay_placement())


def _attn_kernel(q_ref, cq_ref, g_ref, k_ref, ck_ref, vt_ref, o_ref,
                 sa_ref, sb_ref, sa_max_ref, sb_max_ref, m_ref, acc_ref):
    tq = q_ref.shape[1]
    tk = sa_ref.shape[1]
    dh = FOX_HEAD_DIM
    qi = pl.program_id(2)
    first_q = lax.broadcasted_iota(jnp.int32, (tq, 2 * dh), 1) < dh
    first_k = lax.broadcasted_iota(jnp.int32, (tk, 2 * dh), 1) < dh
    q_pair, cq_pair = q_ref[0], cq_ref[0]
    q_aug = (jnp.where(first_q, q_pair, cq_pair), jnp.where(first_q, cq_pair, q_pair))

    ones_blk = jnp.ones((ONES_ROWS, tk), _BF16)

    m_ref[...] = jnp.full_like(m_ref, -jnp.inf)
    acc_ref[...] = jnp.zeros_like(acc_ref)

    def causal_square(s):
        key = lax.broadcasted_iota(jnp.int32, (tk, tk), 0)
        qry = lax.broadcasted_iota(jnp.int32, (tk, tk), 1)
        square = jnp.where(key <= qry, s[:, :tk], -jnp.inf)
        return square if s.shape[1] == tk else jnp.concatenate([square, s[:, tk:]], axis=1)

    def scores(j, bufs, q0, diagonal):
        s_ref, smax_ref = bufs
        start = pl.multiple_of(j * tk, tk)
        k_pair = k_ref[0, pl.ds(start, tk), :]
        ck_pair = ck_ref[0, pl.ds(start, tk), :]
        k_aug = (jnp.where(first_k, k_pair, ck_pair), jnp.where(first_k, ck_pair, k_pair))
        for hh in range(2):
            s = lax.dot_general(k_aug[hh], q_aug[hh][q0:], (((1,), (1,)), ((), ())),
                                preferred_element_type=_F32)
            if diagonal:
                s = causal_square(s)
            s_ref[hh, :, q0:] = s
            smax_ref[hh, :, q0:] = jnp.max(s, axis=0, keepdims=True)

    def accumulate(j, bufs, q0):
        s_ref, smax_ref = bufs
        start = pl.multiple_of(j * tk, tk)
        for hh in range(2):
            m_old = m_ref[hh, :, q0:]
            m_new = jnp.maximum(m_old, smax_ref[hh, :, q0:])
            alpha = jnp.exp2(m_old - m_new)
            p = jnp.exp2(s_ref[hh, :, q0:] - m_new).astype(_BF16)
            vt = jnp.concatenate([vt_ref[hh * dh:(hh + 1) * dh, pl.ds(start, tk)], ones_blk], axis=0)
            acc_ref[hh, :, q0:] = (alpha * acc_ref[hh, :, q0:]
                                   + jnp.dot(vt, p, preferred_element_type=_F32))
            m_ref[hh, :, q0:] = m_new

    per_tile = tq // tk
    buf_a, buf_b = (sa_ref, sa_max_ref), (sb_ref, sb_max_ref)

    scores(0, buf_a, 0, False)

    def body(i, carry):
        for u in range(per_tile):
            j = i * per_tile + u
            cur, nxt = (buf_a, buf_b) if u % 2 == 0 else (buf_b, buf_a)
            scores(j + 1, nxt, 0, False)
            accumulate(j, cur, 0)
        return carry

    lax.fori_loop(0, qi, body, 0)
    for u in range(per_tile):
        j = qi * per_tile + u
        cur, nxt = (buf_a, buf_b) if u % 2 == 0 else (buf_b, buf_a)
        if u + 1 < per_tile:
            scores(j + 1, nxt, (u + 1) * tk, True)
        if u == 0:
            for hh in range(2):
                square = causal_square(sa_ref[hh, :, :tk])
                sa_ref[hh, :, :tk] = square
                sa_max_ref[hh, :, :tk] = jnp.max(square, axis=0, keepdims=True)
        accumulate(j, cur, u * tk)

    halves = []
    for hh in range(2):
        acc = acc_ref[hh]
        halves.append(acc[0:dh, :] / acc[dh:dh + 1, :])
    out = jnp.concatenate(halves, axis=0).T
    o_ref[0] = (out * jax.nn.sigmoid(g_ref[0].astype(_F32))).astype(o_ref.dtype)


def _fox_attention(q, cq, g, k, ck, vt):
    b, s, hd = q.shape
    pairs = hd // LANES
    tq = min(ATTN_TILE, s)
    tk = min(ATTN_KEYS, tq // 2)
    rows = FOX_HEAD_DIM + ONES_ROWS
    qspec = pl.BlockSpec((1, tq, LANES), lambda bi, pi, qi: (bi, qi, pi))
    kspec = pl.BlockSpec((1, s, LANES), lambda bi, pi, qi: (bi, 0, pi))
    vspec = pl.BlockSpec((LANES, s), lambda bi, pi, qi: (pi, bi))
    return pl.pallas_call(
        _attn_kernel,
        grid=(b, pairs, s // tq),
        in_specs=[qspec, qspec, qspec, kspec, kspec, vspec],
        out_specs=qspec,
        out_shape=jax.ShapeDtypeStruct((b, s, hd), _BF16),
        scratch_shapes=[pltpu.VMEM((2, tk, tq), _F32), pltpu.VMEM((2, tk, tq), _F32),
                        pltpu.VMEM((2, 1, tq), _F32), pltpu.VMEM((2, 1, tq), _F32),
                        pltpu.VMEM((2, 1, tq), _F32), pltpu.VMEM((2, rows, tq), _F32)],
        compiler_params=_params("parallel", "parallel", "arbitrary"), name="fox_attention",
    )(q, cq, g, k, ck, vt)


def _pad_cols(w, width):
    return jnp.pad(w, ((0, 0), (0, width - w.shape[1])))


def kernel(x, norm_mix_pre, norm_mix_post, norm_ffn_pre, norm_ffn_post, mlstm_w_in, mlstm_b_gate,
           mlstm_norm, mlstm_w_out, kv_norm, kv_w, kv_b_f, fox_w_qg, fox_w_out, ffn_w_up,
           ffn_conv_w, ffn_conv_b, ffn_w_down):
    b, s, d = x.shape
    depth = norm_mix_pre.shape[0]
    n_a = mlstm_w_in.shape[0]
    xt = x.reshape(b * s, d)
    hk = MLSTM_HEADS * MLSTM_QK_DIM
    hv = MLSTM_HEADS * MLSTM_V_DIM
    k_sh = vt_sh = cq_sh = ck_sh = None
    for layer in range(depth):
        if layer < n_a:
            w_in = mlstm_w_in[layer]
            weights = [(w_in[:, :hk] * MLSTM_QK_DIM ** -0.5).astype(_BF16),
                       w_in[:, hk:2 * hk].astype(_BF16),
                       w_in[:, 2 * hk:2 * hk + hv].T.astype(_BF16),
                       w_in[:, 2 * hk + hv:2 * hk + 2 * hv].astype(_BF16),
                       _pad_cols(w_in[:, 2 * hk + 2 * hv:], LANES).astype(_BF16)]
            q, k, vt, o, gates = _norm_proj(xt, norm_mix_pre[layer], weights,
                                            [_BF16, _BF16, _BF16, _BF16, _F32], "mlstm_in_proj",
                                            transposed=(False, False, True, False, False))
            seq = lambda a: a.reshape(b, s, a.shape[-1])
            hs = _mlstm(seq(q), seq(k), vt, seq(o), seq(gates), mlstm_b_gate[layer], mlstm_norm[layer])
            mixed, w_out = hs.reshape(b * s, hv), mlstm_w_out[layer]
        else:
            j = layer - n_a
            w_qg = fox_w_qg[j]
            weights = [(w_qg[:, :d] * (FOX_HEAD_DIM ** -0.5 * LOG2E)).astype(_BF16),
                       w_qg[:, d:].astype(_BF16)]
            q, g = _norm_proj(xt, norm_mix_pre[layer], weights, [_BF16, _BF16], "fox_in_proj")
            seq = lambda a: a.reshape(b, s, a.shape[-1])
            att = _fox_attention(seq(q), cq_sh, seq(g), k_sh, ck_sh, vt_sh)
            mixed, w_out = att.reshape(b * s, d), fox_w_out[j]
        xt = _mix_ffn(mixed, xt, w_out.astype(_BF16), norm_mix_post[layer], norm_ffn_pre[layer],
                      ffn_w_up[layer].astype(_BF16), ffn_conv_w[layer], ffn_conv_b[layer],
                      ffn_w_down[layer].astype(_BF16), norm_ffn_post[layer], s)
        if layer == n_a - 1:
            weights = [kv_w[:, :d].astype(_BF16), kv_w[:, d:2 * d].T.astype(_BF16),
                       _pad_cols(kv_w[:, 2 * d:], LANES).astype(_BF16)]
            k2, vt_sh, f_pre = _norm_proj(xt, kv_norm, weights, [_BF16, _BF16, _F32], "kv_proj",
                                          transposed=(False, True, False))
            k_sh = k2.reshape(b, s, d)
            cq_sh, ck_sh = _fox_decay(f_pre.reshape(b, s, LANES), kv_b_f)
    return xt.reshape(b, s, d)
```

```python
import functools
import math

import jax
import jax.numpy as jnp
from jax import lax
from jax.experimental import pallas as pl
from jax.experimental.pallas import tpu as pltpu

EPS = 1e-6
MLSTM_HEADS = 8
MLSTM_QK_DIM = 64
MLSTM_V_DIM = 128
FOX_HEADS = 16
FOX_HEAD_DIM = 64
CONV_WIDTH = 3

LANES = 128
SUBLANES = 8
VMEM_LIMIT_BYTES = 56 * 1024 * 1024
ROW_TILE = 512
MLSTM_CHUNK = 128
MLSTM_ROWS = 2
ATTN_TILE = 2048
ATTN_KEYS = 256
FFN_CHUNK = 256
DECAY_PARTS = 3
LOG2E = math.log2(math.e)

_BF16 = jnp.bfloat16
_F32 = jnp.float32


def _params(*semantics):
    return pltpu.CompilerParams(dimension_semantics=semantics, vmem_limit_bytes=VMEM_LIMIT_BYTES)


def _resident(shape):
    return pl.BlockSpec(shape, lambda *_: (0,) * len(shape), pipeline_mode=pl.Buffered(1))


def _rms(x, gain):
    return x * lax.rsqrt(jnp.mean(x * x, axis=-1, keepdims=True) + EPS) * gain


def _norm_proj_kernel(x_ref, g_ref, *refs, transposed):
    n_out = len(transposed)
    w_refs, o_refs = refs[:n_out], refs[n_out:]
    h = _rms(x_ref[...], g_ref[...]).astype(_BF16)
    for w_ref, o_ref, tr in zip(w_refs, o_refs, transposed):
        n = w_ref.shape[0] if tr else w_ref.shape[1]
        for c0 in range(0, n, 512):
            c1 = min(c0 + 512, n)
            if tr:
                y = lax.dot_general(w_ref[c0:c1, :], h, (((1,), (1,)), ((), ())),
                                    preferred_element_type=_F32)
                o_ref[c0:c1, :] = y.astype(o_ref.dtype)
            else:
                y = jnp.dot(h, w_ref[:, c0:c1], preferred_element_type=_F32)
                o_ref[:, c0:c1] = y.astype(o_ref.dtype)


def _norm_proj(x, gain, weights, out_dtypes, name, transposed=None):
    t, d = x.shape
    tm = min(ROW_TILE, t)
    transposed = tuple(transposed or (False,) * len(weights))
    in_specs = [pl.BlockSpec((tm, d), lambda i: (i, 0)), _resident((1, d))]
    in_specs += [_resident(w.shape) for w in weights]
    out_specs, out_shape = [], []
    for w, dt, tr in zip(weights, out_dtypes, transposed):
        if tr:
            out_specs.append(pl.BlockSpec((w.shape[0], tm), lambda i: (0, i)))
            out_shape.append(jax.ShapeDtypeStruct((w.shape[0], t), dt))
        else:
            out_specs.append(pl.BlockSpec((tm, w.shape[1]), lambda i: (i, 0)))
            out_shape.append(jax.ShapeDtypeStruct((t, w.shape[1]), dt))
    return pl.pallas_call(
        functools.partial(_norm_proj_kernel, transposed=transposed),
        grid=(t // tm,), in_specs=in_specs, out_specs=out_specs, out_shape=out_shape,
        compiler_params=_params("parallel"), name=name,
    )(x, gain.reshape(1, d), *weights)


def _log_sigmoid(z):
    return jnp.minimum(z, 0.0) - jnp.log1p(jnp.exp(-jnp.abs(z)))


ONES_ROWS = 16


def _mlstm_kernel(q_ref, k_ref, o_ref, gc_ref, gr_ref, bc_ref, br_ref, hn_ref, *rest, rows):
    nh, dk, dv = MLSTM_HEADS, MLSTM_QK_DIM, MLSTM_V_DIM
    L = q_ref.shape[1]
    vt_refs, (out_ref, ct_ref, m_ref) = rest[:rows], rest[rows:]

    @pl.when(pl.program_id(1) == 0)
    def _():
        ct_ref[...] = jnp.zeros_like(ct_ref)
        m_ref[...] = jnp.zeros_like(m_ref)

    row = lax.broadcasted_iota(jnp.int32, (L, L), 0)
    col = lax.broadcasted_iota(jnp.int32, (L, L), 1)
    source_before_target = row <= col
    tril = (row >= col).astype(_F32)
    triu = source_before_target.astype(_F32)

    heads = [(bb, h) for bb in range(rows) for h in range(nh)]
    lane_qk = lax.broadcasted_iota(jnp.int32, (L, 2 * dk), 1)
    ones_blk = jnp.ones((ONES_ROWS, L), _BF16)
    first_half = lane_qk < dk

    gates = {}
    for bb in range(rows):
        gc = gc_ref[bb] + bc_ref[...]
        gr = gr_ref[bb] + br_ref[:, 0:1]
        lane_g = lax.broadcasted_iota(jnp.int32, gc.shape, 1)
        sub_g = lax.broadcasted_iota(jnp.int32, gr.shape, 0)
        lf_c = jnp.where(lane_g >= nh, _log_sigmoid(gc), 0.0)
        lf_r = jnp.where(sub_g >= nh, _log_sigmoid(gr), 0.0)
        b_c_all = jnp.dot(tril, lf_c, preferred_element_type=_F32, precision=lax.Precision.HIGHEST)
        b_r_all = jnp.dot(lf_r, triu, preferred_element_type=_F32, precision=lax.Precision.HIGHEST)
        gates[bb] = (gc, gr, b_c_all, b_r_all)

    st = {}
    for bb, h in heads:
        pair = slice((h // 2) * 2 * dk, (h // 2 + 1) * 2 * dk)
        qp, kp = q_ref[bb, :, pair], k_ref[bb, :, pair]
        qm = jnp.where(first_half if h % 2 == 0 else ~first_half, qp, jnp.zeros_like(qp))
        ct = ct_ref[bb * nh + h]
        both = lax.dot_general(jnp.concatenate([kp, ct.astype(_BF16)], axis=0), qm,
                               (((1,), (1,)), ((), ())), preferred_element_type=_F32)
        st[bb, h] = dict(kp=kp, ct=ct, s_qk=both[:L], inter=both[L:])

    for bb, h in heads:
        gc, gr, b_c_all, b_r_all = gates[bb]
        e = st[bb, h]
        g_col = gc[:, h:h + 1] - b_c_all[:, nh + h:nh + h + 1]
        b_row = b_r_all[nh + h:nh + h + 1, :]
        i_row = gr[h:h + 1, :]
        m_prev = m_ref[bb * nh + h:bb * nh + h + 1, 0:1]
        dmat = jnp.where(source_before_target, g_col + b_row, -jnp.inf)
        m_inter = b_row + m_prev
        m_t = jnp.maximum(m_inter, jnp.max(dmat, axis=0, keepdims=True))
        sd = (e["s_qk"] * jnp.exp(dmat - m_t)).astype(_BF16)
        vt_aug = jnp.concatenate([vt_refs[bb][h * dv:(h + 1) * dv, :], ones_blk], axis=0)
        tot = jnp.dot(vt_aug, sd, preferred_element_type=_F32) + jnp.exp(m_inter - m_t) * e["inter"]
        num, den = tot[:dv], tot[dv:dv + 1]
        hv = num / jnp.maximum(jnp.abs(den), jnp.exp(-m_t))
        hv = hv * lax.rsqrt(jnp.mean(hv * hv, axis=0, keepdims=True) + EPS)
        hv = (hv * hn_ref[h * dv:(h + 1) * dv, :]).T
        hv = hv * jax.nn.sigmoid(o_ref[bb, :, h * dv:(h + 1) * dv].astype(_F32))
        out_ref[bb, :, h * dv:(h + 1) * dv] = hv.astype(out_ref.dtype)
        e.update(vt_aug=vt_aug, b_row=b_row, i_row=i_row, m_prev=m_prev)

    for bb, h in heads:
        e = st[bb, h]
        b_row, m_prev = e["b_row"], e["m_prev"]
        b_last = b_row[:, L - 1:L]
        g_row = b_last - b_row + e["i_row"]
        m_new = jnp.maximum(b_last + m_prev, jnp.max(g_row, axis=-1, keepdims=True))
        decay = jnp.exp(b_last + m_prev - m_new)
        vw = (e["vt_aug"].astype(_F32) * jnp.exp(g_row - m_new)).astype(_BF16)
        ct_ref[bb * nh + h] = decay * e["ct"] + jnp.dot(vw, e["kp"], preferred_element_type=_F32)
        m_ref[bb * nh + h:bb * nh + h + 1, :] = jnp.broadcast_to(m_new, (1, m_ref.shape[1]))


def _mlstm(q, k, vt, o, gates, b_gate, head_norm):
    b, s, _ = q.shape
    nh, dk, dv = MLSTM_HEADS, MLSTM_QK_DIM, MLSTM_V_DIM
    L = min(MLSTM_CHUNK, s)
    chunks = s // L
    gates_row = jnp.swapaxes(gates[:, :, :2 * nh], 1, 2)
    bias_col = jnp.zeros((1, LANES), _F32).at[0, :2 * nh].set(b_gate)
    bias_row = b_gate.reshape(2 * nh, 1)
    norm_cols = jnp.broadcast_to(head_norm.astype(_F32)[:, None], (nh * dv, L))
    rows = MLSTM_ROWS if b % MLSTM_ROWS == 0 else 1
    tok = lambda width: pl.BlockSpec((rows, L, width), lambda bi, ci: (bi, ci, 0))
    vt_spec = lambda r: pl.BlockSpec((nh * dv, L), lambda bi, ci: (0, (bi * rows + r) * chunks + ci))
    return pl.pallas_call(
        functools.partial(_mlstm_kernel, rows=rows),
        grid=(b // rows, chunks),
        in_specs=[tok(nh * dk), tok(nh * dk), tok(nh * dv), tok(LANES),
                  pl.BlockSpec((rows, 2 * nh, L), lambda bi, ci: (bi, 0, ci)),
                  _resident((1, LANES)), _resident((2 * nh, 1)), _resident((nh * dv, L))]
                 + [vt_spec(r) for r in range(rows)],
        out_specs=tok(nh * dv),
        out_shape=jax.ShapeDtypeStruct((b, s, nh * dv), _BF16),
        scratch_shapes=[pltpu.VMEM((rows * nh, dv + ONES_ROWS, 2 * dk), _F32),
                        pltpu.VMEM((rows * nh, LANES), _F32)],
        compiler_params=_params("parallel", "arbitrary"), name="mlstm_chunk",
    )(q, k, o, gates, gates_row, bias_col, bias_row, norm_cols, *([vt] * rows))


GELU_C = math.sqrt(2.0 / math.pi)
RUN_PITCH = 72


def _mix_ffn_kernel(a0_ref, x0_ref, a_ref, x_ref, wo_ref, gmix_ref, gpre_ref, wup_ref, cw_ref, cb_ref,
                    wdn_ref, gpost_ref, o_ref,
                    h_ref, xp_ref, y_ref, carry_ref, slab_in_ref, slab_out_ref, hid_ref, *, tiles_per_seq):
    tm, d = x_ref.shape
    f = wdn_ref.shape[0]
    run = tm // SUBLANES
    n_slabs = d // LANES
    step = pl.program_id(0)

    def gather_rows(value):
        for c in range(n_slabs):
            for r in range(SUBLANES):
                slab_in_ref[c, r * RUN_PITCH:r * RUN_PITCH + run, :] = (
                    value[r * run:(r + 1) * run, c * LANES:(c + 1) * LANES])
        return jnp.concatenate(
            [jnp.concatenate([slab_in_ref[c, pl.ds(k, SUBLANES, stride=RUN_PITCH), :]
                              for k in range(run)], axis=0) for c in range(n_slabs)], axis=1)

    def scatter_rows(value, dst_ref):
        for c in range(n_slabs):
            for k in range(run):
                slab_out_ref[c, pl.ds(k, SUBLANES, stride=RUN_PITCH), :] = (
                    value[k * SUBLANES:(k + 1) * SUBLANES, c * LANES:(c + 1) * LANES])
            for r in range(SUBLANES):
                dst_ref[r * run:(r + 1) * run, c * LANES:(c + 1) * LANES] = (
                    slab_out_ref[c, r * RUN_PITCH:r * RUN_PITCH + run, :])

    def head(a_blk_ref, x_blk_ref, slot):
        mixed = jnp.dot(a_blk_ref[...], wo_ref[...], preferred_element_type=_F32)
        xp = gather_rows(x_blk_ref[...] + _rms(mixed, gmix_ref[...]))
        xp_ref[slot] = xp
        h_ref[...] = _rms(xp, gpre_ref[...]).astype(_BF16)

    @pl.when(step == 0)
    def _():
        y_ref[...] = jnp.zeros_like(y_ref)
        xp_ref[...] = jnp.zeros_like(xp_ref)
        head(a0_ref, x0_ref, 0)

    @pl.when(step % tiles_per_seq == 0)
    def _():
        carry_ref[...] = jnp.zeros_like(carry_ref)

    other = (step + 1) % 2

    scatter_rows(xp_ref[other] + _rms(y_ref[...], gpost_ref[...]), o_ref)

    h = h_ref[...]
    first_sublane = lax.broadcasted_iota(jnp.int32, (SUBLANES, FFN_CHUNK), 0) == 0

    def conv(c0, width):
        cols = slice(c0, c0 + width)
        u = jnp.dot(h, wup_ref[:, cols], preferred_element_type=_F32)
        prev = carry_ref[:, cols]
        tail1, tail2 = u[tm - SUBLANES:], u[tm - 2 * SUBLANES:tm - SUBLANES]
        carry_ref[:, cols] = u[tm - 2 * SUBLANES:]
        mask = first_sublane[:, :width]
        wrap1 = jnp.where(mask, prev[2 * SUBLANES - 1:, :], pltpu.roll(tail1, 1, axis=0))
        wrap2 = jnp.where(mask, prev[SUBLANES - 1:SUBLANES, :], pltpu.roll(tail2, 1, axis=0))
        u1 = jnp.concatenate([wrap1, u[:tm - SUBLANES]], axis=0)
        u2 = jnp.concatenate([wrap2, wrap1, u[:tm - 2 * SUBLANES]], axis=0)
        return (cb_ref[:, cols] + cw_ref[2:3, cols] * u + cw_ref[1:2, cols] * u1
                + cw_ref[0:1, cols] * u2)

    for c0 in range(0, f, FFN_CHUNK):
        width = min(FFN_CHUNK, f - c0)
        gate = conv(c0, width)
        half_val = conv(f + c0, width)
        inner = gate * ((gate * gate) * (GELU_C * 0.044715) + GELU_C)
        w = gate * half_val
        hid_ref[:, c0:c0 + width] = (w * jnp.tanh(inner) + w).astype(_BF16)

    head(a_ref, x_ref, other)

    y_ref[...] = jnp.dot(hid_ref[...], wdn_ref[...], preferred_element_type=_F32)


def _mix_ffn(a, x, w_out, g_mix, g_pre, w_up, conv_w, conv_b, w_down, g_post, seq_len):
    t, d = x.shape
    k = a.shape[1]
    f = w_down.shape[0]
    tm = min(ROW_TILE, seq_len)
    n = t // tm
    half = jnp.concatenate([jnp.ones((f,), _F32), jnp.full((f,), 0.5, _F32)])
    first = lambda width: pl.BlockSpec((tm, width), lambda i: (0, 0), pipeline_mode=pl.Buffered(1))
    ahead = lambda width: pl.BlockSpec((tm, width), lambda i: (jnp.minimum(i + 1, n - 1), 0))
    row = lambda width: _resident((1, width))
    return pl.pallas_call(
        functools.partial(_mix_ffn_kernel, tiles_per_seq=seq_len // tm),
        grid=(n + 1,),
        in_specs=[first(k), first(d), ahead(k), ahead(d), _resident((k, d)), row(d), row(d),
                  _resident((d, 2 * f)), _resident((CONV_WIDTH, 2 * f)), row(2 * f), _resident((f, d)),
                  row(d)],
        out_specs=pl.BlockSpec((tm, d), lambda i: (jnp.maximum(i - 1, 0), 0)),
        out_shape=jax.ShapeDtypeStruct((t, d), _F32),
        scratch_shapes=[pltpu.VMEM((tm, d), _BF16), pltpu.VMEM((2, tm, d), _F32),
                        pltpu.VMEM((tm, d), _F32), pltpu.VMEM((2 * SUBLANES, 2 * f), _F32),
                        pltpu.VMEM((d // LANES, SUBLANES * RUN_PITCH, LANES), _F32),
                        pltpu.VMEM((d // LANES, SUBLANES * RUN_PITCH, LANES), _F32),
                        pltpu.VMEM((tm, f), _BF16)],
        compiler_params=_params("arbitrary"), name="mix_ffn",
    )(a, x, a, x, w_out, g_mix.reshape(1, d), g_pre.reshape(1, d), w_up, conv_w * half,
      (conv_b * half).reshape(1, 2 * f), w_down, g_post.reshape(1, d))


def _decay_kernel(f_ref, bias_ref, place_ref, cq_ref, ck_ref, carry_ref):
    tm = f_ref.shape[1]

    @pl.when(pl.program_id(1) == 0)
    def _():
        carry_ref[...] = jnp.zeros_like(carry_ref)

    lane = lax.broadcasted_iota(jnp.int32, (tm, LANES), 1)
    row = lax.broadcasted_iota(jnp.int32, (tm, LANES), 0)
    c = jnp.where(lane < FOX_HEADS, _log_sigmoid(f_ref[0] + bias_ref[...]), 0.0)
    shift = 1
    while shift < tm:
        c = c + jnp.where(row >= shift, pltpu.roll(c, shift, axis=0), 0.0)
        shift *= 2
    c = carry_ref[0:1, :] + c
    carry_ref[...] = jnp.broadcast_to(c[tm - 1:tm, :], carry_ref.shape)
    c = c * LOG2E

    packed = jnp.where(lane < FOX_HEADS, 1.0, 0.0)
    packed = pltpu.roll(packed, DECAY_PARTS * FOX_HEADS, axis=1)
    rest = c
    for j in range(DECAY_PARTS):
        part = rest.astype(_BF16).astype(_F32)
        rest = rest - part
        packed = packed + (part if j == 0 else pltpu.roll(part, j * FOX_HEADS, axis=1))
    placed = jnp.dot(packed.astype(_BF16), place_ref[...], preferred_element_type=_F32)
    width = cq_ref.shape[2]
    cq_ref[0] = placed[:, :width].astype(cq_ref.dtype)
    ck_ref[0] = placed[:, width:].astype(ck_ref.dtype)


def _decay_placement():
    nh, dh, parts = FOX_HEADS, FOX_HEAD_DIM, DECAY_PARTS
    width = nh * dh
    heads = jnp.arange(nh)
    base = (heads // 2) * 2 * dh + (1 - heads % 2) * dh
    ones_row = parts * nh + heads
    place = jnp.zeros((LANES, 2 * width), _F32)
    for j in range(parts):
        place = place.at[j * nh + heads, base + j].set(1.0)
        place = place.at[ones_row, base + parts + j].set(1.0)
        place = place.at[ones_row, width + base + j].set(1.0)
        place = place.at[j * nh + heads, width + base + parts + j].set(-1.0)
    return place.astype(_BF16)


def _fox_decay(f_pre, bias):
    b, s, _ = f_pre.shape
    nh, dh = FOX_HEADS, FOX_HEAD_DIM
    tm = min(ROW_TILE, s)
    bias_row = jnp.zeros((1, LANES), _F32).at[0, :nh].set(bias)
    out = jax.ShapeDtypeStruct((b, s, nh * dh), _BF16)
    return pl.pallas_call(
        _decay_kernel,
        grid=(b, s // tm),
        in_specs=[pl.BlockSpec((1, tm, LANES), lambda bi, ti: (bi, ti, 0)), _resident((1, LANES)),
                  _resident((LANES, 2 * nh * dh))],
        out_specs=[pl.BlockSpec((1, tm, nh * dh), lambda bi, ti: (bi, ti, 0))] * 2,
        out_shape=[out, out],
        scratch_shapes=[pltpu.VMEM((SUBLANES, LANES), _F32)],
        compiler_params=_params("parallel", "arbitrary"), name="fox_decay",
    )(f_pre, bias_row, _decay_placement())


def _attn_kernel(q_ref, cq_ref, g_ref, k_ref, ck_ref, vt_ref, o_ref,
                 sa_ref, sb_ref, sa_max_ref, sb_max_ref, m_ref, acc_ref):
    tq = q_ref.shape[1]
    tk = sa_ref.shape[1]
    dh = FOX_HEAD_DIM
    qi = pl.program_id(2)
    first_q = lax.broadcasted_iota(jnp.int32, (tq, 2 * dh), 1) < dh
    first_k = lax.broadcasted_iota(jnp.int32, (tk, 2 * dh), 1) < dh
    q_pair, cq_pair = q_ref[0], cq_ref[0]
    q_aug = (jnp.where(first_q, q_pair, cq_pair), jnp.where(first_q, cq_pair, q_pair))

    ones_blk = jnp.ones((ONES_ROWS, tk), _BF16)

    m_ref[...] = jnp.full_like(m_ref, -jnp.inf)
    acc_ref[...] = jnp.zeros_like(acc_ref)

    def causal_square(s):
        key = lax.broadcasted_iota(jnp.int32, (tk, tk), 0)
        qry = lax.broadcasted_iota(jnp.int32, (tk, tk), 1)
        square = jnp.where(key <= qry, s[:, :tk], -jnp.inf)
        return square if s.shape[1] == tk else jnp.concatenate([square, s[:, tk:]], axis=1)

    def scores(j, bufs, q0, diagonal):
        s_ref, smax_ref = bufs
        start = pl.multiple_of(j * tk, tk)
        k_pair = k_ref[0, pl.ds(start, tk), :]
        ck_pair = ck_ref[0, pl.ds(start, tk), :]
        k_aug = (jnp.where(first_k, k_pair, ck_pair), jnp.where(first_k, ck_pair, k_pair))
        for hh in range(2):
            s = lax.dot_general(k_aug[hh], q_aug[hh][q0:], (((1,), (1,)), ((), ())),
                                preferred_element_type=_F32)
            if diagonal:
                s = causal_square(s)
            s_ref[hh, :, q0:] = s
            smax_ref[hh, :, q0:] = jnp.max(s, axis=0, keepdims=True)

    def accumulate(j, bufs, q0):
        s_ref, smax_ref = bufs
        start = pl.multiple_of(j * tk, tk)
        for hh in range(2):
            m_old = m_ref[hh, :, q0:]
            m_new = jnp.maximum(m_old, smax_ref[hh, :, q0:])
            alpha = jnp.exp2(m_old - m_new)
            p = jnp.exp2(s_ref[hh, :, q0:] - m_new).astype(_BF16)
            vt = jnp.concatenate([vt_ref[hh * dh:(hh + 1) * dh, pl.ds(start, tk)], ones_blk], axis=0)
            acc_ref[hh, :, q0:] = (alpha * acc_ref[hh, :, q0:]
                                   + jnp.dot(vt, p, preferred_element_type=_F32))
            m_ref[hh, :, q0:] = m_new

    per_tile = tq // tk
    buf_a, buf_b = (sa_ref, sa_max_ref), (sb_ref, sb_max_ref)

    scores(0, buf_a, 0, False)

    def body(i, carry):
        for u in range(per_tile):
            j = i * per_tile + u
            cur, nxt = (buf_a, buf_b) if u % 2 == 0 else (buf_b, buf_a)
            scores(j + 1, nxt, 0, False)
            accumulate(j, cur, 0)
        return carry

    lax.fori_loop(0, qi, body, 0)
    for u in range(per_tile):
        j = qi * per_tile + u
        cur, nxt = (buf_a, buf_b) if u % 2 == 0 else (buf_b, buf_a)
        if u + 1 < per_tile:
            scores(j + 1, nxt, (u + 1) * tk, True)
        if u == 0:
            for hh in range(2):
                square = causal_square(sa_ref[hh, :, :tk])
                sa_ref[hh, :, :tk] = square
                sa_max_ref[hh, :, :tk] = jnp.max(square, axis=0, keepdims=True)
        accumulate(j, cur, u * tk)

    halves = []
    for hh in range(2):
        acc = acc_ref[hh]
        halves.append(acc[0:dh, :] / acc[dh:dh + 1, :])
    out = jnp.concatenate(halves, axis=0).T
    o_ref[0] = (out * jax.nn.sigmoid(g_ref[0].astype(_F32))).astype(o_ref.dtype)


def _fox_attention(q, cq, g, k, ck, vt):
    b, s, hd = q.shape
    pairs = hd // LANES
    tq = min(ATTN_TILE, s)
    tk = min(ATTN_KEYS, tq // 2)
    rows = FOX_HEAD_DIM + ONES_ROWS
    qspec = pl.BlockSpec((1, tq, LANES), lambda bi, pi, qi: (bi, qi, pi))
    kspec = pl.BlockSpec((1, s, LANES), lambda bi, pi, qi: (bi, 0, pi))
    vspec = pl.BlockSpec((LANES, s), lambda bi, pi, qi: (pi, bi))
    return pl.pallas_call(
        _attn_kernel,
        grid=(b, pairs, s // tq),
        in_specs=[qspec, qspec, qspec, kspec, kspec, vspec],
        out_specs=qspec,
        out_shape=jax.ShapeDtypeStruct((b, s, hd), _BF16),
        scratch_shapes=[pltpu.VMEM((2, tk, tq), _F32), pltpu.VMEM((2, tk, tq), _F32),
                        pltpu.VMEM((2, 1, tq), _F32), pltpu.VMEM((2, 1, tq), _F32),
                        pltpu.VMEM((2, 1, tq), _F32), pltpu.VMEM((2, rows, tq), _F32)],
        compiler_params=_params("parallel", "parallel", "arbitrary"), name="fox_attention",
    )(q, cq, g, k, ck, vt)


def _pad_cols(w, width):
    return jnp.pad(w, ((0, 0), (0, width - w.shape[1])))


def kernel(x, norm_mix_pre, norm_mix_post, norm_ffn_pre, norm_ffn_post, mlstm_w_in, mlstm_b_gate,
           mlstm_norm, mlstm_w_out, kv_norm, kv_w, kv_b_f, fox_w_qg, fox_w_out, ffn_w_up,
           ffn_conv_w, ffn_conv_b, ffn_w_down):
    b, s, d = x.shape
    depth = norm_mix_pre.shape[0]
    n_a = mlstm_w_in.shape[0]
    xt = x.reshape(b * s, d)
    hk = MLSTM_HEADS * MLSTM_QK_DIM
    hv = MLSTM_HEADS * MLSTM_V_DIM
    k_sh = vt_sh = cq_sh = ck_sh = None
    bf = lambda w: w.astype(_BF16)
    m_q = bf(mlstm_w_in[:, :, :hk] * MLSTM_QK_DIM ** -0.5)
    m_k = bf(mlstm_w_in[:, :, hk:2 * hk])
    m_vt = bf(jnp.swapaxes(mlstm_w_in[:, :, 2 * hk:2 * hk + hv], 1, 2))
    m_o = bf(mlstm_w_in[:, :, 2 * hk + hv:2 * hk + 2 * hv])
    m_gates = bf(jnp.pad(mlstm_w_in[:, :, 2 * hk + 2 * hv:], ((0, 0), (0, 0), (0, LANES - 2 * MLSTM_HEADS))))
    f_q = bf(fox_w_qg[:, :, :d] * (FOX_HEAD_DIM ** -0.5 * LOG2E))
    f_g = bf(fox_w_qg[:, :, d:])
    out_w = [bf(mlstm_w_out), bf(fox_w_out)]
    up_w, down_w = bf(ffn_w_up), bf(ffn_w_down)
    seq = lambda a: a.reshape(b, s, a.shape[-1])
    for layer in range(depth):
        if layer < n_a:
            weights = [m_q[layer], m_k[layer], m_vt[layer], m_o[layer], m_gates[layer]]
            q, k, vt, o, gates = _norm_proj(xt, norm_mix_pre[layer], weights,
                                            [_BF16, _BF16, _BF16, _BF16, _F32], "mlstm_in_proj",
                                            transposed=(False, False, True, False, False))
            hs = _mlstm(seq(q), seq(k), vt, seq(o), seq(gates), mlstm_b_gate[layer], mlstm_norm[layer])
            mixed, w_out = hs.reshape(b * s, hv), out_w[0][layer]
        else:
            j = layer - n_a
            q, g = _norm_proj(xt, norm_mix_pre[layer], [f_q[j], f_g[j]], [_BF16, _BF16], "fox_in_proj")
            att = _fox_attention(seq(q), cq_sh, seq(g), k_sh, ck_sh, vt_sh)
            mixed, w_out = att.reshape(b * s, d), out_w[1][j]
        xt = _mix_ffn(mixed, xt, w_out, norm_mix_post[layer], norm_ffn_pre[layer], up_w[layer],
                      ffn_conv_w[layer], ffn_conv_b[layer], down_w[layer], norm_ffn_post[layer], s)
        if layer == n_a - 1:
            weights = [bf(kv_w[:, :d]), bf(kv_w[:, d:2 * d].T), bf(_pad_cols(kv_w[:, 2 * d:], LANES))]
            k2, vt_sh, f_pre = _norm_proj(xt, kv_norm, weights, [_BF16, _BF16, _F32], "kv_proj",
                                          transposed=(False, True, False))
            k_sh = k2.reshape(b, s, d)
            cq_sh, ck_sh = _fox_decay(f_pre.reshape(b, s, LANES), kv_b_f)
    return xt.reshape(b, s, d)
```

```python
import functools
import math

import jax
import jax.numpy as jnp
from jax import lax
from jax.experimental import pallas as pl
from jax.experimental.pallas import tpu as pltpu

EPS = 1e-6
MLSTM_HEADS = 8
MLSTM_QK_DIM = 64
MLSTM_V_DIM = 128
FOX_HEADS = 16
FOX_HEAD_DIM = 64
CONV_WIDTH = 3

LANES = 128
SUBLANES = 8
VMEM_LIMIT_BYTES = 56 * 1024 * 1024
ROW_TILE = 512
MLSTM_CHUNK = 128
MLSTM_ROWS = 2
ATTN_TILE = 2048
ATTN_KEYS = 256
FFN_CHUNK = 256
DECAY_PARTS = 3
LOG2E = math.log2(math.e)

_BF16 = jnp.bfloat16
_F32 = jnp.float32


def _params(*semantics):
    return pltpu.CompilerParams(dimension_semantics=semantics, vmem_limit_bytes=VMEM_LIMIT_BYTES)


def _resident(shape):
    return pl.BlockSpec(shape, lambda *_: (0,) * len(shape), pipeline_mode=pl.Buffered(1))


def _rms(x, gain):
    return x * lax.rsqrt(jnp.mean(x * x, axis=-1, keepdims=True) + EPS) * gain


def _norm_proj_kernel(x_ref, g_ref, *refs, transposed):
    n_out = len(transposed)
    w_refs, o_refs = refs[:n_out], refs[n_out:]
    h = _rms(x_ref[...], g_ref[...]).astype(_BF16)
    for w_ref, o_ref, tr in zip(w_refs, o_refs, transposed):
        n = w_ref.shape[0] if tr else w_ref.shape[1]
        for c0 in range(0, n, 512):
            c1 = min(c0 + 512, n)
            if tr:
                y = lax.dot_general(w_ref[c0:c1, :], h, (((1,), (1,)), ((), ())),
                                    preferred_element_type=_F32)
                o_ref[c0:c1, :] = y.astype(o_ref.dtype)
            else:
                y = jnp.dot(h, w_ref[:, c0:c1], preferred_element_type=_F32)
                o_ref[:, c0:c1] = y.astype(o_ref.dtype)


def _norm_proj(x, gain, weights, out_dtypes, name, transposed=None):
    t, d = x.shape
    tm = min(ROW_TILE, t)
    transposed = tuple(transposed or (False,) * len(weights))
    in_specs = [pl.BlockSpec((tm, d), lambda i: (i, 0)), _resident((1, d))]
    in_specs += [_resident(w.shape) for w in weights]
    out_specs, out_shape = [], []
    for w, dt, tr in zip(weights, out_dtypes, transposed):
        if tr:
            out_specs.append(pl.BlockSpec((w.shape[0], tm), lambda i: (0, i)))
            out_shape.append(jax.ShapeDtypeStruct((w.shape[0], t), dt))
        else:
            out_specs.append(pl.BlockSpec((tm, w.shape[1]), lambda i: (i, 0)))
            out_shape.append(jax.ShapeDtypeStruct((t, w.shape[1]), dt))
    return pl.pallas_call(
        functools.partial(_norm_proj_kernel, transposed=transposed),
        grid=(t // tm,), in_specs=in_specs, out_specs=out_specs, out_shape=out_shape,
        compiler_params=_params("parallel"), name=name,
    )(x, gain.reshape(1, d), *weights)


def _log_sigmoid(z):
    return jnp.minimum(z, 0.0) - jnp.log1p(jnp.exp(-jnp.abs(z)))


ONES_ROWS = 16


def _mlstm_kernel(q_ref, k_ref, o_ref, gc_ref, gr_ref, bc_ref, br_ref, hn_ref, *rest, rows):
    nh, dk, dv = MLSTM_HEADS, MLSTM_QK_DIM, MLSTM_V_DIM
    L = q_ref.shape[1]
    vt_refs, (out_ref, ct_ref, m_ref) = rest[:rows], rest[rows:]

    @pl.when(pl.program_id(1) == 0)
    def _():
        ct_ref[...] = jnp.zeros_like(ct_ref)
        m_ref[...] = jnp.zeros_like(m_ref)

    row = lax.broadcasted_iota(jnp.int32, (L, L), 0)
    col = lax.broadcasted_iota(jnp.int32, (L, L), 1)
    source_before_target = row <= col
    tril = (row >= col).astype(_F32)
    triu = source_before_target.astype(_F32)

    heads = [(bb, h) for bb in range(rows) for h in range(nh)]
    lane_qk = lax.broadcasted_iota(jnp.int32, (L, 2 * dk), 1)
    ones_blk = jnp.ones((ONES_ROWS, L), _BF16)
    first_half = lane_qk < dk

    gates = {}
    for bb in range(rows):
        gc = gc_ref[bb] + bc_ref[...]
        gr = gr_ref[bb] + br_ref[:, 0:1]
        lane_g = lax.broadcasted_iota(jnp.int32, gc.shape, 1)
        sub_g = lax.broadcasted_iota(jnp.int32, gr.shape, 0)
        lf_c = jnp.where(lane_g >= nh, _log_sigmoid(gc), 0.0)
        lf_r = jnp.where(sub_g >= nh, _log_sigmoid(gr), 0.0)
        b_c_all = jnp.dot(tril, lf_c, preferred_element_type=_F32, precision=lax.Precision.HIGHEST)
        b_r_all = jnp.dot(lf_r, triu, preferred_element_type=_F32, precision=lax.Precision.HIGHEST)
        gates[bb] = (gc, gr, b_c_all, b_r_all)

    st = {}
    for bb, h in heads:
        pair = slice((h // 2) * 2 * dk, (h // 2 + 1) * 2 * dk)
        qp, kp = q_ref[bb, :, pair], k_ref[bb, :, pair]
        qm = jnp.where(first_half if h % 2 == 0 else ~first_half, qp, jnp.zeros_like(qp))
        ct = ct_ref[bb * nh + h]
        both = lax.dot_general(jnp.concatenate([kp, ct.astype(_BF16)], axis=0), qm,
                               (((1,), (1,)), ((), ())), preferred_element_type=_F32)
        st[bb, h] = dict(kp=kp, ct=ct, s_qk=both[:L], inter=both[L:])

    for bb, h in heads:
        gc, gr, b_c_all, b_r_all = gates[bb]
        e = st[bb, h]
        g_col = gc[:, h:h + 1] - b_c_all[:, nh + h:nh + h + 1]
        b_row = b_r_all[nh + h:nh + h + 1, :]
        i_row = gr[h:h + 1, :]
        m_prev = m_ref[bb * nh + h:bb * nh + h + 1, 0:1]
        dmat = jnp.where(source_before_target, g_col + b_row, -jnp.inf)
        m_inter = b_row + m_prev
        m_t = jnp.maximum(m_inter, jnp.max(dmat, axis=0, keepdims=True))
        sd = (e["s_qk"] * jnp.exp(dmat - m_t)).astype(_BF16)
        vt_aug = jnp.concatenate([vt_refs[bb][h * dv:(h + 1) * dv, :], ones_blk], axis=0)
        tot = jnp.dot(vt_aug, sd, preferred_element_type=_F32) + jnp.exp(m_inter - m_t) * e["inter"]
        num, den = tot[:dv], tot[dv:dv + 1]
        hv = num / jnp.maximum(jnp.abs(den), jnp.exp(-m_t))
        hv = hv * lax.rsqrt(jnp.mean(hv * hv, axis=0, keepdims=True) + EPS)
        hv = (hv * hn_ref[h * dv:(h + 1) * dv, :]).T
        hv = hv * jax.nn.sigmoid(o_ref[bb, :, h * dv:(h + 1) * dv].astype(_F32))
        out_ref[bb, :, h * dv:(h + 1) * dv] = hv.astype(out_ref.dtype)
        e.update(vt_aug=vt_aug, b_row=b_row, i_row=i_row, m_prev=m_prev)

    for bb, h in heads:
        e = st[bb, h]
        b_row, m_prev = e["b_row"], e["m_prev"]
        b_last = b_row[:, L - 1:L]
        g_row = b_last - b_row + e["i_row"]
        m_new = jnp.maximum(b_last + m_prev, jnp.max(g_row, axis=-1, keepdims=True))
        decay = jnp.exp(b_last + m_prev - m_new)
        vw = (e["vt_aug"].astype(_F32) * jnp.exp(g_row - m_new)).astype(_BF16)
        ct_ref[bb * nh + h] = decay * e["ct"] + jnp.dot(vw, e["kp"], preferred_element_type=_F32)
        m_ref[bb * nh + h:bb * nh + h + 1, :] = jnp.broadcast_to(m_new, (1, m_ref.shape[1]))


def _mlstm(q, k, vt, o, gates, b_gate, head_norm):
    b, s, _ = q.shape
    nh, dk, dv = MLSTM_HEADS, MLSTM_QK_DIM, MLSTM_V_DIM
    L = min(MLSTM_CHUNK, s)
    chunks = s // L
    gates_row = jnp.swapaxes(gates[:, :, :2 * nh], 1, 2)
    bias_col = jnp.zeros((1, LANES), _F32).at[0, :2 * nh].set(b_gate)
    bias_row = b_gate.reshape(2 * nh, 1)
    norm_cols = jnp.broadcast_to(head_norm.astype(_F32)[:, None], (nh * dv, L))
    rows = MLSTM_ROWS if b % MLSTM_ROWS == 0 else 1
    tok = lambda width: pl.BlockSpec((rows, L, width), lambda bi, ci: (bi, ci, 0))
    vt_spec = lambda r: pl.BlockSpec((nh * dv, L), lambda bi, ci: (0, (bi * rows + r) * chunks + ci))
    return pl.pallas_call(
        functools.partial(_mlstm_kernel, rows=rows),
        grid=(b // rows, chunks),
        in_specs=[tok(nh * dk), tok(nh * dk), tok(nh * dv), tok(LANES),
                  pl.BlockSpec((rows, 2 * nh, L), lambda bi, ci: (bi, 0, ci)),
                  _resident((1, LANES)), _resident((2 * nh, 1)), _resident((nh * dv, L))]
                 + [vt_spec(r) for r in range(rows)],
        out_specs=tok(nh * dv),
        out_shape=jax.ShapeDtypeStruct((b, s, nh * dv), _BF16),
        scratch_shapes=[pltpu.VMEM((rows * nh, dv + ONES_ROWS, 2 * dk), _F32),
                        pltpu.VMEM((rows * nh, LANES), _F32)],
        compiler_params=_params("parallel", "arbitrary"), name="mlstm_chunk",
    )(q, k, o, gates, gates_row, bias_col, bias_row, norm_cols, *([vt] * rows))


GELU_C = math.sqrt(2.0 / math.pi)
RUN_PITCH = 72


def _mix_ffn_kernel(a0_ref, x0_ref, a_ref, x_ref, wo_ref, gmix_ref, gpre_ref, wup_ref, cw_ref, cb_ref,
                    wdn_ref, gpost_ref, o_ref,
                    h_ref, xp_ref, y_ref, carry_ref, slab_in_ref, slab_out_ref, hid_ref, *, tiles_per_seq):
    tm, d = x_ref.shape
    f = wdn_ref.shape[0]
    run = tm // SUBLANES
    n_slabs = d // LANES
    step = pl.program_id(0)

    n_slices = SUBLANES
    rows_of = lambda i: slice(i * run, (i + 1) * run)

    def mix_rows(mixed, x_blk_ref, r, zero):
        x1 = x_blk_ref[rows_of(r), :] + _rms(mixed[rows_of(r)] + zero, gmix_ref[...])
        for c in range(n_slabs):
            slab_in_ref[c, r * RUN_PITCH:r * RUN_PITCH + run, :] = x1[:, c * LANES:(c + 1) * LANES]

    def gather_slice(j, slot, zero):
        xp = jnp.concatenate(
            [jnp.concatenate([slab_in_ref[c, pl.ds(k, SUBLANES, stride=RUN_PITCH), :]
                              for k in range(j * SUBLANES, (j + 1) * SUBLANES)], axis=0)
             for c in range(n_slabs)], axis=1) + zero
        xp_ref[slot, rows_of(j), :] = xp
        h_ref[slot, rows_of(j), :] = _rms(xp, gpre_ref[...]).astype(_BF16)

    def finish_slice(j, slot, zero):
        out = xp_ref[slot, rows_of(j), :] + _rms(y_ref[slot, rows_of(j), :] + zero, gpost_ref[...])
        for c in range(n_slabs):
            for kk in range(SUBLANES):
                slab_out_ref[c, pl.ds(j * SUBLANES + kk, SUBLANES, stride=RUN_PITCH), :] = (
                    out[kk * SUBLANES:(kk + 1) * SUBLANES, c * LANES:(c + 1) * LANES])

    def zero_after(value):
        bits = pltpu.bitcast(value[:SUBLANES, :LANES], jnp.uint32)
        bits = lax.shift_right_logical(lax.shift_right_logical(bits, jnp.uint32(16)), jnp.uint32(16))
        return jnp.concatenate([pltpu.bitcast(bits, _F32)[0:1, :]] * n_slabs, axis=1)

    no_wait = jnp.zeros((1, d), _F32)

    @pl.when(step == 0)
    def _():
        y_ref[...] = jnp.zeros_like(y_ref)
        xp_ref[...] = jnp.zeros_like(xp_ref)
        mixed0 = jnp.dot(a0_ref[...], wo_ref[...], preferred_element_type=_F32)
        for r in range(n_slices):
            mix_rows(mixed0, x0_ref, r, no_wait)
        for j in range(n_slices):
            gather_slice(j, 0, no_wait)

    @pl.when(step % tiles_per_seq == 0)
    def _():
        carry_ref[...] = jnp.zeros_like(carry_ref)

    cur = step % 2
    other = (step + 1) % 2

    mixed = jnp.dot(a_ref[...], wo_ref[...], preferred_element_type=_F32)

    h = h_ref[cur]
    first_sublane = lax.broadcasted_iota(jnp.int32, (SUBLANES, FFN_CHUNK), 0) == 0

    def conv(c0, width):
        cols = slice(c0, c0 + width)
        u = jnp.dot(h, wup_ref[:, cols], preferred_element_type=_F32)
        prev = carry_ref[:, cols]
        tail1, tail2 = u[tm - SUBLANES:], u[tm - 2 * SUBLANES:tm - SUBLANES]
        carry_ref[:, cols] = u[tm - 2 * SUBLANES:]
        mask = first_sublane[:, :width]
        wrap1 = jnp.where(mask, prev[2 * SUBLANES - 1:, :], pltpu.roll(tail1, 1, axis=0))
        wrap2 = jnp.where(mask, prev[SUBLANES - 1:SUBLANES, :], pltpu.roll(tail2, 1, axis=0))
        u1 = jnp.concatenate([wrap1, u[:tm - SUBLANES]], axis=0)
        u2 = jnp.concatenate([wrap2, wrap1, u[:tm - 2 * SUBLANES]], axis=0)
        return (cb_ref[:, cols] + cw_ref[2:3, cols] * u + cw_ref[1:2, cols] * u1
                + cw_ref[0:1, cols] * u2), zero_after(u)

    pieces = ([functools.partial(piece, j) for j in range(n_slices)
               for piece in (functools.partial(finish_slice, slot=other),
                             functools.partial(mix_rows, mixed, x_ref))]
              + [functools.partial(gather_slice, j, other) for j in range(n_slices)])
    chunks = list(range(0, f, FFN_CHUNK))
    last_tie = 2 * len(chunks) - 1
    for i, c0 in enumerate(chunks):
        width = min(FFN_CHUNK, f - c0)
        gate, after_gate = conv(c0, width)
        half_val, after_val = conv(f + c0, width)
        inner = gate * ((gate * gate) * (GELU_C * 0.044715) + GELU_C)
        w = gate * half_val
        hid_ref[:, c0:c0 + width] = (w * jnp.tanh(inner) + w).astype(_BF16)
        for tie, zero in ((2 * i, after_gate), (2 * i + 1, after_val)):
            for t, piece in enumerate(pieces):
                if min(t, last_tie) == tie:
                    piece(zero=zero)

    y_ref[cur] = jnp.dot(hid_ref[...], wdn_ref[...], preferred_element_type=_F32)
    for c in range(n_slabs):
        for r in range(SUBLANES):
            o_ref[r * run:(r + 1) * run, c * LANES:(c + 1) * LANES] = (
                slab_out_ref[c, r * RUN_PITCH:r * RUN_PITCH + run, :])


def _mix_ffn(a, x, w_out, g_mix, g_pre, w_up, conv_w, conv_b, w_down, g_post, seq_len):
    t, d = x.shape
    k = a.shape[1]
    f = w_down.shape[0]
    tm = min(ROW_TILE, seq_len)
    n = t // tm
    half = jnp.concatenate([jnp.ones((f,), _F32), jnp.full((f,), 0.5, _F32)])
    first = lambda width: pl.BlockSpec((tm, width), lambda i: (0, 0), pipeline_mode=pl.Buffered(1))
    ahead = lambda width: pl.BlockSpec((tm, width), lambda i: (jnp.minimum(i + 1, n - 1), 0))
    row = lambda width: _resident((1, width))
    return pl.pallas_call(
        functools.partial(_mix_ffn_kernel, tiles_per_seq=seq_len // tm),
        grid=(n + 1,),
        in_specs=[first(k), first(d), ahead(k), ahead(d), _resident((k, d)), row(d), row(d),
                  _resident((d, 2 * f)), _resident((CONV_WIDTH, 2 * f)), row(2 * f), _resident((f, d)),
                  row(d)],
        out_specs=pl.BlockSpec((tm, d), lambda i: (jnp.maximum(i - 1, 0), 0)),
        out_shape=jax.ShapeDtypeStruct((t, d), _F32),
        scratch_shapes=[pltpu.VMEM((2, tm, d), _BF16), pltpu.VMEM((2, tm, d), _F32),
                        pltpu.VMEM((2, tm, d), _F32), pltpu.VMEM((2 * SUBLANES, 2 * f), _F32),
                        pltpu.VMEM((d // LANES, SUBLANES * RUN_PITCH, LANES), _F32),
                        pltpu.VMEM((d // LANES, SUBLANES * RUN_PITCH, LANES), _F32),
                        pltpu.VMEM((tm, f), _BF16)],
        compiler_params=_params("arbitrary"), name="mix_ffn",
    )(a, x, a, x, w_out, g_mix.reshape(1, d), g_pre.reshape(1, d), w_up, conv_w * half,
      (conv_b * half).reshape(1, 2 * f), w_down, g_post.reshape(1, d))


def _decay_kernel(f_ref, bias_ref, place_ref, cq_ref, ck_ref, carry_ref):
    tm = f_ref.shape[1]

    @pl.when(pl.program_id(1) == 0)
    def _():
        carry_ref[...] = jnp.zeros_like(carry_ref)

    lane = lax.broadcasted_iota(jnp.int32, (tm, LANES), 1)
    row = lax.broadcasted_iota(jnp.int32, (tm, LANES), 0)
    c = jnp.where(lane < FOX_HEADS, _log_sigmoid(f_ref[0] + bias_ref[...]), 0.0)
    shift = 1
    while shift < tm:
        c = c + jnp.where(row >= shift, pltpu.roll(c, shift, axis=0), 0.0)
        shift *= 2
    c = carry_ref[0:1, :] + c
    carry_ref[...] = jnp.broadcast_to(c[tm - 1:tm, :], carry_ref.shape)
    c = c * LOG2E

    packed = jnp.where(lane < FOX_HEADS, 1.0, 0.0)
    packed = pltpu.roll(packed, DECAY_PARTS * FOX_HEADS, axis=1)
    rest = c
    for j in range(DECAY_PARTS):
        part = rest.astype(_BF16).astype(_F32)
        rest = rest - part
        packed = packed + (part if j == 0 else pltpu.roll(part, j * FOX_HEADS, axis=1))
    placed = jnp.dot(packed.astype(_BF16), place_ref[...], preferred_element_type=_F32)
    width = cq_ref.shape[2]
    cq_ref[0] = placed[:, :width].astype(cq_ref.dtype)
    ck_ref[0] = placed[:, width:].astype(ck_ref.dtype)


def _decay_placement():
    nh, dh, parts = FOX_HEADS, FOX_HEAD_DIM, DECAY_PARTS
    width = nh * dh
    heads = jnp.arange(nh)
    base = (heads // 2) * 2 * dh + (1 - heads % 2) * dh
    ones_row = parts * nh + heads
    place = jnp.zeros((LANES, 2 * width), _F32)
    for j in range(parts):
        place = place.at[j * nh + heads, base + j].set(1.0)
        place = place.at[ones_row, base + parts + j].set(1.0)
        place = place.at[ones_row, width + base + j].set(1.0)
        place = place.at[j * nh + heads, width + base + parts + j].set(-1.0)
    return place.astype(_BF16)


def _fox_decay(f_pre, bias):
    b, s, _ = f_pre.shape
    nh, dh = FOX_HEADS, FOX_HEAD_DIM
    tm = min(ROW_TILE, s)
    bias_row = jnp.zeros((1, LANES), _F32).at[0, :nh].set(bias)
    out = jax.ShapeDtypeStruct((b, s, nh * dh), _BF16)
    return pl.pallas_call(
        _decay_kernel,
        grid=(b, s // tm),
        in_specs=[pl.BlockSpec((1, tm, LANES), lambda bi, ti: (bi, ti, 0)), _resident((1, LANES)),
                  _resident((LANES, 2 * nh * dh))],
        out_specs=[pl.BlockSpec((1, tm, nh * dh), lambda bi, ti: (bi, ti, 0))] * 2,
        out_shape=[out, out],
        scratch_shapes=[pltpu.VMEM((SUBLANES, LANES), _F32)],
        compiler_params=_params("parallel", "arbitrary"), name="fox_decay",
    )(f_pre, bias_row, _decay_placement())


def _attn_kernel(q_ref, cq_ref, g_ref, k_ref, ck_ref, vt_ref, o_ref,
                 sa_ref, sb_ref, sa_max_ref, sb_max_ref, m_ref, acc_ref):
    tq = q_ref.shape[1]
    tk = sa_ref.shape[1]
    dh = FOX_HEAD_DIM
    qi = pl.program_id(2)
    first_q = lax.broadcasted_iota(jnp.int32, (tq, 2 * dh), 1) < dh
    first_k = lax.broadcasted_iota(jnp.int32, (tk, 2 * dh), 1) < dh
    q_pair, cq_pair = q_ref[0], cq_ref[0]
    q_aug = (jnp.where(first_q, q_pair, cq_pair), jnp.where(first_q, cq_pair, q_pair))

    ones_blk = jnp.ones((ONES_ROWS, tk), _BF16)

    m_ref[...] = jnp.full_like(m_ref, -jnp.inf)
    acc_ref[...] = jnp.zeros_like(acc_ref)

    def causal_square(s):
        key = lax.broadcasted_iota(jnp.int32, (tk, tk), 0)
        qry = lax.broadcasted_iota(jnp.int32, (tk, tk), 1)
        square = jnp.where(key <= qry, s[:, :tk], -jnp.inf)
        return square if s.shape[1] == tk else jnp.concatenate([square, s[:, tk:]], axis=1)

    def scores(j, bufs, q0, diagonal):
        s_ref, smax_ref = bufs
        start = pl.multiple_of(j * tk, tk)
        k_pair = k_ref[0, pl.ds(start, tk), :]
        ck_pair = ck_ref[0, pl.ds(start, tk), :]
        k_aug = (jnp.where(first_k, k_pair, ck_pair), jnp.where(first_k, ck_pair, k_pair))
        for hh in range(2):
            s = lax.dot_general(k_aug[hh], q_aug[hh][q0:], (((1,), (1,)), ((), ())),
                                preferred_element_type=_F32)
            if diagonal:
                s = causal_square(s)
            s_ref[hh, :, q0:] = s
            smax_ref[hh, :, q0:] = jnp.max(s, axis=0, keepdims=True)

    def accumulate(j, bufs, q0):
        s_ref, smax_ref = bufs
        start = pl.multiple_of(j * tk, tk)
        for hh in range(2):
            m_old = m_ref[hh, :, q0:]
            m_new = jnp.maximum(m_old, smax_ref[hh, :, q0:])
            alpha = jnp.exp2(m_old - m_new)
            p = jnp.exp2(s_ref[hh, :, q0:] - m_new).astype(_BF16)
            vt = jnp.concatenate([vt_ref[hh * dh:(hh + 1) * dh, pl.ds(start, tk)], ones_blk], axis=0)
            acc_ref[hh, :, q0:] = (alpha * acc_ref[hh, :, q0:]
                                   + jnp.dot(vt, p, preferred_element_type=_F32))
            m_ref[hh, :, q0:] = m_new

    per_tile = tq // tk
    buf_a, buf_b = (sa_ref, sa_max_ref), (sb_ref, sb_max_ref)

    scores(0, buf_a, 0, False)

    def body(i, carry):
        for u in range(per_tile):
            j = i * per_tile + u
            cur, nxt = (buf_a, buf_b) if u % 2 == 0 else (buf_b, buf_a)
            scores(j + 1, nxt, 0, False)
            accumulate(j, cur, 0)
        return carry

    lax.fori_loop(0, qi, body, 0)
    for u in range(per_tile):
        j = qi * per_tile + u
        cur, nxt = (buf_a, buf_b) if u % 2 == 0 else (buf_b, buf_a)
        if u + 1 < per_tile:
            scores(j + 1, nxt, (u + 1) * tk, True)
        if u == 0:
            for hh in range(2):
                square = causal_square(sa_ref[hh, :, :tk])
                sa_ref[hh, :, :tk] = square
                sa_max_ref[hh, :, :tk] = jnp.max(square, axis=0, keepdims=True)
        accumulate(j, cur, u * tk)

    halves = []
    for hh in range(2):
        acc = acc_ref[hh]
        halves.append(acc[0:dh, :] / acc[dh:dh + 1, :])
    out = jnp.concatenate(halves, axis=0).T
    o_ref[0] = (out * jax.nn.sigmoid(g_ref[0].astype(_F32))).astype(o_ref.dtype)


def _fox_attention(q, cq, g, k, ck, vt):
    b, s, hd = q.shape
    pairs = hd // LANES
    tq = min(ATTN_TILE, s)
    tk = min(ATTN_KEYS, tq // 2)
    rows = FOX_HEAD_DIM + ONES_ROWS
    qspec = pl.BlockSpec((1, tq, LANES), lambda bi, pi, qi: (bi, qi, pi))
    kspec = pl.BlockSpec((1, s, LANES), lambda bi, pi, qi: (bi, 0, pi))
    vspec = pl.BlockSpec((LANES, s), lambda bi, pi, qi: (pi, bi))
    return pl.pallas_call(
        _attn_kernel,
        grid=(b, pairs, s // tq),
        in_specs=[qspec, qspec, qspec, kspec, kspec, vspec],
        out_specs=qspec,
        out_shape=jax.ShapeDtypeStruct((b, s, hd), _BF16),
        scratch_shapes=[pltpu.VMEM((2, tk, tq), _F32), pltpu.VMEM((2, tk, tq), _F32),
                        pltpu.VMEM((2, 1, tq), _F32), pltpu.VMEM((2, 1, tq), _F32),
                        pltpu.VMEM((2, 1, tq), _F32), pltpu.VMEM((2, rows, tq), _F32)],
        compiler_params=_params("parallel", "parallel", "arbitrary"), name="fox_attention",
    )(q, cq, g, k, ck, vt)


def _pad_cols(w, width):
    return jnp.pad(w, ((0, 0), (0, width - w.shape[1])))


def kernel(x, norm_mix_pre, norm_mix_post, norm_ffn_pre, norm_ffn_post, mlstm_w_in, mlstm_b_gate,
           mlstm_norm, mlstm_w_out, kv_norm, kv_w, kv_b_f, fox_w_qg, fox_w_out, ffn_w_up,
           ffn_conv_w, ffn_conv_b, ffn_w_down):
    b, s, d = x.shape
    depth = norm_mix_pre.shape[0]
    n_a = mlstm_w_in.shape[0]
    xt = x.reshape(b * s, d)
    hk = MLSTM_HEADS * MLSTM_QK_DIM
    hv = MLSTM_HEADS * MLSTM_V_DIM
    k_sh = vt_sh = cq_sh = ck_sh = None
    bf = lambda w: w.astype(_BF16)
    m_q = bf(mlstm_w_in[:, :, :hk] * MLSTM_QK_DIM ** -0.5)
    m_k = bf(mlstm_w_in[:, :, hk:2 * hk])
    m_vt = bf(jnp.swapaxes(mlstm_w_in[:, :, 2 * hk:2 * hk + hv], 1, 2))
    m_o = bf(mlstm_w_in[:, :, 2 * hk + hv:2 * hk + 2 * hv])
    m_gates = bf(jnp.pad(mlstm_w_in[:, :, 2 * hk + 2 * hv:], ((0, 0), (0, 0), (0, LANES - 2 * MLSTM_HEADS))))
    f_q = bf(fox_w_qg[:, :, :d] * (FOX_HEAD_DIM ** -0.5 * LOG2E))
    f_g = bf(fox_w_qg[:, :, d:])
    out_w = [bf(mlstm_w_out), bf(fox_w_out)]
    up_w, down_w = bf(ffn_w_up), bf(ffn_w_down)
    seq = lambda a: a.reshape(b, s, a.shape[-1])
    for layer in range(depth):
        if layer < n_a:
            weights = [m_q[layer], m_k[layer], m_vt[layer], m_o[layer], m_gates[layer]]
            q, k, vt, o, gates = _norm_proj(xt, norm_mix_pre[layer], weights,
                                            [_BF16, _BF16, _BF16, _BF16, _F32], "mlstm_in_proj",
                                            transposed=(False, False, True, False, False))
            hs = _mlstm(seq(q), seq(k), vt, seq(o), seq(gates), mlstm_b_gate[layer], mlstm_norm[layer])
            mixed, w_out = hs.reshape(b * s, hv), out_w[0][layer]
        else:
            j = layer - n_a
            q, g = _norm_proj(xt, norm_mix_pre[layer], [f_q[j], f_g[j]], [_BF16, _BF16], "fox_in_proj")
            att = _fox_attention(seq(q), cq_sh, seq(g), k_sh, ck_sh, vt_sh)
            mixed, w_out = att.reshape(b * s, d), out_w[1][j]
        xt = _mix_ffn(mixed, xt, w_out, norm_mix_post[layer], norm_ffn_pre[layer], up_w[layer],
                      ffn_conv_w[layer], ffn_conv_b[layer], down_w[layer], norm_ffn_post[layer], s)
        if layer == n_a - 1:
            weights = [bf(kv_w[:, :d]), bf(kv_w[:, d:2 * d].T), bf(_pad_cols(kv_w[:, 2 * d:], LANES))]
            k2, vt_sh, f_pre = _norm_proj(xt, kv_norm, weights, [_BF16, _BF16, _F32], "kv_proj",
                                          transposed=(False, True, False))
            k_sh = k2.reshape(b, s, d)
            cq_sh, ck_sh = _fox_decay(f_pre.reshape(b, s, LANES), kv_b_f)
    return xt.reshape(b, s, d)
```

```python
import functools
import math

import jax
import jax.numpy as jnp
from jax import lax
from jax.experimental import pallas as pl
from jax.experimental.pallas import tpu as pltpu

EPS = 1e-6
MLSTM_HEADS = 8
MLSTM_QK_DIM = 64
MLSTM_V_DIM = 128
FOX_HEADS = 16
FOX_HEAD_DIM = 64
CONV_WIDTH = 3

LANES = 128
SUBLANES = 8
VMEM_LIMIT_BYTES = 56 * 1024 * 1024
ROW_TILE = 512
MLSTM_CHUNK = 128
MLSTM_ROWS = 2
ATTN_TILE = 2048
ATTN_KEYS = 256
FFN_CHUNK = 256
DECAY_PARTS = 3
LOG2E = math.log2(math.e)

_BF16 = jnp.bfloat16
_F32 = jnp.float32


def _params(*semantics):
    return pltpu.CompilerParams(dimension_semantics=semantics, vmem_limit_bytes=VMEM_LIMIT_BYTES)


def _resident(shape):
    return pl.BlockSpec(shape, lambda *_: (0,) * len(shape), pipeline_mode=pl.Buffered(1))


def _rms(x, gain):
    return x * lax.rsqrt(jnp.mean(x * x, axis=-1, keepdims=True) + EPS) * gain


def _norm_proj_kernel(x_ref, g_ref, *refs, transposed):
    n_out = len(transposed)
    w_refs, o_refs = refs[:n_out], refs[n_out:]
    h = _rms(x_ref[...], g_ref[...]).astype(_BF16)
    for w_ref, o_ref, tr in zip(w_refs, o_refs, transposed):
        n = w_ref.shape[0] if tr else w_ref.shape[1]
        for c0 in range(0, n, 512):
            c1 = min(c0 + 512, n)
            if tr:
                y = lax.dot_general(w_ref[c0:c1, :], h, (((1,), (1,)), ((), ())),
                                    preferred_element_type=_F32)
                o_ref[c0:c1, :] = y.astype(o_ref.dtype)
            else:
                y = jnp.dot(h, w_ref[:, c0:c1], preferred_element_type=_F32)
                o_ref[:, c0:c1] = y.astype(o_ref.dtype)


def _norm_proj(x, gain, weights, out_dtypes, name, transposed=None):
    t, d = x.shape
    tm = min(ROW_TILE, t)
    transposed = tuple(transposed or (False,) * len(weights))
    in_specs = [pl.BlockSpec((tm, d), lambda i: (i, 0)), _resident((1, d))]
    in_specs += [_resident(w.shape) for w in weights]
    out_specs, out_shape = [], []
    for w, dt, tr in zip(weights, out_dtypes, transposed):
        if tr:
            out_specs.append(pl.BlockSpec((w.shape[0], tm), lambda i: (0, i)))
            out_shape.append(jax.ShapeDtypeStruct((w.shape[0], t), dt))
        else:
            out_specs.append(pl.BlockSpec((tm, w.shape[1]), lambda i: (i, 0)))
            out_shape.append(jax.ShapeDtypeStruct((t, w.shape[1]), dt))
    return pl.pallas_call(
        functools.partial(_norm_proj_kernel, transposed=transposed),
        grid=(t // tm,), in_specs=in_specs, out_specs=out_specs, out_shape=out_shape,
        compiler_params=_params("parallel"), name=name,
    )(x, gain.reshape(1, d), *weights)


def _log_sigmoid(z):
    return jnp.minimum(z, 0.0) - jnp.log1p(jnp.exp(-jnp.abs(z)))


ONES_ROWS = 16


def _mlstm_kernel(q_ref, k_ref, o_ref, gc_ref, gr_ref, bc_ref, br_ref, hn_ref, *rest, rows):
    nh, dk, dv = MLSTM_HEADS, MLSTM_QK_DIM, MLSTM_V_DIM
    L = q_ref.shape[1]
    vt_refs, (out_ref, ct_ref, m_ref) = rest[:rows], rest[rows:]

    @pl.when(pl.program_id(1) == 0)
    def _():
        ct_ref[...] = jnp.zeros_like(ct_ref)
        m_ref[...] = jnp.zeros_like(m_ref)

    row = lax.broadcasted_iota(jnp.int32, (L, L), 0)
    col = lax.broadcasted_iota(jnp.int32, (L, L), 1)
    source_before_target = row <= col
    tril = (row >= col).astype(_F32)
    triu = source_before_target.astype(_F32)

    heads = [(bb, h) for bb in range(rows) for h in range(nh)]
    lane_qk = lax.broadcasted_iota(jnp.int32, (L, 2 * dk), 1)
    ones_blk = jnp.ones((ONES_ROWS, L), _BF16)
    first_half = lane_qk < dk

    gates = {}
    for bb in range(rows):
        gc = gc_ref[bb] + bc_ref[...]
        gr = gr_ref[bb] + br_ref[:, 0:1]
        lane_g = lax.broadcasted_iota(jnp.int32, gc.shape, 1)
        sub_g = lax.broadcasted_iota(jnp.int32, gr.shape, 0)
        lf_c = jnp.where(lane_g >= nh, _log_sigmoid(gc), 0.0)
        lf_r = jnp.where(sub_g >= nh, _log_sigmoid(gr), 0.0)
        b_c_all = jnp.dot(tril, lf_c, preferred_element_type=_F32, precision=lax.Precision.HIGHEST)
        b_r_all = jnp.dot(lf_r, triu, preferred_element_type=_F32, precision=lax.Precision.HIGHEST)
        gates[bb] = (gc, gr, b_c_all, b_r_all)

    st = {}
    for bb, h in heads:
        pair = slice((h // 2) * 2 * dk, (h // 2 + 1) * 2 * dk)
        qp, kp = q_ref[bb, :, pair], k_ref[bb, :, pair]
        qm = jnp.where(first_half if h % 2 == 0 else ~first_half, qp, jnp.zeros_like(qp))
        ct = ct_ref[bb * nh + h]
        both = lax.dot_general(jnp.concatenate([kp, ct.astype(_BF16)], axis=0), qm,
                               (((1,), (1,)), ((), ())), preferred_element_type=_F32)
        st[bb, h] = dict(kp=kp, ct=ct, s_qk=both[:L], inter=both[L:])

    for bb, h in heads:
        gc, gr, b_c_all, b_r_all = gates[bb]
        e = st[bb, h]
        g_col = gc[:, h:h + 1] - b_c_all[:, nh + h:nh + h + 1]
        b_row = b_r_all[nh + h:nh + h + 1, :]
        i_row = gr[h:h + 1, :]
        m_prev = m_ref[bb * nh + h:bb * nh + h + 1, 0:1]
        dmat = jnp.where(source_before_target, g_col + b_row, -jnp.inf)
        m_inter = b_row + m_prev
        m_t = jnp.maximum(m_inter, jnp.max(dmat, axis=0, keepdims=True))
        sd = (e["s_qk"] * jnp.exp(dmat - m_t)).astype(_BF16)
        vt_aug = jnp.concatenate([vt_refs[bb][h * dv:(h + 1) * dv, :], ones_blk], axis=0)
        tot = jnp.dot(vt_aug, sd, preferred_element_type=_F32) + jnp.exp(m_inter - m_t) * e["inter"]
        num, den = tot[:dv], tot[dv:dv + 1]
        hv = num / jnp.maximum(jnp.abs(den), jnp.exp(-m_t))
        hv = hv * lax.rsqrt(jnp.mean(hv * hv, axis=0, keepdims=True) + EPS)
        hv = (hv * hn_ref[h * dv:(h + 1) * dv, :]).T
        hv = hv * jax.nn.sigmoid(o_ref[bb, :, h * dv:(h + 1) * dv].astype(_F32))
        out_ref[bb, :, h * dv:(h + 1) * dv] = hv.astype(out_ref.dtype)
        e.update(vt_aug=vt_aug, b_row=b_row, i_row=i_row, m_prev=m_prev)

    for bb, h in heads:
        e = st[bb, h]
        b_row, m_prev = e["b_row"], e["m_prev"]
        b_last = b_row[:, L - 1:L]
        g_row = b_last - b_row + e["i_row"]
        m_new = jnp.maximum(b_last + m_prev, jnp.max(g_row, axis=-1, keepdims=True))
        decay = jnp.exp(b_last + m_prev - m_new)
        vw = (e["vt_aug"].astype(_F32) * jnp.exp(g_row - m_new)).astype(_BF16)
        ct_ref[bb * nh + h] = decay * e["ct"] + jnp.dot(vw, e["kp"], preferred_element_type=_F32)
        m_ref[bb * nh + h:bb * nh + h + 1, :] = jnp.broadcast_to(m_new, (1, m_ref.shape[1]))


def _mlstm(q, k, vt, o, gates, b_gate, head_norm):
    b, s, _ = q.shape
    nh, dk, dv = MLSTM_HEADS, MLSTM_QK_DIM, MLSTM_V_DIM
    L = min(MLSTM_CHUNK, s)
    chunks = s // L
    gates_row = jnp.swapaxes(gates[:, :, :2 * nh], 1, 2)
    bias_col = jnp.zeros((1, LANES), _F32).at[0, :2 * nh].set(b_gate)
    bias_row = b_gate.reshape(2 * nh, 1)
    norm_cols = jnp.broadcast_to(head_norm.astype(_F32)[:, None], (nh * dv, L))
    rows = MLSTM_ROWS if b % MLSTM_ROWS == 0 else 1
    tok = lambda width: pl.BlockSpec((rows, L, width), lambda bi, ci: (bi, ci, 0))
    vt_spec = lambda r: pl.BlockSpec((nh * dv, L), lambda bi, ci: (0, (bi * rows + r) * chunks + ci))
    return pl.pallas_call(
        functools.partial(_mlstm_kernel, rows=rows),
        grid=(b // rows, chunks),
        in_specs=[tok(nh * dk), tok(nh * dk), tok(nh * dv), tok(LANES),
                  pl.BlockSpec((rows, 2 * nh, L), lambda bi, ci: (bi, 0, ci)),
                  _resident((1, LANES)), _resident((2 * nh, 1)), _resident((nh * dv, L))]
                 + [vt_spec(r) for r in range(rows)],
        out_specs=tok(nh * dv),
        out_shape=jax.ShapeDtypeStruct((b, s, nh * dv), _BF16),
        scratch_shapes=[pltpu.VMEM((rows * nh, dv + ONES_ROWS, 2 * dk), _F32),
                        pltpu.VMEM((rows * nh, LANES), _F32)],
        compiler_params=_params("parallel", "arbitrary"), name="mlstm_chunk",
    )(q, k, o, gates, gates_row, bias_col, bias_row, norm_cols, *([vt] * rows))


GELU_C = math.sqrt(2.0 / math.pi)
RUN_PITCH = 72


def _mix_ffn_kernel(a0_ref, x0_ref, a_ref, x_ref, wo_ref, gmix_ref, gpre_ref, wup_ref, cw_ref, cb_ref,
                    wdn_ref, gpost_ref, o_ref,
                    h_ref, xp_ref, y_ref, carry_ref, slab_in_ref, slab_out_ref, hid_ref, *, tiles_per_seq):
    tm, d = x_ref.shape
    f = wdn_ref.shape[0]
    run = tm // SUBLANES
    n_slabs = d // LANES
    step = pl.program_id(0)

    n_slices = SUBLANES
    rows_of = lambda i: slice(i * run, (i + 1) * run)

    def mix_rows(mixed, x_blk_ref, r, zero):
        x1 = x_blk_ref[rows_of(r), :] + _rms(mixed[rows_of(r)] + zero, gmix_ref[...])
        for c in range(n_slabs):
            slab_in_ref[c, r * RUN_PITCH:r * RUN_PITCH + run, :] = x1[:, c * LANES:(c + 1) * LANES]

    def gather_slice(j, slot, zero):
        xp = jnp.concatenate(
            [jnp.concatenate([slab_in_ref[c, pl.ds(k, SUBLANES, stride=RUN_PITCH), :]
                              for k in range(j * SUBLANES, (j + 1) * SUBLANES)], axis=0)
             for c in range(n_slabs)], axis=1) + zero
        xp_ref[slot, rows_of(j), :] = xp
        h_ref[slot, rows_of(j), :] = _rms(xp, gpre_ref[...]).astype(_BF16)

    def finish_slice(j, slot, zero):
        out = xp_ref[slot, rows_of(j), :] + _rms(y_ref[slot, rows_of(j), :] + zero, gpost_ref[...])
        for c in range(n_slabs):
            for kk in range(SUBLANES):
                slab_out_ref[c, pl.ds(j * SUBLANES + kk, SUBLANES, stride=RUN_PITCH), :] = (
                    out[kk * SUBLANES:(kk + 1) * SUBLANES, c * LANES:(c + 1) * LANES])

    def zero_after(value):
        bits = pltpu.bitcast(value[:SUBLANES, :LANES], jnp.uint32)
        bits = lax.shift_right_logical(lax.shift_right_logical(bits, jnp.uint32(16)), jnp.uint32(16))
        return jnp.concatenate([pltpu.bitcast(bits, _F32)[0:1, :]] * n_slabs, axis=1)

    no_wait = jnp.zeros((1, d), _F32)

    @pl.when(step == 0)
    def _():
        y_ref[...] = jnp.zeros_like(y_ref)
        xp_ref[...] = jnp.zeros_like(xp_ref)
        mixed0 = jnp.dot(a0_ref[...], wo_ref[...], preferred_element_type=_F32)
        for r in range(n_slices):
            mix_rows(mixed0, x0_ref, r, no_wait)
        for j in range(n_slices):
            gather_slice(j, 0, no_wait)

    @pl.when(step % tiles_per_seq == 0)
    def _():
        carry_ref[...] = jnp.zeros_like(carry_ref)

    cur = step % 2
    other = (step + 1) % 2

    mixed = jnp.dot(a_ref[...], wo_ref[...], preferred_element_type=_F32)

    h = h_ref[cur]
    first_sublane = lax.broadcasted_iota(jnp.int32, (SUBLANES, FFN_CHUNK), 0) == 0

    def conv(c0, width):
        cols = slice(c0, c0 + width)
        u = jnp.dot(h, wup_ref[:, cols], preferred_element_type=_F32)
        prev = carry_ref[:, cols]
        tail1, tail2 = u[tm - SUBLANES:], u[tm - 2 * SUBLANES:tm - SUBLANES]
        carry_ref[:, cols] = u[tm - 2 * SUBLANES:]
        mask = first_sublane[:, :width]
        wrap1 = jnp.where(mask, prev[2 * SUBLANES - 1:, :], pltpu.roll(tail1, 1, axis=0))
        wrap2 = jnp.where(mask, prev[SUBLANES - 1:SUBLANES, :], pltpu.roll(tail2, 1, axis=0))
        u1 = jnp.concatenate([wrap1, u[:tm - SUBLANES]], axis=0)
        u2 = jnp.concatenate([wrap2, wrap1, u[:tm - 2 * SUBLANES]], axis=0)
        return (cb_ref[:, cols] + cw_ref[2:3, cols] * u + cw_ref[1:2, cols] * u1
                + cw_ref[0:1, cols] * u2), zero_after(u)

    pieces = ([functools.partial(piece, j) for j in range(n_slices)
               for piece in (functools.partial(finish_slice, slot=other),
                             functools.partial(mix_rows, mixed, x_ref))]
              + [functools.partial(gather_slice, j, other) for j in range(n_slices)])
    chunks = list(range(0, f, FFN_CHUNK))
    last_tie = 2 * len(chunks) - 1
    for i, c0 in enumerate(chunks):
        width = min(FFN_CHUNK, f - c0)
        gate, after_gate = conv(c0, width)
        half_val, after_val = conv(f + c0, width)
        inner = gate * ((gate * gate) * (GELU_C * 0.044715) + GELU_C)
        w = gate * half_val
        hid_ref[:, c0:c0 + width] = (w * jnp.tanh(inner) + w).astype(_BF16)
        for tie, zero in ((2 * i, after_gate), (2 * i + 1, after_val)):
            for t, piece in enumerate(pieces):
                if min(t, last_tie) == tie:
                    piece(zero=zero)

    y_ref[cur] = jnp.dot(hid_ref[...], wdn_ref[...], preferred_element_type=_F32)
    for c in range(n_slabs):
        for r in range(SUBLANES):
            o_ref[r * run:(r + 1) * run, c * LANES:(c + 1) * LANES] = (
                slab_out_ref[c, r * RUN_PITCH:r * RUN_PITCH + run, :])


def _mix_ffn(a, x, w_out, g_mix, g_pre, w_up, conv_w, conv_b, w_down, g_post, seq_len):
    t, d = x.shape
    k = a.shape[1]
    f = w_down.shape[0]
    tm = min(ROW_TILE, seq_len)
    n = t // tm
    half = jnp.concatenate([jnp.ones((f,), _F32), jnp.full((f,), 0.5, _F32)])
    first = lambda width: pl.BlockSpec((tm, width), lambda i: (0, 0), pipeline_mode=pl.Buffered(1))
    ahead = lambda width: pl.BlockSpec((tm, width), lambda i: (jnp.minimum(i + 1, n - 1), 0))
    row = lambda width: _resident((1, width))
    return pl.pallas_call(
        functools.partial(_mix_ffn_kernel, tiles_per_seq=seq_len // tm),
        grid=(n + 1,),
        in_specs=[first(k), first(d), ahead(k), ahead(d), _resident((k, d)), row(d), row(d),
                  _resident((d, 2 * f)), _resident((CONV_WIDTH, 2 * f)), row(2 * f), _resident((f, d)),
                  row(d)],
        out_specs=pl.BlockSpec((tm, d), lambda i: (jnp.maximum(i - 1, 0), 0)),
        out_shape=jax.ShapeDtypeStruct((t, d), _F32),
        scratch_shapes=[pltpu.VMEM((2, tm, d), _BF16), pltpu.VMEM((2, tm, d), _F32),
                        pltpu.VMEM((2, tm, d), _F32), pltpu.VMEM((2 * SUBLANES, 2 * f), _F32),
                        pltpu.VMEM((d // LANES, SUBLANES * RUN_PITCH, LANES), _F32),
                        pltpu.VMEM((d // LANES, SUBLANES * RUN_PITCH, LANES), _F32),
                        pltpu.VMEM((tm, f), _BF16)],
        compiler_params=_params("arbitrary"), name="mix_ffn",
    )(a, x, a, x, w_out, g_mix.reshape(1, d), g_pre.reshape(1, d), w_up, conv_w * half,
      (conv_b * half).reshape(1, 2 * f), w_down, g_post.reshape(1, d))


def _decay_kernel(f_ref, bias_ref, place_ref, cq_ref, ck_ref, carry_ref):
    tm = f_ref.shape[1]

    @pl.when(pl.program_id(1) == 0)
    def _():
        carry_ref[...] = jnp.zeros_like(carry_ref)

    lane = lax.broadcasted_iota(jnp.int32, (tm, LANES), 1)
    row = lax.broadcasted_iota(jnp.int32, (tm, LANES), 0)
    c = jnp.where(lane < FOX_HEADS, _log_sigmoid(f_ref[0] + bias_ref[...]), 0.0)
    shift = 1
    while shift < tm:
        c = c + jnp.where(row >= shift, pltpu.roll(c, shift, axis=0), 0.0)
        shift *= 2
    c = carry_ref[0:1, :] + c
    carry_ref[...] = jnp.broadcast_to(c[tm - 1:tm, :], carry_ref.shape)
    c = c * LOG2E

    packed = jnp.where(lane < FOX_HEADS, 1.0, 0.0)
    packed = pltpu.roll(packed, DECAY_PARTS * FOX_HEADS, axis=1)
    rest = c
    for j in range(DECAY_PARTS):
        part = rest.astype(_BF16).astype(_F32)
        rest = rest - part
        packed = packed + (part if j == 0 else pltpu.roll(part, j * FOX_HEADS, axis=1))
    placed = jnp.dot(packed.astype(_BF16), place_ref[...], preferred_element_type=_F32)
    width = cq_ref.shape[2]
    cq_ref[0] = placed[:, :width].astype(cq_ref.dtype)
    ck_ref[0] = placed[:, width:].astype(ck_ref.dtype)


def _decay_placement():
    nh, dh, parts = FOX_HEADS, FOX_HEAD_DIM, DECAY_PARTS
    width = nh * dh
    heads = jnp.arange(nh)
    base = (heads // 2) * 2 * dh + (1 - heads % 2) * dh
    ones_row = parts * nh + heads
    place = jnp.zeros((LANES, 2 * width), _F32)
    for j in range(parts):
        place = place.at[j * nh + heads, base + j].set(1.0)
        place = place.at[ones_row, base + parts + j].set(1.0)
        place = place.at[ones_row, width + base + j].set(1.0)
        place = place.at[j * nh + heads, width + base + parts + j].set(-1.0)
    return place.astype(_BF16)


def _fox_decay(f_pre, bias):
    b, s, _ = f_pre.shape
    nh, dh = FOX_HEADS, FOX_HEAD_DIM
    tm = min(ROW_TILE, s)
    bias_row = jnp.zeros((1, LANES), _F32).at[0, :nh].set(bias)
    out = jax.ShapeDtypeStruct((b, s, nh * dh), _BF16)
    return pl.pallas_call(
        _decay_kernel,
        grid=(b, s // tm),
        in_specs=[pl.BlockSpec((1, tm, LANES), lambda bi, ti: (bi, ti, 0)), _resident((1, LANES)),
                  _resident((LANES, 2 * nh * dh))],
        out_specs=[pl.BlockSpec((1, tm, nh * dh), lambda bi, ti: (bi, ti, 0))] * 2,
        out_shape=[out, out],
        scratch_shapes=[pltpu.VMEM((SUBLANES, LANES), _F32)],
        compiler_params=_params("parallel", "arbitrary"), name="fox_decay",
    )(f_pre, bias_row, _decay_placement())


def _attn_kernel(q_ref, cq_ref, g_ref, k_ref, ck_ref, vt_ref, o_ref,
                 sa_ref, sb_ref, sa_max_ref, sb_max_ref, m_ref, acc_ref):
    tq = q_ref.shape[1]
    tk = sa_ref.shape[1]
    dh = FOX_HEAD_DIM
    qi = pl.program_id(2)
    first_q = lax.broadcasted_iota(jnp.int32, (tq, 2 * dh), 1) < dh
    first_k = lax.broadcasted_iota(jnp.int32, (tk, 2 * dh), 1) < dh
    q_pair, cq_pair = q_ref[0], cq_ref[0]
    q_aug = (jnp.where(first_q, q_pair, cq_pair), jnp.where(first_q, cq_pair, q_pair))

    ones_blk = jnp.ones((ONES_ROWS, tk), _BF16)

    m_ref[...] = jnp.full_like(m_ref, -jnp.inf)
    acc_ref[...] = jnp.zeros_like(acc_ref)

    def keys(j):
        start = pl.multiple_of(j * tk, tk)
        k_pair = k_ref[0, pl.ds(start, tk), :]
        ck_pair = ck_ref[0, pl.ds(start, tk), :]
        return jnp.where(first_k, k_pair, ck_pair), jnp.where(first_k, ck_pair, k_pair)

    def scores(k_aug, bufs, c, hh, diagonal):
        s_ref, smax_ref = bufs
        cols = slice(c * tk, (c + 1) * tk)
        s = lax.dot_general(k_aug[hh], q_aug[hh][cols], (((1,), (1,)), ((), ())),
                            preferred_element_type=_F32)
        if diagonal:
            key = lax.broadcasted_iota(jnp.int32, (tk, tk), 0)
            qry = lax.broadcasted_iota(jnp.int32, (tk, tk), 1)
            s = jnp.where(key <= qry, s, -jnp.inf)
        s_ref[hh, :, cols] = s
        smax_ref[hh, :, cols] = jnp.max(s, axis=0, keepdims=True)

    def accumulate(j, bufs, c, hh):
        s_ref, smax_ref = bufs
        start = pl.multiple_of(j * tk, tk)
        cols = slice(c * tk, (c + 1) * tk)
        m_old = m_ref[hh, :, cols]
        m_new = jnp.maximum(m_old, smax_ref[hh, :, cols])
        alpha = jnp.exp2(m_old - m_new)
        p = jnp.exp2(s_ref[hh, :, cols] - m_new).astype(_BF16)
        vt = jnp.concatenate([vt_ref[hh * dh:(hh + 1) * dh, pl.ds(start, tk)], ones_blk], axis=0)
        acc_ref[hh, :, cols] = alpha * acc_ref[hh, :, cols] + jnp.dot(vt, p, preferred_element_type=_F32)
        m_ref[hh, :, cols] = m_new

    per_tile = tq // tk
    buf_a, buf_b = (sa_ref, sa_max_ref), (sb_ref, sb_max_ref)
    pieces = [(hh, c) for hh in range(2) for c in range(per_tile)]

    first_keys = keys(0)
    for hh, c in pieces:
        scores(first_keys, buf_a, c, hh, False)

    def body(i, carry):
        for u in range(per_tile):
            j = i * per_tile + u
            cur, nxt = (buf_a, buf_b) if u % 2 == 0 else (buf_b, buf_a)
            next_keys = keys(j + 1)
            for hh, c in pieces:
                scores(next_keys, nxt, c, hh, False)
                accumulate(j, cur, c, hh)
        return carry

    lax.fori_loop(0, qi, body, 0)
    for hh in range(2):
        key = lax.broadcasted_iota(jnp.int32, (tk, tk), 0)
        qry = lax.broadcasted_iota(jnp.int32, (tk, tk), 1)
        square = jnp.where(key <= qry, sa_ref[hh, :, :tk], -jnp.inf)
        sa_ref[hh, :, :tk] = square
        sa_max_ref[hh, :, :tk] = jnp.max(square, axis=0, keepdims=True)
    for u in range(per_tile):
        j = qi * per_tile + u
        cur, nxt = (buf_a, buf_b) if u % 2 == 0 else (buf_b, buf_a)
        next_keys = keys(j + 1) if u + 1 < per_tile else None
        for hh in range(2):
            accumulate(j, cur, u, hh)
            for c in range(u + 1, per_tile):
                scores(next_keys, nxt, c, hh, c == u + 1)
                accumulate(j, cur, c, hh)

    halves = []
    for hh in range(2):
        acc = acc_ref[hh]
        halves.append(acc[0:dh, :] / acc[dh:dh + 1, :])
    out = jnp.concatenate(halves, axis=0).T
    o_ref[0] = (out * jax.nn.sigmoid(g_ref[0].astype(_F32))).astype(o_ref.dtype)


def _fox_attention(q, cq, g, k, ck, vt):
    b, s, hd = q.shape
    pairs = hd // LANES
    tq = min(ATTN_TILE, s)
    tk = min(ATTN_KEYS, tq // 2)
    rows = FOX_HEAD_DIM + ONES_ROWS
    qspec = pl.BlockSpec((1, tq, LANES), lambda bi, pi, qi: (bi, qi, pi))
    kspec = pl.BlockSpec((1, s, LANES), lambda bi, pi, qi: (bi, 0, pi))
    vspec = pl.BlockSpec((LANES, s), lambda bi, pi, qi: (pi, bi))
    return pl.pallas_call(
        _attn_kernel,
        grid=(b, pairs, s // tq),
        in_specs=[qspec, qspec, qspec, kspec, kspec, vspec],
        out_specs=qspec,
        out_shape=jax.ShapeDtypeStruct((b, s, hd), _BF16),
        scratch_shapes=[pltpu.VMEM((2, tk, tq), _F32), pltpu.VMEM((2, tk, tq), _F32),
                        pltpu.VMEM((2, 1, tq), _F32), pltpu.VMEM((2, 1, tq), _F32),
                        pltpu.VMEM((2, 1, tq), _F32), pltpu.VMEM((2, rows, tq), _F32)],
        compiler_params=_params("parallel", "parallel", "arbitrary"), name="fox_attention",
    )(q, cq, g, k, ck, vt)


def _pad_cols(w, width):
    return jnp.pad(w, ((0, 0), (0, width - w.shape[1])))


def kernel(x, norm_mix_pre, norm_mix_post, norm_ffn_pre, norm_ffn_post, mlstm_w_in, mlstm_b_gate,
           mlstm_norm, mlstm_w_out, kv_norm, kv_w, kv_b_f, fox_w_qg, fox_w_out, ffn_w_up,
           ffn_conv_w, ffn_conv_b, ffn_w_down):
    b, s, d = x.shape
    depth = norm_mix_pre.shape[0]
    n_a = mlstm_w_in.shape[0]
    xt = x.reshape(b * s, d)
    hk = MLSTM_HEADS * MLSTM_QK_DIM
    hv = MLSTM_HEADS * MLSTM_V_DIM
    k_sh = vt_sh = cq_sh = ck_sh = None
    bf = lambda w: w.astype(_BF16)
    m_q = bf(mlstm_w_in[:, :, :hk] * MLSTM_QK_DIM ** -0.5)
    m_k = bf(mlstm_w_in[:, :, hk:2 * hk])
    m_vt = bf(jnp.swapaxes(mlstm_w_in[:, :, 2 * hk:2 * hk + hv], 1, 2))
    m_o = bf(mlstm_w_in[:, :, 2 * hk + hv:2 * hk + 2 * hv])
    m_gates = bf(jnp.pad(mlstm_w_in[:, :, 2 * hk + 2 * hv:], ((0, 0), (0, 0), (0, LANES - 2 * MLSTM_HEADS))))
    f_q = bf(fox_w_qg[:, :, :d] * (FOX_HEAD_DIM ** -0.5 * LOG2E))
    f_g = bf(fox_w_qg[:, :, d:])
    out_w = [bf(mlstm_w_out), bf(fox_w_out)]
    up_w, down_w = bf(ffn_w_up), bf(ffn_w_down)
    seq = lambda a: a.reshape(b, s, a.shape[-1])
    for layer in range(depth):
        if layer < n_a:
            weights = [m_q[layer], m_k[layer], m_vt[layer], m_o[layer], m_gates[layer]]
            q, k, vt, o, gates = _norm_proj(xt, norm_mix_pre[layer], weights,
                                            [_BF16, _BF16, _BF16, _BF16, _F32], "mlstm_in_proj",
                                            transposed=(False, False, True, False, False))
            hs = _mlstm(seq(q), seq(k), vt, seq(o), seq(gates), mlstm_b_gate[layer], mlstm_norm[layer])
            mixed, w_out = hs.reshape(b * s, hv), out_w[0][layer]
        else:
            j = layer - n_a
            q, g = _norm_proj(xt, norm_mix_pre[layer], [f_q[j], f_g[j]], [_BF16, _BF16], "fox_in_proj")
            att = _fox_attention(seq(q), cq_sh, seq(g), k_sh, ck_sh, vt_sh)
            mixed, w_out = att.reshape(b * s, d), out_w[1][j]
        xt = _mix_ffn(mixed, xt, w_out, norm_mix_post[layer], norm_ffn_pre[layer], up_w[layer],
                      ffn_conv_w[layer], ffn_conv_b[layer], down_w[layer], norm_ffn_post[layer], s)
        if layer == n_a - 1:
            weights = [bf(kv_w[:, :d]), bf(kv_w[:, d:2 * d].T), bf(_pad_cols(kv_w[:, 2 * d:], LANES))]
            k2, vt_sh, f_pre = _norm_proj(xt, kv_norm, weights, [_BF16, _BF16, _F32], "kv_proj",
                                          transposed=(False, True, False))
            k_sh = k2.reshape(b, s, d)
            cq_sh, ck_sh = _fox_decay(f_pre.reshape(b, s, LANES), kv_b_f)
    return xt.reshape(b, s, d)
```

```python
import functools
import math

import jax
import jax.numpy as jnp
from jax import lax
from jax.experimental import pallas as pl
from jax.experimental.pallas import tpu as pltpu

EPS = 1e-6
MLSTM_HEADS = 8
MLSTM_QK_DIM = 64
MLSTM_V_DIM = 128
FOX_HEADS = 16
FOX_HEAD_DIM = 64
CONV_WIDTH = 3

LANES = 128
SUBLANES = 8
VMEM_LIMIT_BYTES = 56 * 1024 * 1024
ROW_TILE = 512
MLSTM_CHUNK = 128
MLSTM_ROWS = 2
ATTN_TILE = 2048
ATTN_KEYS = 256
ATTN_RING = 16
FFN_CHUNK = 256
DECAY_PARTS = 3
LOG2E = math.log2(math.e)

_BF16 = jnp.bfloat16
_F32 = jnp.float32


def _params(*semantics):
    return pltpu.CompilerParams(dimension_semantics=semantics, vmem_limit_bytes=VMEM_LIMIT_BYTES)


def _resident(shape):
    return pl.BlockSpec(shape, lambda *_: (0,) * len(shape), pipeline_mode=pl.Buffered(1))


def _rms(x, gain):
    return x * lax.rsqrt(jnp.mean(x * x, axis=-1, keepdims=True) + EPS) * gain


def _norm_proj_kernel(x_ref, g_ref, *refs, transposed):
    n_out = len(transposed)
    w_refs, o_refs = refs[:n_out], refs[n_out:]
    h = _rms(x_ref[...], g_ref[...]).astype(_BF16)
    for w_ref, o_ref, tr in zip(w_refs, o_refs, transposed):
        n = w_ref.shape[0] if tr else w_ref.shape[1]
        for c0 in range(0, n, 512):
            c1 = min(c0 + 512, n)
            if tr:
                y = lax.dot_general(w_ref[c0:c1, :], h, (((1,), (1,)), ((), ())),
                                    preferred_element_type=_F32)
                o_ref[c0:c1, :] = y.astype(o_ref.dtype)
            else:
                y = jnp.dot(h, w_ref[:, c0:c1], preferred_element_type=_F32)
                o_ref[:, c0:c1] = y.astype(o_ref.dtype)


def _norm_proj(x, gain, weights, out_dtypes, name, transposed=None):
    t, d = x.shape
    tm = min(ROW_TILE, t)
    transposed = tuple(transposed or (False,) * len(weights))
    in_specs = [pl.BlockSpec((tm, d), lambda i: (i, 0)), _resident((1, d))]
    in_specs += [_resident(w.shape) for w in weights]
    out_specs, out_shape = [], []
    for w, dt, tr in zip(weights, out_dtypes, transposed):
        if tr:
            out_specs.append(pl.BlockSpec((w.shape[0], tm), lambda i: (0, i)))
            out_shape.append(jax.ShapeDtypeStruct((w.shape[0], t), dt))
        else:
            out_specs.append(pl.BlockSpec((tm, w.shape[1]), lambda i: (i, 0)))
            out_shape.append(jax.ShapeDtypeStruct((t, w.shape[1]), dt))
    return pl.pallas_call(
        functools.partial(_norm_proj_kernel, transposed=transposed),
        grid=(t // tm,), in_specs=in_specs, out_specs=out_specs, out_shape=out_shape,
        compiler_params=_params("parallel"), name=name,
    )(x, gain.reshape(1, d), *weights)


def _log_sigmoid(z):
    return jnp.minimum(z, 0.0) - jnp.log1p(jnp.exp(-jnp.abs(z)))


ONES_ROWS = 16


def _mlstm_kernel(q_ref, k_ref, o_ref, gc_ref, gr_ref, bc_ref, br_ref, hn_ref, *rest, rows):
    nh, dk, dv = MLSTM_HEADS, MLSTM_QK_DIM, MLSTM_V_DIM
    L = q_ref.shape[1]
    vt_refs, (out_ref, ct_ref, m_ref) = rest[:rows], rest[rows:]

    @pl.when(pl.program_id(1) == 0)
    def _():
        ct_ref[...] = jnp.zeros_like(ct_ref)
        m_ref[...] = jnp.zeros_like(m_ref)

    row = lax.broadcasted_iota(jnp.int32, (L, L), 0)
    col = lax.broadcasted_iota(jnp.int32, (L, L), 1)
    source_before_target = row <= col
    tril = (row >= col).astype(_F32)
    triu = source_before_target.astype(_F32)

    heads = [(bb, h) for bb in range(rows) for h in range(nh)]
    lane_qk = lax.broadcasted_iota(jnp.int32, (L, 2 * dk), 1)
    ones_blk = jnp.ones((ONES_ROWS, L), _BF16)
    first_half = lane_qk < dk

    gates = {}
    for bb in range(rows):
        gc = gc_ref[bb] + bc_ref[...]
        gr = gr_ref[bb] + br_ref[:, 0:1]
        lane_g = lax.broadcasted_iota(jnp.int32, gc.shape, 1)
        sub_g = lax.broadcasted_iota(jnp.int32, gr.shape, 0)
        lf_c = jnp.where(lane_g >= nh, _log_sigmoid(gc), 0.0)
        lf_r = jnp.where(sub_g >= nh, _log_sigmoid(gr), 0.0)
        b_c_all = jnp.dot(tril, lf_c, preferred_element_type=_F32, precision=lax.Precision.HIGHEST)
        b_r_all = jnp.dot(lf_r, triu, preferred_element_type=_F32, precision=lax.Precision.HIGHEST)
        gates[bb] = (gc, gr, b_c_all, b_r_all)

    st = {}
    for bb, h in heads:
        pair = slice((h // 2) * 2 * dk, (h // 2 + 1) * 2 * dk)
        qp, kp = q_ref[bb, :, pair], k_ref[bb, :, pair]
        qm = jnp.where(first_half if h % 2 == 0 else ~first_half, qp, jnp.zeros_like(qp))
        ct = ct_ref[bb * nh + h]
        both = lax.dot_general(jnp.concatenate([kp, ct.astype(_BF16)], axis=0), qm,
                               (((1,), (1,)), ((), ())), preferred_element_type=_F32)
        st[bb, h] = dict(kp=kp, ct=ct, s_qk=both[:L], inter=both[L:])

    for bb, h in heads:
        gc, gr, b_c_all, b_r_all = gates[bb]
        e = st[bb, h]
        g_col = gc[:, h:h + 1] - b_c_all[:, nh + h:nh + h + 1]
        b_row = b_r_all[nh + h:nh + h + 1, :]
        i_row = gr[h:h + 1, :]
        m_prev = m_ref[bb * nh + h:bb * nh + h + 1, 0:1]
        dmat = jnp.where(source_before_target, g_col + b_row, -jnp.inf)
        m_inter = b_row + m_prev
        m_t = jnp.maximum(m_inter, jnp.max(dmat, axis=0, keepdims=True))
        sd = (e["s_qk"] * jnp.exp(dmat - m_t)).astype(_BF16)
        vt_aug = jnp.concatenate([vt_refs[bb][h * dv:(h + 1) * dv, :], ones_blk], axis=0)
        tot = jnp.dot(vt_aug, sd, preferred_element_type=_F32) + jnp.exp(m_inter - m_t) * e["inter"]
        num, den = tot[:dv], tot[dv:dv + 1]
        hv = num / jnp.maximum(jnp.abs(den), jnp.exp(-m_t))
        hv = hv * lax.rsqrt(jnp.mean(hv * hv, axis=0, keepdims=True) + EPS)
        hv = (hv * hn_ref[h * dv:(h + 1) * dv, :]).T
        hv = hv * jax.nn.sigmoid(o_ref[bb, :, h * dv:(h + 1) * dv].astype(_F32))
        out_ref[bb, :, h * dv:(h + 1) * dv] = hv.astype(out_ref.dtype)
        e.update(vt_aug=vt_aug, b_row=b_row, i_row=i_row, m_prev=m_prev)

    for bb, h in heads:
        e = st[bb, h]
        b_row, m_prev = e["b_row"], e["m_prev"]
        b_last = b_row[:, L - 1:L]
        g_row = b_last - b_row + e["i_row"]
        m_new = jnp.maximum(b_last + m_prev, jnp.max(g_row, axis=-1, keepdims=True))
        decay = jnp.exp(b_last + m_prev - m_new)
        vw = (e["vt_aug"].astype(_F32) * jnp.exp(g_row - m_new)).astype(_BF16)
        ct_ref[bb * nh + h] = decay * e["ct"] + jnp.dot(vw, e["kp"], preferred_element_type=_F32)
        m_ref[bb * nh + h:bb * nh + h + 1, :] = jnp.broadcast_to(m_new, (1, m_ref.shape[1]))


def _mlstm(q, k, vt, o, gates, b_gate, head_norm):
    b, s, _ = q.shape
    nh, dk, dv = MLSTM_HEADS, MLSTM_QK_DIM, MLSTM_V_DIM
    L = min(MLSTM_CHUNK, s)
    chunks = s // L
    gates_row = jnp.swapaxes(gates[:, :, :2 * nh], 1, 2)
    bias_col = jnp.zeros((1, LANES), _F32).at[0, :2 * nh].set(b_gate)
    bias_row = b_gate.reshape(2 * nh, 1)
    norm_cols = jnp.broadcast_to(head_norm.astype(_F32)[:, None], (nh * dv, L))
    rows = MLSTM_ROWS if b % MLSTM_ROWS == 0 else 1
    tok = lambda width: pl.BlockSpec((rows, L, width), lambda bi, ci: (bi, ci, 0))
    vt_spec = lambda r: pl.BlockSpec((nh * dv, L), lambda bi, ci: (0, (bi * rows + r) * chunks + ci))
    return pl.pallas_call(
        functools.partial(_mlstm_kernel, rows=rows),
        grid=(b // rows, chunks),
        in_specs=[tok(nh * dk), tok(nh * dk), tok(nh * dv), tok(LANES),
                  pl.BlockSpec((rows, 2 * nh, L), lambda bi, ci: (bi, 0, ci)),
                  _resident((1, LANES)), _resident((2 * nh, 1)), _resident((nh * dv, L))]
                 + [vt_spec(r) for r in range(rows)],
        out_specs=tok(nh * dv),
        out_shape=jax.ShapeDtypeStruct((b, s, nh * dv), _BF16),
        scratch_shapes=[pltpu.VMEM((rows * nh, dv + ONES_ROWS, 2 * dk), _F32),
                        pltpu.VMEM((rows * nh, LANES), _F32)],
        compiler_params=_params("parallel", "arbitrary"), name="mlstm_chunk",
    )(q, k, o, gates, gates_row, bias_col, bias_row, norm_cols, *([vt] * rows))


GELU_C = math.sqrt(2.0 / math.pi)
RUN_PITCH = 72


def _mix_ffn_kernel(a0_ref, x0_ref, a_ref, x_ref, wo_ref, gmix_ref, gpre_ref, wup_ref, cw_ref, cb_ref,
                    wdn_ref, gpost_ref, o_ref,
                    h_ref, xp_ref, y_ref, carry_ref, slab_in_ref, slab_out_ref, hid_ref, *, tiles_per_seq):
    tm, d = x_ref.shape
    f = wdn_ref.shape[0]
    run = tm // SUBLANES
    n_slabs = d // LANES
    step = pl.program_id(0)

    n_slices = SUBLANES
    rows_of = lambda i: slice(i * run, (i + 1) * run)

    def mix_rows(mixed, x_blk_ref, r, zero):
        x1 = x_blk_ref[rows_of(r), :] + _rms(mixed[rows_of(r)] + zero, gmix_ref[...])
        for c in range(n_slabs):
            slab_in_ref[c, r * RUN_PITCH:r * RUN_PITCH + run, :] = x1[:, c * LANES:(c + 1) * LANES]

    def gather_slice(j, slot, zero):
        xp = jnp.concatenate(
            [jnp.concatenate([slab_in_ref[c, pl.ds(k, SUBLANES, stride=RUN_PITCH), :]
                              for k in range(j * SUBLANES, (j + 1) * SUBLANES)], axis=0)
             for c in range(n_slabs)], axis=1) + zero
        xp_ref[slot, rows_of(j), :] = xp
        h_ref[slot, rows_of(j), :] = _rms(xp, gpre_ref[...]).astype(_BF16)

    def finish_slice(j, slot, zero):
        out = xp_ref[slot, rows_of(j), :] + _rms(y_ref[slot, rows_of(j), :] + zero, gpost_ref[...])
        for c in range(n_slabs):
            for kk in range(SUBLANES):
                slab_out_ref[c, pl.ds(j * SUBLANES + kk, SUBLANES, stride=RUN_PITCH), :] = (
                    out[kk * SUBLANES:(kk + 1) * SUBLANES, c * LANES:(c + 1) * LANES])

    def zero_after(value):
        bits = pltpu.bitcast(value[:SUBLANES, :LANES], jnp.uint32)
        bits = lax.shift_right_logical(lax.shift_right_logical(bits, jnp.uint32(16)), jnp.uint32(16))
        return jnp.concatenate([pltpu.bitcast(bits, _F32)[0:1, :]] * n_slabs, axis=1)

    no_wait = jnp.zeros((1, d), _F32)

    @pl.when(step == 0)
    def _():
        y_ref[...] = jnp.zeros_like(y_ref)
        xp_ref[...] = jnp.zeros_like(xp_ref)
        mixed0 = jnp.dot(a0_ref[...], wo_ref[...], preferred_element_type=_F32)
        for r in range(n_slices):
            mix_rows(mixed0, x0_ref, r, no_wait)
        for j in range(n_slices):
            gather_slice(j, 0, no_wait)

    @pl.when(step % tiles_per_seq == 0)
    def _():
        carry_ref[...] = jnp.zeros_like(carry_ref)

    cur = step % 2
    other = (step + 1) % 2

    mixed = jnp.dot(a_ref[...], wo_ref[...], preferred_element_type=_F32)

    h = h_ref[cur]
    first_sublane = lax.broadcasted_iota(jnp.int32, (SUBLANES, FFN_CHUNK), 0) == 0

    def conv(c0, width):
        cols = slice(c0, c0 + width)
        u = jnp.dot(h, wup_ref[:, cols], preferred_element_type=_F32)
        prev = carry_ref[:, cols]
        tail1, tail2 = u[tm - SUBLANES:], u[tm - 2 * SUBLANES:tm - SUBLANES]
        carry_ref[:, cols] = u[tm - 2 * SUBLANES:]
        mask = first_sublane[:, :width]
        wrap1 = jnp.where(mask, prev[2 * SUBLANES - 1:, :], pltpu.roll(tail1, 1, axis=0))
        wrap2 = jnp.where(mask, prev[SUBLANES - 1:SUBLANES, :], pltpu.roll(tail2, 1, axis=0))
        u1 = jnp.concatenate([wrap1, u[:tm - SUBLANES]], axis=0)
        u2 = jnp.concatenate([wrap2, wrap1, u[:tm - 2 * SUBLANES]], axis=0)
        return (cb_ref[:, cols] + cw_ref[2:3, cols] * u + cw_ref[1:2, cols] * u1
                + cw_ref[0:1, cols] * u2), zero_after(u)

    pieces = ([functools.partial(piece, j) for j in range(n_slices)
               for piece in (functools.partial(finish_slice, slot=other),
                             functools.partial(mix_rows, mixed, x_ref))]
              + [functools.partial(gather_slice, j, other) for j in range(n_slices)])
    chunks = list(range(0, f, FFN_CHUNK))
    last_tie = 2 * len(chunks) - 1
    for i, c0 in enumerate(chunks):
        width = min(FFN_CHUNK, f - c0)
        gate, after_gate = conv(c0, width)
        half_val, after_val = conv(f + c0, width)
        inner = gate * ((gate * gate) * (GELU_C * 0.044715) + GELU_C)
        w = gate * half_val
        hid_ref[:, c0:c0 + width] = (w * jnp.tanh(inner) + w).astype(_BF16)
        for tie, zero in ((2 * i, after_gate), (2 * i + 1, after_val)):
            for t, piece in enumerate(pieces):
                if min(t, last_tie) == tie:
                    piece(zero=zero)

    y_ref[cur] = jnp.dot(hid_ref[...], wdn_ref[...], preferred_element_type=_F32)
    for c in range(n_slabs):
        for r in range(SUBLANES):
            o_ref[r * run:(r + 1) * run, c * LANES:(c + 1) * LANES] = (
                slab_out_ref[c, r * RUN_PITCH:r * RUN_PITCH + run, :])


def _mix_ffn(a, x, w_out, g_mix, g_pre, w_up, conv_w, conv_b, w_down, g_post, seq_len):
    t, d = x.shape
    k = a.shape[1]
    f = w_down.shape[0]
    tm = min(ROW_TILE, seq_len)
    n = t // tm
    half = jnp.concatenate([jnp.ones((f,), _F32), jnp.full((f,), 0.5, _F32)])
    first = lambda width: pl.BlockSpec((tm, width), lambda i: (0, 0), pipeline_mode=pl.Buffered(1))
    ahead = lambda width: pl.BlockSpec((tm, width), lambda i: (jnp.minimum(i + 1, n - 1), 0))
    row = lambda width: _resident((1, width))
    return pl.pallas_call(
        functools.partial(_mix_ffn_kernel, tiles_per_seq=seq_len // tm),
        grid=(n + 1,),
        in_specs=[first(k), first(d), ahead(k), ahead(d), _resident((k, d)), row(d), row(d),
                  _resident((d, 2 * f)), _resident((CONV_WIDTH, 2 * f)), row(2 * f), _resident((f, d)),
                  row(d)],
        out_specs=pl.BlockSpec((tm, d), lambda i: (jnp.maximum(i - 1, 0), 0)),
        out_shape=jax.ShapeDtypeStruct((t, d), _F32),
        scratch_shapes=[pltpu.VMEM((2, tm, d), _BF16), pltpu.VMEM((2, tm, d), _F32),
                        pltpu.VMEM((2, tm, d), _F32), pltpu.VMEM((2 * SUBLANES, 2 * f), _F32),
                        pltpu.VMEM((d // LANES, SUBLANES * RUN_PITCH, LANES), _F32),
                        pltpu.VMEM((d // LANES, SUBLANES * RUN_PITCH, LANES), _F32),
                        pltpu.VMEM((tm, f), _BF16)],
        compiler_params=_params("arbitrary"), name="mix_ffn",
    )(a, x, a, x, w_out, g_mix.reshape(1, d), g_pre.reshape(1, d), w_up, conv_w * half,
      (conv_b * half).reshape(1, 2 * f), w_down, g_post.reshape(1, d))


def _decay_kernel(f_ref, bias_ref, place_ref, cq_ref, ck_ref, carry_ref):
    tm = f_ref.shape[1]

    @pl.when(pl.program_id(1) == 0)
    def _():
        carry_ref[...] = jnp.zeros_like(carry_ref)

    lane = lax.broadcasted_iota(jnp.int32, (tm, LANES), 1)
    row = lax.broadcasted_iota(jnp.int32, (tm, LANES), 0)
    c = jnp.where(lane < FOX_HEADS, _log_sigmoid(f_ref[0] + bias_ref[...]), 0.0)
    shift = 1
    while shift < tm:
        c = c + jnp.where(row >= shift, pltpu.roll(c, shift, axis=0), 0.0)
        shift *= 2
    c = carry_ref[0:1, :] + c
    carry_ref[...] = jnp.broadcast_to(c[tm - 1:tm, :], carry_ref.shape)
    c = c * LOG2E

    packed = jnp.where(lane < FOX_HEADS, 1.0, 0.0)
    packed = pltpu.roll(packed, DECAY_PARTS * FOX_HEADS, axis=1)
    rest = c
    for j in range(DECAY_PARTS):
        part = rest.astype(_BF16).astype(_F32)
        rest = rest - part
        packed = packed + (part if j == 0 else pltpu.roll(part, j * FOX_HEADS, axis=1))
    placed = jnp.dot(packed.astype(_BF16), place_ref[...], preferred_element_type=_F32)
    width = cq_ref.shape[2]
    cq_ref[0] = placed[:, :width].astype(cq_ref.dtype)
    ck_ref[0] = placed[:, width:].astype(ck_ref.dtype)


def _decay_placement():
    nh, dh, parts = FOX_HEADS, FOX_HEAD_DIM, DECAY_PARTS
    width = nh * dh
    heads = jnp.arange(nh)
    base = (heads // 2) * 2 * dh + (1 - heads % 2) * dh
    ones_row = parts * nh + heads
    place = jnp.zeros((LANES, 2 * width), _F32)
    for j in range(parts):
        place = place.at[j * nh + heads, base + j].set(1.0)
        place = place.at[ones_row, base + parts + j].set(1.0)
        place = place.at[ones_row, width + base + j].set(1.0)
        place = place.at[j * nh + heads, width + base + parts + j].set(-1.0)
    return place.astype(_BF16)


def _fox_decay(f_pre, bias):
    b, s, _ = f_pre.shape
    nh, dh = FOX_HEADS, FOX_HEAD_DIM
    tm = min(ROW_TILE, s)
    bias_row = jnp.zeros((1, LANES), _F32).at[0, :nh].set(bias)
    out = jax.ShapeDtypeStruct((b, s, nh * dh), _BF16)
    return pl.pallas_call(
        _decay_kernel,
        grid=(b, s // tm),
        in_specs=[pl.BlockSpec((1, tm, LANES), lambda bi, ti: (bi, ti, 0)), _resident((1, LANES)),
                  _resident((LANES, 2 * nh * dh))],
        out_specs=[pl.BlockSpec((1, tm, nh * dh), lambda bi, ti: (bi, ti, 0))] * 2,
        out_shape=[out, out],
        scratch_shapes=[pltpu.VMEM((SUBLANES, LANES), _F32)],
        compiler_params=_params("parallel", "arbitrary"), name="fox_decay",
    )(f_pre, bias_row, _decay_placement())


def _attn_kernel(q_ref, cq_ref, g_ref, k_ref, ck_ref, vt_ref, o_ref, s_ref, smax_ref, m_ref, acc_ref):
    tq = q_ref.shape[1]
    tk = s_ref.shape[1]
    dh = FOX_HEAD_DIM
    qi = pl.program_id(2)
    first_q = lax.broadcasted_iota(jnp.int32, (tq, 2 * dh), 1) < dh
    first_k = lax.broadcasted_iota(jnp.int32, (tk, 2 * dh), 1) < dh
    q_pair, cq_pair = q_ref[0], cq_ref[0]
    q_aug = (jnp.where(first_q, q_pair, cq_pair), jnp.where(first_q, cq_pair, q_pair))

    ones_blk = jnp.ones((ONES_ROWS, tk), _BF16)

    m_ref[...] = jnp.full_like(m_ref, -jnp.inf)
    acc_ref[...] = jnp.zeros_like(acc_ref)

    def keys(j):
        start = pl.multiple_of(j * tk, tk)
        k_pair = k_ref[0, pl.ds(start, tk), :]
        ck_pair = ck_ref[0, pl.ds(start, tk), :]
        return jnp.where(first_k, k_pair, ck_pair), jnp.where(first_k, ck_pair, k_pair)

    causal = (lax.broadcasted_iota(jnp.int32, (tk, tk), 0)
              <= lax.broadcasted_iota(jnp.int32, (tk, tk), 1))

    def scores(k_aug, bufs, c, hh, diagonal):
        s_ref, smax_ref = bufs
        s = lax.dot_general(k_aug[hh], q_aug[hh][c * tk:(c + 1) * tk], (((1,), (1,)), ((), ())),
                            preferred_element_type=_F32)
        if diagonal:
            s = jnp.where(causal, s, -jnp.inf)
        s_ref[...] = s
        smax_ref[...] = jnp.max(s, axis=0, keepdims=True)

    def accumulate(j, bufs, c, hh):
        s_ref, smax_ref = bufs
        start = pl.multiple_of(j * tk, tk)
        cols = slice(c * tk, (c + 1) * tk)
        m_old = m_ref[hh, :, cols]
        m_new = jnp.maximum(m_old, smax_ref[...])
        alpha = jnp.exp2(m_old - m_new)
        p = jnp.exp2(s_ref[...] - m_new).astype(_BF16)
        vt = jnp.concatenate([vt_ref[hh * dh:(hh + 1) * dh, pl.ds(start, tk)], ones_blk], axis=0)
        acc_ref[hh, :, cols] = alpha * acc_ref[hh, :, cols] + jnp.dot(vt, p, preferred_element_type=_F32)
        m_ref[hh, :, cols] = m_new

    per_tile = tq // tk
    ring = s_ref.shape[0]
    ahead = ring // 2
    full = [(u, hh, c) for u in range(per_tile) for hh in range(2) for c in range(per_tile)]
    tail = [(u, hh, c) for u in range(per_tile) for hh in range(2) for c in range(u, per_tile)]
    assert len(full) % ring == 0 and ahead <= 2 * per_tile

    def run(order, base, wrap, mask_diagonal):
        k_block, k_aug = None, None
        for n, (u, hh, c) in enumerate(order):
            m = n + ahead
            if m < len(order) or wrap is not None:
                nu, nhh, nc = order[m] if m < len(order) else wrap[m - len(order)]
                block = base + nu + (per_tile if m >= len(order) else 0)
                if k_block is None or (nu, m >= len(order)) != k_block:
                    k_block, k_aug = (nu, m >= len(order)), keys(block)
                scores(k_aug, (s_ref.at[m % ring], smax_ref.at[m % ring]), nc, nhh,
                       mask_diagonal and m < len(order) and nc == nu)
            accumulate(base + u, (s_ref.at[n % ring], smax_ref.at[n % ring]), c, hh)

    first_keys = keys(0)
    for n in range(ahead):
        scores(first_keys, (s_ref.at[n], smax_ref.at[n]), full[n][2], full[n][1], False)

    def body(i, carry):
        run(full, i * per_tile, full, False)
        return carry

    lax.fori_loop(0, qi, body, 0)
    first = jnp.where(causal, s_ref[0], -jnp.inf)
    s_ref[0] = first
    smax_ref[0] = jnp.max(first, axis=0, keepdims=True)
    run(tail, qi * per_tile, None, True)

    halves = []
    for hh in range(2):
        acc = acc_ref[hh]
        halves.append(acc[0:dh, :] / acc[dh:dh + 1, :])
    out = jnp.concatenate(halves, axis=0).T
    o_ref[0] = (out * jax.nn.sigmoid(g_ref[0].astype(_F32))).astype(o_ref.dtype)


def _fox_attention(q, cq, g, k, ck, vt):
    b, s, hd = q.shape
    pairs = hd // LANES
    tq = min(ATTN_TILE, s)
    tk = min(ATTN_KEYS, tq // 2)
    rows = FOX_HEAD_DIM + ONES_ROWS
    qspec = pl.BlockSpec((1, tq, LANES), lambda bi, pi, qi: (bi, qi, pi))
    kspec = pl.BlockSpec((1, s, LANES), lambda bi, pi, qi: (bi, 0, pi))
    vspec = pl.BlockSpec((LANES, s), lambda bi, pi, qi: (pi, bi))
    return pl.pallas_call(
        _attn_kernel,
        grid=(b, pairs, s // tq),
        in_specs=[qspec, qspec, qspec, kspec, kspec, vspec],
        out_specs=qspec,
        out_shape=jax.ShapeDtypeStruct((b, s, hd), _BF16),
        scratch_shapes=[pltpu.VMEM((ATTN_RING, tk, tk), _F32), pltpu.VMEM((ATTN_RING, 1, tk), _F32),
                        pltpu.VMEM((2, 1, tq), _F32), pltpu.VMEM((2, rows, tq), _F32)],
        compiler_params=_params("parallel", "parallel", "arbitrary"), name="fox_attention",
    )(q, cq, g, k, ck, vt)


def _pad_cols(w, width):
    return jnp.pad(w, ((0, 0), (0, width - w.shape[1])))


def kernel(x, norm_mix_pre, norm_mix_post, norm_ffn_pre, norm_ffn_post, mlstm_w_in, mlstm_b_gate,
           mlstm_norm, mlstm_w_out, kv_norm, kv_w, kv_b_f, fox_w_qg, fox_w_out, ffn_w_up,
           ffn_conv_w, ffn_conv_b, ffn_w_down):
    b, s, d = x.shape
    depth = norm_mix_pre.shape[0]
    n_a = mlstm_w_in.shape[0]
    xt = x.reshape(b * s, d)
    hk = MLSTM_HEADS * MLSTM_QK_DIM
    hv = MLSTM_HEADS * MLSTM_V_DIM
    k_sh = vt_sh = cq_sh = ck_sh = None
    bf = lambda w: w.astype(_BF16)
    m_q = bf(mlstm_w_in[:, :, :hk] * MLSTM_QK_DIM ** -0.5)
    m_k = bf(mlstm_w_in[:, :, hk:2 * hk])
    m_vt = bf(jnp.swapaxes(mlstm_w_in[:, :, 2 * hk:2 * hk + hv], 1, 2))
    m_o = bf(mlstm_w_in[:, :, 2 * hk + hv:2 * hk + 2 * hv])
    m_gates = bf(jnp.pad(mlstm_w_in[:, :, 2 * hk + 2 * hv:], ((0, 0), (0, 0), (0, LANES - 2 * MLSTM_HEADS))))
    f_q = bf(fox_w_qg[:, :, :d] * (FOX_HEAD_DIM ** -0.5 * LOG2E))
    f_g = bf(fox_w_qg[:, :, d:])
    out_w = [bf(mlstm_w_out), bf(fox_w_out)]
    up_w, down_w = bf(ffn_w_up), bf(ffn_w_down)
    seq = lambda a: a.reshape(b, s, a.shape[-1])
    for layer in range(depth):
        if layer < n_a:
            weights = [m_q[layer], m_k[layer], m_vt[layer], m_o[layer], m_gates[layer]]
            q, k, vt, o, gates = _norm_proj(xt, norm_mix_pre[layer], weights,
                                            [_BF16, _BF16, _BF16, _BF16, _F32], "mlstm_in_proj",
                                            transposed=(False, False, True, False, False))
            hs = _mlstm(seq(q), seq(k), vt, seq(o), seq(gates), mlstm_b_gate[layer], mlstm_norm[layer])
            mixed, w_out = hs.reshape(b * s, hv), out_w[0][layer]
        else:
            j = layer - n_a
            q, g = _norm_proj(xt, norm_mix_pre[layer], [f_q[j], f_g[j]], [_BF16, _BF16], "fox_in_proj")
            att = _fox_attention(seq(q), cq_sh, seq(g), k_sh, ck_sh, vt_sh)
            mixed, w_out = att.reshape(b * s, d), out_w[1][j]
        xt = _mix_ffn(mixed, xt, w_out, norm_mix_post[layer], norm_ffn_pre[layer], up_w[layer],
                      ffn_conv_w[layer], ffn_conv_b[layer], down_w[layer], norm_ffn_post[layer], s)
        if layer == n_a - 1:
            weights = [bf(kv_w[:, :d]), bf(kv_w[:, d:2 * d].T), bf(_pad_cols(kv_w[:, 2 * d:], LANES))]
            k2, vt_sh, f_pre = _norm_proj(xt, kv_norm, weights, [_BF16, _BF16, _F32], "kv_proj",
                                          transposed=(False, True, False))
            k_sh = k2.reshape(b, s, d)
            cq_sh, ck_sh = _fox_decay(f_pre.reshape(b, s, LANES), kv_b_f)
    return xt.reshape(b, s, d)
```

```python
import functools
import math

import jax
import jax.numpy as jnp
from jax import lax
from jax.experimental import pallas as pl
from jax.experimental.pallas import tpu as pltpu

EPS = 1e-6
MLSTM_HEADS = 8
MLSTM_QK_DIM = 64
MLSTM_V_DIM = 128
FOX_HEADS = 16
FOX_HEAD_DIM = 64
CONV_WIDTH = 3

LANES = 128
SUBLANES = 8
VMEM_LIMIT_BYTES = 56 * 1024 * 1024
ROW_TILE = 512
PROJ_TILE = 1024
MLSTM_CHUNK = 128
MLSTM_ROWS = 2
ATTN_TILE = 2048
ATTN_KEYS = 256
FFN_CHUNK = 256
DECAY_PARTS = 3
LOG2E = math.log2(math.e)

_BF16 = jnp.bfloat16
_F32 = jnp.float32


def _params(*semantics):
    return pltpu.CompilerParams(dimension_semantics=semantics, vmem_limit_bytes=VMEM_LIMIT_BYTES)


def _resident(shape):
    return pl.BlockSpec(shape, lambda *_: (0,) * len(shape), pipeline_mode=pl.Buffered(1))


def _rms(x, gain):
    return x * lax.rsqrt(jnp.mean(x * x, axis=-1, keepdims=True) + EPS) * gain


def _norm_proj_kernel(x_ref, g_ref, *refs, transposed):
    n_out = len(transposed)
    w_refs, o_refs = refs[:n_out], refs[n_out:]
    h = _rms(x_ref[...], g_ref[...]).astype(_BF16)
    for w_ref, o_ref, tr in zip(w_refs, o_refs, transposed):
        n = w_ref.shape[0] if tr else w_ref.shape[1]
        for c0 in range(0, n, 512):
            c1 = min(c0 + 512, n)
            if tr:
                y = lax.dot_general(w_ref[c0:c1, :], h, (((1,), (1,)), ((), ())),
                                    preferred_element_type=_F32)
                o_ref[c0:c1, :] = y.astype(o_ref.dtype)
            else:
                y = jnp.dot(h, w_ref[:, c0:c1], preferred_element_type=_F32)
                o_ref[:, c0:c1] = y.astype(o_ref.dtype)


def _norm_proj(x, gain, weights, out_dtypes, name, transposed=None):
    t, d = x.shape
    tm = min(PROJ_TILE, t)
    transposed = tuple(transposed or (False,) * len(weights))
    in_specs = [pl.BlockSpec((tm, d), lambda i: (i, 0)), _resident((1, d))]
    in_specs += [_resident(w.shape) for w in weights]
    out_specs, out_shape = [], []
    for w, dt, tr in zip(weights, out_dtypes, transposed):
        if tr:
            out_specs.append(pl.BlockSpec((w.shape[0], tm), lambda i: (0, i)))
            out_shape.append(jax.ShapeDtypeStruct((w.shape[0], t), dt))
        else:
            out_specs.append(pl.BlockSpec((tm, w.shape[1]), lambda i: (i, 0)))
            out_shape.append(jax.ShapeDtypeStruct((t, w.shape[1]), dt))
    return pl.pallas_call(
        functools.partial(_norm_proj_kernel, transposed=transposed),
        grid=(t // tm,), in_specs=in_specs, out_specs=out_specs, out_shape=out_shape,
        compiler_params=_params("parallel"), name=name,
    )(x, gain.reshape(1, d), *weights)


def _log_sigmoid(z):
    return jnp.minimum(z, 0.0) - jnp.log1p(jnp.exp(-jnp.abs(z)))


ONES_ROWS = 16


def _mlstm_kernel(q_ref, k_ref, o_ref, gc_ref, gr_ref, bc_ref, br_ref, hn_ref, *rest, rows):
    nh, dk, dv = MLSTM_HEADS, MLSTM_QK_DIM, MLSTM_V_DIM
    L = q_ref.shape[1]
    vt_refs, (out_ref, ct_ref, m_ref) = rest[:rows], rest[rows:]

    @pl.when(pl.program_id(1) == 0)
    def _():
        ct_ref[...] = jnp.zeros_like(ct_ref)
        m_ref[...] = jnp.zeros_like(m_ref)

    row = lax.broadcasted_iota(jnp.int32, (L, L), 0)
    col = lax.broadcasted_iota(jnp.int32, (L, L), 1)
    source_before_target = row <= col
    tril = (row >= col).astype(_F32)
    triu = source_before_target.astype(_F32)

    heads = [(bb, h) for bb in range(rows) for h in range(nh)]
    lane_qk = lax.broadcasted_iota(jnp.int32, (L, 2 * dk), 1)
    ones_blk = jnp.ones((ONES_ROWS, L), _BF16)
    first_half = lane_qk < dk

    gates = {}
    for bb in range(rows):
        gc = gc_ref[bb] + bc_ref[...]
        gr = gr_ref[bb] + br_ref[:, 0:1]
        lane_g = lax.broadcasted_iota(jnp.int32, gc.shape, 1)
        sub_g = lax.broadcasted_iota(jnp.int32, gr.shape, 0)
        lf_c = jnp.where(lane_g >= nh, _log_sigmoid(gc), 0.0)
        lf_r = jnp.where(sub_g >= nh, _log_sigmoid(gr), 0.0)
        b_c_all = jnp.dot(tril, lf_c, preferred_element_type=_F32, precision=lax.Precision.HIGHEST)
        b_r_all = jnp.dot(lf_r, triu, preferred_element_type=_F32, precision=lax.Precision.HIGHEST)
        gates[bb] = (gc, gr, b_c_all, b_r_all)

    st = {}
    for bb, h in heads:
        pair = slice((h // 2) * 2 * dk, (h // 2 + 1) * 2 * dk)
        qp, kp = q_ref[bb, :, pair], k_ref[bb, :, pair]
        qm = jnp.where(first_half if h % 2 == 0 else ~first_half, qp, jnp.zeros_like(qp))
        ct = ct_ref[bb * nh + h]
        both = lax.dot_general(jnp.concatenate([kp, ct.astype(_BF16)], axis=0), qm,
                               (((1,), (1,)), ((), ())), preferred_element_type=_F32)
        st[bb, h] = dict(kp=kp, ct=ct, s_qk=both[:L], inter=both[L:])

    for bb, h in heads:
        gc, gr, b_c_all, b_r_all = gates[bb]
        e = st[bb, h]
        g_col = gc[:, h:h + 1] - b_c_all[:, nh + h:nh + h + 1]
        b_row = b_r_all[nh + h:nh + h + 1, :]
        i_row = gr[h:h + 1, :]
        m_prev = m_ref[bb * nh + h:bb * nh + h + 1, 0:1]
        dmat = jnp.where(source_before_target, g_col + b_row, -jnp.inf)
        m_inter = b_row + m_prev
        m_t = jnp.maximum(m_inter, jnp.max(dmat, axis=0, keepdims=True))
        sd = (e["s_qk"] * jnp.exp(dmat - m_t)).astype(_BF16)
        vt_aug = jnp.concatenate([vt_refs[bb][h * dv:(h + 1) * dv, :], ones_blk], axis=0)
        tot = jnp.dot(vt_aug, sd, preferred_element_type=_F32) + jnp.exp(m_inter - m_t) * e["inter"]
        num, den = tot[:dv], tot[dv:dv + 1]
        hv = num / jnp.maximum(jnp.abs(den), jnp.exp(-m_t))
        hv = hv * lax.rsqrt(jnp.mean(hv * hv, axis=0, keepdims=True) + EPS)
        hv = (hv * hn_ref[h * dv:(h + 1) * dv, :]).T
        hv = hv * jax.nn.sigmoid(o_ref[bb, :, h * dv:(h + 1) * dv].astype(_F32))
        out_ref[bb, :, h * dv:(h + 1) * dv] = hv.astype(out_ref.dtype)
        e.update(vt_aug=vt_aug, b_row=b_row, i_row=i_row, m_prev=m_prev)

    for bb, h in heads:
        e = st[bb, h]
        b_row, m_prev = e["b_row"], e["m_prev"]
        b_last = b_row[:, L - 1:L]
        g_row = b_last - b_row + e["i_row"]
        m_new = jnp.maximum(b_last + m_prev, jnp.max(g_row, axis=-1, keepdims=True))
        decay = jnp.exp(b_last + m_prev - m_new)
        vw = (e["vt_aug"].astype(_F32) * jnp.exp(g_row - m_new)).astype(_BF16)
        ct_ref[bb * nh + h] = decay * e["ct"] + jnp.dot(vw, e["kp"], preferred_element_type=_F32)
        m_ref[bb * nh + h:bb * nh + h + 1, :] = jnp.broadcast_to(m_new, (1, m_ref.shape[1]))


def _mlstm(q, k, vt, o, gates, b_gate, head_norm):
    b, s, _ = q.shape
    nh, dk, dv = MLSTM_HEADS, MLSTM_QK_DIM, MLSTM_V_DIM
    L = min(MLSTM_CHUNK, s)
    chunks = s // L
    gates_row = jnp.swapaxes(gates[:, :, :2 * nh], 1, 2)
    bias_col = jnp.zeros((1, LANES), _F32).at[0, :2 * nh].set(b_gate)
    bias_row = b_gate.reshape(2 * nh, 1)
    norm_cols = jnp.broadcast_to(head_norm.astype(_F32)[:, None], (nh * dv, L))
    rows = MLSTM_ROWS if b % MLSTM_ROWS == 0 else 1
    tok = lambda width: pl.BlockSpec((rows, L, width), lambda bi, ci: (bi, ci, 0))
    vt_spec = lambda r: pl.BlockSpec((nh * dv, L), lambda bi, ci: (0, (bi * rows + r) * chunks + ci))
    return pl.pallas_call(
        functools.partial(_mlstm_kernel, rows=rows),
        grid=(b // rows, chunks),
        in_specs=[tok(nh * dk), tok(nh * dk), tok(nh * dv), tok(LANES),
                  pl.BlockSpec((rows, 2 * nh, L), lambda bi, ci: (bi, 0, ci)),
                  _resident((1, LANES)), _resident((2 * nh, 1)), _resident((nh * dv, L))]
                 + [vt_spec(r) for r in range(rows)],
        out_specs=tok(nh * dv),
        out_shape=jax.ShapeDtypeStruct((b, s, nh * dv), _BF16),
        scratch_shapes=[pltpu.VMEM((rows * nh, dv + ONES_ROWS, 2 * dk), _F32),
                        pltpu.VMEM((rows * nh, LANES), _F32)],
        compiler_params=_params("parallel", "arbitrary"), name="mlstm_chunk",
    )(q, k, o, gates, gates_row, bias_col, bias_row, norm_cols, *([vt] * rows))


GELU_C = math.sqrt(2.0 / math.pi)
RUN_PITCH = 72


def _mix_ffn_kernel(a0_ref, x0_ref, a_ref, x_ref, wo_ref, gmix_ref, gpre_ref, wup_ref, cw_ref, cb_ref,
                    wdn_ref, gpost_ref, o_ref,
                    h_ref, xp_ref, y_ref, carry_ref, slab_in_ref, slab_out_ref, hid_ref, *, tiles_per_seq):
    tm, d = x_ref.shape
    f = wdn_ref.shape[0]
    run = tm // SUBLANES
    n_slabs = d // LANES
    step = pl.program_id(0)

    n_slices = SUBLANES
    rows_of = lambda i: slice(i * run, (i + 1) * run)

    def mix_rows(mixed, x_blk_ref, r, zero):
        x1 = x_blk_ref[rows_of(r), :] + _rms(mixed[rows_of(r)] + zero, gmix_ref[...])
        for c in range(n_slabs):
            slab_in_ref[c, r * RUN_PITCH:r * RUN_PITCH + run, :] = x1[:, c * LANES:(c + 1) * LANES]

    def gather_slice(j, slot, zero):
        xp = jnp.concatenate(
            [jnp.concatenate([slab_in_ref[c, pl.ds(k, SUBLANES, stride=RUN_PITCH), :]
                              for k in range(j * SUBLANES, (j + 1) * SUBLANES)], axis=0)
             for c in range(n_slabs)], axis=1) + zero
        xp_ref[slot, rows_of(j), :] = xp
        h_ref[slot, rows_of(j), :] = _rms(xp, gpre_ref[...]).astype(_BF16)

    def finish_slice(j, slot, zero):
        out = xp_ref[slot, rows_of(j), :] + _rms(y_ref[slot, rows_of(j), :] + zero, gpost_ref[...])
        for c in range(n_slabs):
            for kk in range(SUBLANES):
                slab_out_ref[c, pl.ds(j * SUBLANES + kk, SUBLANES, stride=RUN_PITCH), :] = (
                    out[kk * SUBLANES:(kk + 1) * SUBLANES, c * LANES:(c + 1) * LANES])

    def zero_after(value):
        bits = pltpu.bitcast(value[:SUBLANES, :LANES], jnp.uint32)
        bits = lax.shift_right_logical(lax.shift_right_logical(bits, jnp.uint32(16)), jnp.uint32(16))
        return jnp.concatenate([pltpu.bitcast(bits, _F32)[0:1, :]] * n_slabs, axis=1)

    no_wait = jnp.zeros((1, d), _F32)

    @pl.when(step == 0)
    def _():
        y_ref[...] = jnp.zeros_like(y_ref)
        xp_ref[...] = jnp.zeros_like(xp_ref)
        mixed0 = jnp.dot(a0_ref[...], wo_ref[...], preferred_element_type=_F32)
        for r in range(n_slices):
            mix_rows(mixed0, x0_ref, r, no_wait)
        for j in range(n_slices):
            gather_slice(j, 0, no_wait)

    @pl.when(step % tiles_per_seq == 0)
    def _():
        carry_ref[...] = jnp.zeros_like(carry_ref)

    cur = step % 2
    other = (step + 1) % 2

    mixed = jnp.dot(a_ref[...], wo_ref[...], preferred_element_type=_F32)

    h = h_ref[cur]
    first_sublane = lax.broadcasted_iota(jnp.int32, (SUBLANES, FFN_CHUNK), 0) == 0

    def conv(c0, width):
        cols = slice(c0, c0 + width)
        u = jnp.dot(h, wup_ref[:, cols], preferred_element_type=_F32)
        prev = carry_ref[:, cols]
        tail1, tail2 = u[tm - SUBLANES:], u[tm - 2 * SUBLANES:tm - SUBLANES]
        carry_ref[:, cols] = u[tm - 2 * SUBLANES:]
        mask = first_sublane[:, :width]
        wrap1 = jnp.where(mask, prev[2 * SUBLANES - 1:, :], pltpu.roll(tail1, 1, axis=0))
        wrap2 = jnp.where(mask, prev[SUBLANES - 1:SUBLANES, :], pltpu.roll(tail2, 1, axis=0))
        u1 = jnp.concatenate([wrap1, u[:tm - SUBLANES]], axis=0)
        u2 = jnp.concatenate([wrap2, wrap1, u[:tm - 2 * SUBLANES]], axis=0)
        return (cb_ref[:, cols] + cw_ref[2:3, cols] * u + cw_ref[1:2, cols] * u1
                + cw_ref[0:1, cols] * u2), zero_after(u)

    pieces = ([functools.partial(piece, j) for j in range(n_slices)
               for piece in (functools.partial(finish_slice, slot=other),
                             functools.partial(mix_rows, mixed, x_ref))]
              + [functools.partial(gather_slice, j, other) for j in range(n_slices)])
    chunks = list(range(0, f, FFN_CHUNK))
    last_tie = 2 * len(chunks) - 1
    for i, c0 in enumerate(chunks):
        width = min(FFN_CHUNK, f - c0)
        gate, after_gate = conv(c0, width)
        half_val, after_val = conv(f + c0, width)
        inner = gate * ((gate * gate) * (GELU_C * 0.044715) + GELU_C)
        w = gate * half_val
        hid_ref[:, c0:c0 + width] = (w * jnp.tanh(inner) + w).astype(_BF16)
        for tie, zero in ((2 * i, after_gate), (2 * i + 1, after_val)):
            for t, piece in enumerate(pieces):
                if min(t, last_tie) == tie:
                    piece(zero=zero)

    y_ref[cur] = jnp.dot(hid_ref[...], wdn_ref[...], preferred_element_type=_F32)
    for c in range(n_slabs):
        for r in range(SUBLANES):
            o_ref[r * run:(r + 1) * run, c * LANES:(c + 1) * LANES] = (
                slab_out_ref[c, r * RUN_PITCH:r * RUN_PITCH + run, :])


def _mix_ffn(a, x, w_out, g_mix, g_pre, w_up, conv_w, conv_b, w_down, g_post, seq_len):
    t, d = x.shape
    k = a.shape[1]
    f = w_down.shape[0]
    tm = min(ROW_TILE, seq_len)
    n = t // tm
    half = jnp.concatenate([jnp.ones((f,), _F32), jnp.full((f,), 0.5, _F32)])
    first = lambda width: pl.BlockSpec((tm, width), lambda i: (0, 0), pipeline_mode=pl.Buffered(1))
    ahead = lambda width: pl.BlockSpec((tm, width), lambda i: (jnp.minimum(i + 1, n - 1), 0))
    row = lambda width: _resident((1, width))
    return pl.pallas_call(
        functools.partial(_mix_ffn_kernel, tiles_per_seq=seq_len // tm),
        grid=(n + 1,),
        in_specs=[first(k), first(d), ahead(k), ahead(d), _resident((k, d)), row(d), row(d),
                  _resident((d, 2 * f)), _resident((CONV_WIDTH, 2 * f)), row(2 * f), _resident((f, d)),
                  row(d)],
        out_specs=pl.BlockSpec((tm, d), lambda i: (jnp.maximum(i - 1, 0), 0)),
        out_shape=jax.ShapeDtypeStruct((t, d), _F32),
        scratch_shapes=[pltpu.VMEM((2, tm, d), _BF16), pltpu.VMEM((2, tm, d), _F32),
                        pltpu.VMEM((2, tm, d), _F32), pltpu.VMEM((2 * SUBLANES, 2 * f), _F32),
                        pltpu.VMEM((d // LANES, SUBLANES * RUN_PITCH, LANES), _F32),
                        pltpu.VMEM((d // LANES, SUBLANES * RUN_PITCH, LANES), _F32),
                        pltpu.VMEM((tm, f), _BF16)],
        compiler_params=_params("arbitrary"), name="mix_ffn",
    )(a, x, a, x, w_out, g_mix.reshape(1, d), g_pre.reshape(1, d), w_up, conv_w * half,
      (conv_b * half).reshape(1, 2 * f), w_down, g_post.reshape(1, d))


def _decay_kernel(f_ref, bias_ref, place_ref, cq_ref, ck_ref, carry_ref):
    tm = f_ref.shape[1]

    @pl.when(pl.program_id(1) == 0)
    def _():
        carry_ref[...] = jnp.zeros_like(carry_ref)

    lane = lax.broadcasted_iota(jnp.int32, (tm, LANES), 1)
    row = lax.broadcasted_iota(jnp.int32, (tm, LANES), 0)
    c = jnp.where(lane < FOX_HEADS, _log_sigmoid(f_ref[0] + bias_ref[...]), 0.0)
    shift = 1
    while shift < tm:
        c = c + jnp.where(row >= shift, pltpu.roll(c, shift, axis=0), 0.0)
        shift *= 2
    c = carry_ref[0:1, :] + c
    carry_ref[...] = jnp.broadcast_to(c[tm - 1:tm, :], carry_ref.shape)
    c = c * LOG2E

    packed = jnp.where(lane < FOX_HEADS, 1.0, 0.0)
    packed = pltpu.roll(packed, DECAY_PARTS * FOX_HEADS, axis=1)
    rest = c
    for j in range(DECAY_PARTS):
        part = rest.astype(_BF16).astype(_F32)
        rest = rest - part
        packed = packed + (part if j == 0 else pltpu.roll(part, j * FOX_HEADS, axis=1))
    placed = jnp.dot(packed.astype(_BF16), place_ref[...], preferred_element_type=_F32)
    width = cq_ref.shape[2]
    cq_ref[0] = placed[:, :width].astype(cq_ref.dtype)
    ck_ref[0] = placed[:, width:].astype(ck_ref.dtype)


def _decay_placement():
    nh, dh, parts = FOX_HEADS, FOX_HEAD_DIM, DECAY_PARTS
    width = nh * dh
    heads = jnp.arange(nh)
    base = (heads // 2) * 2 * dh + (1 - heads % 2) * dh
    ones_row = parts * nh + heads
    place = jnp.zeros((LANES, 2 * width), _F32)
    for j in range(parts):
        place = place.at[j * nh + heads, base + j].set(1.0)
        place = place.at[ones_row, base + parts + j].set(1.0)
        place = place.at[ones_row, width + base + j].set(1.0)
        place = place.at[j * nh + heads, width + base + parts + j].set(-1.0)
    return place.astype(_BF16)


def _fox_decay(f_pre, bias):
    b, s, _ = f_pre.shape
    nh, dh = FOX_HEADS, FOX_HEAD_DIM
    tm = min(ROW_TILE, s)
    bias_row = jnp.zeros((1, LANES), _F32).at[0, :nh].set(bias)
    out = jax.ShapeDtypeStruct((b, s, nh * dh), _BF16)
    return pl.pallas_call(
        _decay_kernel,
        grid=(b, s // tm),
        in_specs=[pl.BlockSpec((1, tm, LANES), lambda bi, ti: (bi, ti, 0)), _resident((1, LANES)),
                  _resident((LANES, 2 * nh * dh))],
        out_specs=[pl.BlockSpec((1, tm, nh * dh), lambda bi, ti: (bi, ti, 0))] * 2,
        out_shape=[out, out],
        scratch_shapes=[pltpu.VMEM((SUBLANES, LANES), _F32)],
        compiler_params=_params("parallel", "arbitrary"), name="fox_decay",
    )(f_pre, bias_row, _decay_placement())


def _attn_kernel(q_ref, cq_ref, g_ref, k_ref, ck_ref, vt_ref, o_ref,
                 sa_ref, sb_ref, sa_max_ref, sb_max_ref, m_ref, acc_ref):
    tq = q_ref.shape[1]
    tk = sa_ref.shape[1]
    dh = FOX_HEAD_DIM
    qi = pl.program_id(2)
    first_q = lax.broadcasted_iota(jnp.int32, (tq, 2 * dh), 1) < dh
    first_k = lax.broadcasted_iota(jnp.int32, (tk, 2 * dh), 1) < dh
    q_pair, cq_pair = q_ref[0], cq_ref[0]
    q_aug = (jnp.where(first_q, q_pair, cq_pair), jnp.where(first_q, cq_pair, q_pair))

    ones_blk = jnp.ones((ONES_ROWS, tk), _BF16)

    m_ref[...] = jnp.full_like(m_ref, -jnp.inf)
    acc_ref[...] = jnp.zeros_like(acc_ref)

    def keys(j):
        start = pl.multiple_of(j * tk, tk)
        k_pair = k_ref[0, pl.ds(start, tk), :]
        ck_pair = ck_ref[0, pl.ds(start, tk), :]
        return jnp.where(first_k, k_pair, ck_pair), jnp.where(first_k, ck_pair, k_pair)

    def scores(k_aug, bufs, c, hh, diagonal):
        s_ref, smax_ref = bufs
        cols = slice(c * tk, (c + 1) * tk)
        s = lax.dot_general(k_aug[hh], q_aug[hh][cols], (((1,), (1,)), ((), ())),
                            preferred_element_type=_F32)
        if diagonal:
            key = lax.broadcasted_iota(jnp.int32, (tk, tk), 0)
            qry = lax.broadcasted_iota(jnp.int32, (tk, tk), 1)
            s = jnp.where(key <= qry, s, -jnp.inf)
        s_ref[hh, :, cols] = s
        smax_ref[hh, :, cols] = jnp.max(s, axis=0, keepdims=True)

    def accumulate(j, bufs, c, hh):
        s_ref, smax_ref = bufs
        start = pl.multiple_of(j * tk, tk)
        cols = slice(c * tk, (c + 1) * tk)
        m_old = m_ref[hh, :, cols]
        m_new = jnp.maximum(m_old, smax_ref[hh, :, cols])
        alpha = jnp.exp2(m_old - m_new)
        p = jnp.exp2(s_ref[hh, :, cols] - m_new).astype(_BF16)
        vt = jnp.concatenate([vt_ref[hh * dh:(hh + 1) * dh, pl.ds(start, tk)], ones_blk], axis=0)
        acc_ref[hh, :, cols] = alpha * acc_ref[hh, :, cols] + jnp.dot(vt, p, preferred_element_type=_F32)
        m_ref[hh, :, cols] = m_new

    per_tile = tq // tk
    buf_a, buf_b = (sa_ref, sa_max_ref), (sb_ref, sb_max_ref)
    pieces = [(hh, c) for hh in range(2) for c in range(per_tile)]

    first_keys = keys(0)
    for hh, c in pieces:
        scores(first_keys, buf_a, c, hh, False)

    def body(i, carry):
        for u in range(per_tile):
            j = i * per_tile + u
            cur, nxt = (buf_a, buf_b) if u % 2 == 0 else (buf_b, buf_a)
            next_keys = keys(j + 1)
            for hh, c in pieces:
                scores(next_keys, nxt, c, hh, False)
                accumulate(j, cur, c, hh)
        return carry

    lax.fori_loop(0, qi, body, 0)
    for hh in range(2):
        key = lax.broadcasted_iota(jnp.int32, (tk, tk), 0)
        qry = lax.broadcasted_iota(jnp.int32, (tk, tk), 1)
        square = jnp.where(key <= qry, sa_ref[hh, :, :tk], -jnp.inf)
        sa_ref[hh, :, :tk] = square
        sa_max_ref[hh, :, :tk] = jnp.max(square, axis=0, keepdims=True)
    for u in range(per_tile):
        j = qi * per_tile + u
        cur, nxt = (buf_a, buf_b) if u % 2 == 0 else (buf_b, buf_a)
        next_keys = keys(j + 1) if u + 1 < per_tile else None
        for hh in range(2):
            accumulate(j, cur, u, hh)
            for c in range(u + 1, per_tile):
                scores(next_keys, nxt, c, hh, c == u + 1)
                accumulate(j, cur, c, hh)

    halves = []
    for hh in range(2):
        acc = acc_ref[hh]
        halves.append(acc[0:dh, :] / acc[dh:dh + 1, :])
    out = jnp.concatenate(halves, axis=0).T
    o_ref[0] = (out * jax.nn.sigmoid(g_ref[0].astype(_F32))).astype(o_ref.dtype)


def _fox_attention(q, cq, g, k, ck, vt):
    b, s, hd = q.shape
    pairs = hd // LANES
    tq = min(ATTN_TILE, s)
    tk = min(ATTN_KEYS, tq // 2)
    rows = FOX_HEAD_DIM + ONES_ROWS
    qspec = pl.BlockSpec((1, tq, LANES), lambda bi, pi, qi: (bi, qi, pi))
    kspec = pl.BlockSpec((1, s, LANES), lambda bi, pi, qi: (bi, 0, pi))
    vspec = pl.BlockSpec((LANES, s), lambda bi, pi, qi: (pi, bi))
    return pl.pallas_call(
        _attn_kernel,
        grid=(b, pairs, s // tq),
        in_specs=[qspec, qspec, qspec, kspec, kspec, vspec],
        out_specs=qspec,
        out_shape=jax.ShapeDtypeStruct((b, s, hd), _BF16),
        scratch_shapes=[pltpu.VMEM((2, tk, tq), _F32), pltpu.VMEM((2, tk, tq), _F32),
                        pltpu.VMEM((2, 1, tq), _F32), pltpu.VMEM((2, 1, tq), _F32),
                        pltpu.VMEM((2, 1, tq), _F32), pltpu.VMEM((2, rows, tq), _F32)],
        compiler_params=_params("parallel", "parallel", "arbitrary"), name="fox_attention",
    )(q, cq, g, k, ck, vt)


def _pad_cols(w, width):
    return jnp.pad(w, ((0, 0), (0, width - w.shape[1])))


def kernel(x, norm_mix_pre, norm_mix_post, norm_ffn_pre, norm_ffn_post, mlstm_w_in, mlstm_b_gate,
           mlstm_norm, mlstm_w_out, kv_norm, kv_w, kv_b_f, fox_w_qg, fox_w_out, ffn_w_up,
           ffn_conv_w, ffn_conv_b, ffn_w_down):
    b, s, d = x.shape
    depth = norm_mix_pre.shape[0]
    n_a = mlstm_w_in.shape[0]
    xt = x.reshape(b * s, d)
    hk = MLSTM_HEADS * MLSTM_QK_DIM
    hv = MLSTM_HEADS * MLSTM_V_DIM
    k_sh = vt_sh = cq_sh = ck_sh = None
    bf = lambda w: w.astype(_BF16)
    m_q = bf(mlstm_w_in[:, :, :hk] * MLSTM_QK_DIM ** -0.5)
    m_k = bf(mlstm_w_in[:, :, hk:2 * hk])
    m_vt = bf(jnp.swapaxes(mlstm_w_in[:, :, 2 * hk:2 * hk + hv], 1, 2))
    m_o = bf(mlstm_w_in[:, :, 2 * hk + hv:2 * hk + 2 * hv])
    m_gates = bf(jnp.pad(mlstm_w_in[:, :, 2 * hk + 2 * hv:], ((0, 0), (0, 0), (0, LANES - 2 * MLSTM_HEADS))))
    f_q = bf(fox_w_qg[:, :, :d] * (FOX_HEAD_DIM ** -0.5 * LOG2E))
    f_g = bf(fox_w_qg[:, :, d:])
    out_w = [bf(mlstm_w_out), bf(fox_w_out)]
    up_w, down_w = bf(ffn_w_up), bf(ffn_w_down)
    seq = lambda a: a.reshape(b, s, a.shape[-1])
    for layer in range(depth):
        if layer < n_a:
            weights = [m_q[layer], m_k[layer], m_vt[layer], m_o[layer], m_gates[layer]]
            q, k, vt, o, gates = _norm_proj(xt, norm_mix_pre[layer], weights,
                                            [_BF16, _BF16, _BF16, _BF16, _F32], "mlstm_in_proj",
                                            transposed=(False, False, True, False, False))
            hs = _mlstm(seq(q), seq(k), vt, seq(o), seq(gates), mlstm_b_gate[layer], mlstm_norm[layer])
            mixed, w_out = hs.reshape(b * s, hv), out_w[0][layer]
        else:
            j = layer - n_a
            q, g = _norm_proj(xt, norm_mix_pre[layer], [f_q[j], f_g[j]], [_BF16, _BF16], "fox_in_proj")
            att = _fox_attention(seq(q), cq_sh, seq(g), k_sh, ck_sh, vt_sh)
            mixed, w_out = att.reshape(b * s, d), out_w[1][j]
        xt = _mix_ffn(mixed, xt, w_out, norm_mix_post[layer], norm_ffn_pre[layer], up_w[layer],
                      ffn_conv_w[layer], ffn_conv_b[layer], down_w[layer], norm_ffn_post[layer], s)
        if layer == n_a - 1:
            weights = [bf(kv_w[:, :d]), bf(kv_w[:, d:2 * d].T), bf(_pad_cols(kv_w[:, 2 * d:], LANES))]
            k2, vt_sh, f_pre = _norm_proj(xt, kv_norm, weights, [_BF16, _BF16, _F32], "kv_proj",
                                          transposed=(False, True, False))
            k_sh = k2.reshape(b, s, d)
            cq_sh, ck_sh = _fox_decay(f_pre.reshape(b, s, LANES), kv_b_f)
    return xt.reshape(b, s, d)
```

```python
import functools
import math

import jax
import jax.numpy as jnp
from jax import lax
from jax.experimental import pallas as pl
from jax.experimental.pallas import tpu as pltpu

EPS = 1e-6
MLSTM_HEADS = 8
MLSTM_QK_DIM = 64
MLSTM_V_DIM = 128
FOX_HEADS = 16
FOX_HEAD_DIM = 64
CONV_WIDTH = 3

LANES = 128
SUBLANES = 8
VMEM_LIMIT_BYTES = 56 * 1024 * 1024
ROW_TILE = 512
PROJ_TILE = 1024
PROJ_COLS = 512
MLSTM_CHUNK = 128
MLSTM_ROWS = 4
ATTN_TILE = 2048
ATTN_KEYS = 256
FFN_CHUNK = 256
DECAY_PARTS = 3
LOG2E = math.log2(math.e)

_BF16 = jnp.bfloat16
_F32 = jnp.float32


def _params(*semantics):
    return pltpu.CompilerParams(dimension_semantics=semantics, vmem_limit_bytes=VMEM_LIMIT_BYTES)


def _resident(shape):
    return pl.BlockSpec(shape, lambda *_: (0,) * len(shape), pipeline_mode=pl.Buffered(1))


def _rms(x, gain):
    return x * lax.rsqrt(jnp.mean(x * x, axis=-1, keepdims=True) + EPS) * gain


def _norm_proj_kernel(x_ref, g_ref, *refs, transposed):
    n_out = len(transposed)
    w_refs, o_refs = refs[:n_out], refs[n_out:]
    h = _rms(x_ref[...], g_ref[...]).astype(_BF16)
    for w_ref, o_ref, tr in zip(w_refs, o_refs, transposed):
        n = w_ref.shape[0] if tr else w_ref.shape[1]
        for c0 in range(0, n, PROJ_COLS):
            c1 = min(c0 + PROJ_COLS, n)
            if tr:
                y = lax.dot_general(w_ref[c0:c1, :], h, (((1,), (1,)), ((), ())),
                                    preferred_element_type=_F32)
                o_ref[c0:c1, :] = y.astype(o_ref.dtype)
            else:
                y = jnp.dot(h, w_ref[:, c0:c1], preferred_element_type=_F32)
                o_ref[:, c0:c1] = y.astype(o_ref.dtype)


def _norm_proj(x, gain, weights, out_dtypes, name, transposed=None):
    t, d = x.shape
    tm = min(PROJ_TILE, t)
    transposed = tuple(transposed or (False,) * len(weights))
    in_specs = [pl.BlockSpec((tm, d), lambda i: (i, 0)), _resident((1, d))]
    in_specs += [_resident(w.shape) for w in weights]
    out_specs, out_shape = [], []
    for w, dt, tr in zip(weights, out_dtypes, transposed):
        if tr:
            out_specs.append(pl.BlockSpec((w.shape[0], tm), lambda i: (0, i)))
            out_shape.append(jax.ShapeDtypeStruct((w.shape[0], t), dt))
        else:
            out_specs.append(pl.BlockSpec((tm, w.shape[1]), lambda i: (i, 0)))
            out_shape.append(jax.ShapeDtypeStruct((t, w.shape[1]), dt))
    return pl.pallas_call(
        functools.partial(_norm_proj_kernel, transposed=transposed),
        grid=(t // tm,), in_specs=in_specs, out_specs=out_specs, out_shape=out_shape,
        compiler_params=_params("parallel"), name=name,
    )(x, gain.reshape(1, d), *weights)


def _log_sigmoid(z):
    return jnp.minimum(z, 0.0) - jnp.log1p(jnp.exp(-jnp.abs(z)))


ONES_ROWS = 16


def _mlstm_kernel(q_ref, k_ref, o_ref, gc_ref, gr_ref, bc_ref, br_ref, hn_ref, *rest, rows):
    nh, dk, dv = MLSTM_HEADS, MLSTM_QK_DIM, MLSTM_V_DIM
    L = q_ref.shape[1]
    vt_refs, (out_ref, ct_ref, m_ref) = rest[:rows], rest[rows:]

    @pl.when(pl.program_id(1) == 0)
    def _():
        ct_ref[...] = jnp.zeros_like(ct_ref)
        m_ref[...] = jnp.zeros_like(m_ref)

    row = lax.broadcasted_iota(jnp.int32, (L, L), 0)
    col = lax.broadcasted_iota(jnp.int32, (L, L), 1)
    source_before_target = row <= col
    tril = (row >= col).astype(_F32)
    triu = source_before_target.astype(_F32)

    heads = [(bb, h) for bb in range(rows) for h in range(nh)]
    lane_qk = lax.broadcasted_iota(jnp.int32, (L, 2 * dk), 1)
    ones_blk = jnp.ones((ONES_ROWS, L), _BF16)
    first_half = lane_qk < dk

    gates = {}
    for bb in range(rows):
        gc = gc_ref[bb] + bc_ref[...]
        gr = gr_ref[bb] + br_ref[:, 0:1]
        lane_g = lax.broadcasted_iota(jnp.int32, gc.shape, 1)
        sub_g = lax.broadcasted_iota(jnp.int32, gr.shape, 0)
        lf_c = jnp.where(lane_g >= nh, _log_sigmoid(gc), 0.0)
        lf_r = jnp.where(sub_g >= nh, _log_sigmoid(gr), 0.0)
        b_c_all = jnp.dot(tril, lf_c, preferred_element_type=_F32, precision=lax.Precision.HIGHEST)
        b_r_all = jnp.dot(lf_r, triu, preferred_element_type=_F32, precision=lax.Precision.HIGHEST)
        gates[bb] = (gc, gr, b_c_all, b_r_all)

    st = {}
    for bb, h in heads:
        pair = slice((h // 2) * 2 * dk, (h // 2 + 1) * 2 * dk)
        qp, kp = q_ref[bb, :, pair], k_ref[bb, :, pair]
        qm = jnp.where(first_half if h % 2 == 0 else ~first_half, qp, jnp.zeros_like(qp))
        ct = ct_ref[bb * nh + h]
        both = lax.dot_general(jnp.concatenate([kp, ct.astype(_BF16)], axis=0), qm,
                               (((1,), (1,)), ((), ())), preferred_element_type=_F32)
        st[bb, h] = dict(kp=kp, ct=ct, s_qk=both[:L], inter=both[L:])

    for bb, h in heads:
        gc, gr, b_c_all, b_r_all = gates[bb]
        e = st[bb, h]
        g_col = gc[:, h:h + 1] - b_c_all[:, nh + h:nh + h + 1]
        b_row = b_r_all[nh + h:nh + h + 1, :]
        i_row = gr[h:h + 1, :]
        m_prev = m_ref[bb * nh + h:bb * nh + h + 1, 0:1]
        dmat = jnp.where(source_before_target, g_col + b_row, -jnp.inf)
        m_inter = b_row + m_prev
        m_t = jnp.maximum(m_inter, jnp.max(dmat, axis=0, keepdims=True))
        sd = (e["s_qk"] * jnp.exp(dmat - m_t)).astype(_BF16)
        vt_aug = jnp.concatenate([vt_refs[bb][h * dv:(h + 1) * dv, :], ones_blk], axis=0)
        tot = jnp.dot(vt_aug, sd, preferred_element_type=_F32) + jnp.exp(m_inter - m_t) * e["inter"]
        num, den = tot[:dv], tot[dv:dv + 1]
        hv = num / jnp.maximum(jnp.abs(den), jnp.exp(-m_t))
        hv = hv * lax.rsqrt(jnp.mean(hv * hv, axis=0, keepdims=True) + EPS)
        hv = (hv * hn_ref[h * dv:(h + 1) * dv, :]).T
        hv = hv * jax.nn.sigmoid(o_ref[bb, :, h * dv:(h + 1) * dv].astype(_F32))
        out_ref[bb, :, h * dv:(h + 1) * dv] = hv.astype(out_ref.dtype)
        e.update(vt_aug=vt_aug, b_row=b_row, i_row=i_row, m_prev=m_prev)

    for bb, h in heads:
        e = st[bb, h]
        b_row, m_prev = e["b_row"], e["m_prev"]
        b_last = b_row[:, L - 1:L]
        g_row = b_last - b_row + e["i_row"]
        m_new = jnp.maximum(b_last + m_prev, jnp.max(g_row, axis=-1, keepdims=True))
        decay = jnp.exp(b_last + m_prev - m_new)
        vw = (e["vt_aug"].astype(_F32) * jnp.exp(g_row - m_new)).astype(_BF16)
        ct_ref[bb * nh + h] = decay * e["ct"] + jnp.dot(vw, e["kp"], preferred_element_type=_F32)
        m_ref[bb * nh + h:bb * nh + h + 1, :] = jnp.broadcast_to(m_new, (1, m_ref.shape[1]))


def _mlstm(q, k, vt, o, gates, b_gate, head_norm):
    b, s, _ = q.shape
    nh, dk, dv = MLSTM_HEADS, MLSTM_QK_DIM, MLSTM_V_DIM
    L = min(MLSTM_CHUNK, s)
    chunks = s // L
    gates_row = jnp.swapaxes(gates[:, :, :2 * nh], 1, 2)
    bias_col = jnp.zeros((1, LANES), _F32).at[0, :2 * nh].set(b_gate)
    bias_row = b_gate.reshape(2 * nh, 1)
    norm_cols = jnp.broadcast_to(head_norm.astype(_F32)[:, None], (nh * dv, L))
    rows = MLSTM_ROWS if b % MLSTM_ROWS == 0 else 1
    tok = lambda width: pl.BlockSpec((rows, L, width), lambda bi, ci: (bi, ci, 0))
    vt_spec = lambda r: pl.BlockSpec((nh * dv, L), lambda bi, ci: (0, (bi * rows + r) * chunks + ci))
    return pl.pallas_call(
        functools.partial(_mlstm_kernel, rows=rows),
        grid=(b // rows, chunks),
        in_specs=[tok(nh * dk), tok(nh * dk), tok(nh * dv), tok(LANES),
                  pl.BlockSpec((rows, 2 * nh, L), lambda bi, ci: (bi, 0, ci)),
                  _resident((1, LANES)), _resident((2 * nh, 1)), _resident((nh * dv, L))]
                 + [vt_spec(r) for r in range(rows)],
        out_specs=tok(nh * dv),
        out_shape=jax.ShapeDtypeStruct((b, s, nh * dv), _BF16),
        scratch_shapes=[pltpu.VMEM((rows * nh, dv + ONES_ROWS, 2 * dk), _F32),
                        pltpu.VMEM((rows * nh, LANES), _F32)],
        compiler_params=_params("parallel", "arbitrary"), name="mlstm_chunk",
    )(q, k, o, gates, gates_row, bias_col, bias_row, norm_cols, *([vt] * rows))


GELU_C = math.sqrt(2.0 / math.pi)
RUN_PITCH = 72


def _mix_ffn_kernel(a0_ref, x0_ref, a_ref, x_ref, wo_ref, gmix_ref, gpre_ref, wup_ref, cw_ref, cb_ref,
                    wdn_ref, gpost_ref, o_ref,
                    h_ref, xp_ref, y_ref, carry_ref, slab_in_ref, slab_out_ref, hid_ref, *, tiles_per_seq):
    tm, d = x_ref.shape
    f = wdn_ref.shape[0]
    run = tm // SUBLANES
    n_slabs = d // LANES
    step = pl.program_id(0)

    n_slices = SUBLANES
    rows_of = lambda i: slice(i * run, (i + 1) * run)

    def mix_rows(mixed, x_blk_ref, r, zero):
        x1 = x_blk_ref[rows_of(r), :] + _rms(mixed[rows_of(r)] + zero, gmix_ref[...])
        for c in range(n_slabs):
            slab_in_ref[c, r * RUN_PITCH:r * RUN_PITCH + run, :] = x1[:, c * LANES:(c + 1) * LANES]

    def gather_slice(j, slot, zero):
        xp = jnp.concatenate(
            [jnp.concatenate([slab_in_ref[c, pl.ds(k, SUBLANES, stride=RUN_PITCH), :]
                              for k in range(j * SUBLANES, (j + 1) * SUBLANES)], axis=0)
             for c in range(n_slabs)], axis=1) + zero
        xp_ref[slot, rows_of(j), :] = xp
        h_ref[slot, rows_of(j), :] = _rms(xp, gpre_ref[...]).astype(_BF16)

    def finish_slice(j, slot, zero):
        out = xp_ref[slot, rows_of(j), :] + _rms(y_ref[slot, rows_of(j), :] + zero, gpost_ref[...])
        for c in range(n_slabs):
            for kk in range(SUBLANES):
                slab_out_ref[c, pl.ds(j * SUBLANES + kk, SUBLANES, stride=RUN_PITCH), :] = (
                    out[kk * SUBLANES:(kk + 1) * SUBLANES, c * LANES:(c + 1) * LANES])

    def zero_after(value):
        bits = pltpu.bitcast(value[:SUBLANES, :LANES], jnp.uint32)
        bits = lax.shift_right_logical(lax.shift_right_logical(bits, jnp.uint32(16)), jnp.uint32(16))
        return jnp.concatenate([pltpu.bitcast(bits, _F32)[0:1, :]] * n_slabs, axis=1)

    no_wait = jnp.zeros((1, d), _F32)

    @pl.when(step == 0)
    def _():
        y_ref[...] = jnp.zeros_like(y_ref)
        xp_ref[...] = jnp.zeros_like(xp_ref)
        mixed0 = jnp.dot(a0_ref[...], wo_ref[...], preferred_element_type=_F32)
        for r in range(n_slices):
            mix_rows(mixed0, x0_ref, r, no_wait)
        for j in range(n_slices):
            gather_slice(j, 0, no_wait)

    @pl.when(step % tiles_per_seq == 0)
    def _():
        carry_ref[...] = jnp.zeros_like(carry_ref)

    cur = step % 2
    other = (step + 1) % 2

    mixed = jnp.dot(a_ref[...], wo_ref[...], preferred_element_type=_F32)

    h = h_ref[cur]
    first_sublane = lax.broadcasted_iota(jnp.int32, (SUBLANES, FFN_CHUNK), 0) == 0

    def conv(c0, width):
        cols = slice(c0, c0 + width)
        u = jnp.dot(h, wup_ref[:, cols], preferred_element_type=_F32)
        prev = carry_ref[:, cols]
        tail1, tail2 = u[tm - SUBLANES:], u[tm - 2 * SUBLANES:tm - SUBLANES]
        carry_ref[:, cols] = u[tm - 2 * SUBLANES:]
        mask = first_sublane[:, :width]
        wrap1 = jnp.where(mask, prev[2 * SUBLANES - 1:, :], pltpu.roll(tail1, 1, axis=0))
        wrap2 = jnp.where(mask, prev[SUBLANES - 1:SUBLANES, :], pltpu.roll(tail2, 1, axis=0))
        u1 = jnp.concatenate([wrap1, u[:tm - SUBLANES]], axis=0)
        u2 = jnp.concatenate([wrap2, wrap1, u[:tm - 2 * SUBLANES]], axis=0)
        return (cb_ref[:, cols] + cw_ref[2:3, cols] * u + cw_ref[1:2, cols] * u1
                + cw_ref[0:1, cols] * u2), zero_after(u)

    pieces = ([functools.partial(piece, j) for j in range(n_slices)
               for piece in (functools.partial(finish_slice, slot=other),
                             functools.partial(mix_rows, mixed, x_ref))]
              + [functools.partial(gather_slice, j, other) for j in range(n_slices)])
    chunks = list(range(0, f, FFN_CHUNK))
    last_tie = 2 * len(chunks) - 1
    for i, c0 in enumerate(chunks):
        width = min(FFN_CHUNK, f - c0)
        gate, after_gate = conv(c0, width)
        half_val, after_val = conv(f + c0, width)
        inner = gate * ((gate * gate) * (GELU_C * 0.044715) + GELU_C)
        w = gate * half_val
        hid_ref[:, c0:c0 + width] = (w * jnp.tanh(inner) + w).astype(_BF16)
        for tie, zero in ((2 * i, after_gate), (2 * i + 1, after_val)):
            for t, piece in enumerate(pieces):
                if min(t, last_tie) == tie:
                    piece(zero=zero)

    y_ref[cur] = jnp.dot(hid_ref[...], wdn_ref[...], preferred_element_type=_F32)
    for c in range(n_slabs):
        for r in range(SUBLANES):
            o_ref[r * run:(r + 1) * run, c * LANES:(c + 1) * LANES] = (
                slab_out_ref[c, r * RUN_PITCH:r * RUN_PITCH + run, :])


def _mix_ffn(a, x, w_out, g_mix, g_pre, w_up, conv_w, conv_b, w_down, g_post, seq_len):
    t, d = x.shape
    k = a.shape[1]
    f = w_down.shape[0]
    tm = min(ROW_TILE, seq_len)
    n = t // tm
    half = jnp.concatenate([jnp.ones((f,), _F32), jnp.full((f,), 0.5, _F32)])
    first = lambda width: pl.BlockSpec((tm, width), lambda i: (0, 0), pipeline_mode=pl.Buffered(1))
    ahead = lambda width: pl.BlockSpec((tm, width), lambda i: (jnp.minimum(i + 1, n - 1), 0))
    row = lambda width: _resident((1, width))
    return pl.pallas_call(
        functools.partial(_mix_ffn_kernel, tiles_per_seq=seq_len // tm),
        grid=(n + 1,),
        in_specs=[first(k), first(d), ahead(k), ahead(d), _resident((k, d)), row(d), row(d),
                  _resident((d, 2 * f)), _resident((CONV_WIDTH, 2 * f)), row(2 * f), _resident((f, d)),
                  row(d)],
        out_specs=pl.BlockSpec((tm, d), lambda i: (jnp.maximum(i - 1, 0), 0)),
        out_shape=jax.ShapeDtypeStruct((t, d), _F32),
        scratch_shapes=[pltpu.VMEM((2, tm, d), _BF16), pltpu.VMEM((2, tm, d), _F32),
                        pltpu.VMEM((2, tm, d), _F32), pltpu.VMEM((2 * SUBLANES, 2 * f), _F32),
                        pltpu.VMEM((d // LANES, SUBLANES * RUN_PITCH, LANES), _F32),
                        pltpu.VMEM((d // LANES, SUBLANES * RUN_PITCH, LANES), _F32),
                        pltpu.VMEM((tm, f), _BF16)],
        compiler_params=_params("arbitrary"), name="mix_ffn",
    )(a, x, a, x, w_out, g_mix.reshape(1, d), g_pre.reshape(1, d), w_up, conv_w * half,
      (conv_b * half).reshape(1, 2 * f), w_down, g_post.reshape(1, d))


def _decay_kernel(f_ref, bias_ref, place_ref, cq_ref, ck_ref, carry_ref):
    tm = f_ref.shape[1]

    @pl.when(pl.program_id(1) == 0)
    def _():
        carry_ref[...] = jnp.zeros_like(carry_ref)

    lane = lax.broadcasted_iota(jnp.int32, (tm, LANES), 1)
    row = lax.broadcasted_iota(jnp.int32, (tm, LANES), 0)
    c = jnp.where(lane < FOX_HEADS, _log_sigmoid(f_ref[0] + bias_ref[...]), 0.0)
    shift = 1
    while shift < tm:
        c = c + jnp.where(row >= shift, pltpu.roll(c, shift, axis=0), 0.0)
        shift *= 2
    c = carry_ref[0:1, :] + c
    carry_ref[...] = jnp.broadcast_to(c[tm - 1:tm, :], carry_ref.shape)
    c = c * LOG2E

    packed = jnp.where(lane < FOX_HEADS, 1.0, 0.0)
    packed = pltpu.roll(packed, DECAY_PARTS * FOX_HEADS, axis=1)
    rest = c
    for j in range(DECAY_PARTS):
        part = rest.astype(_BF16).astype(_F32)
        rest = rest - part
        packed = packed + (part if j == 0 else pltpu.roll(part, j * FOX_HEADS, axis=1))
    placed = jnp.dot(packed.astype(_BF16), place_ref[...], preferred_element_type=_F32)
    width = cq_ref.shape[2]
    cq_ref[0] = placed[:, :width].astype(cq_ref.dtype)
    ck_ref[0] = placed[:, width:].astype(ck_ref.dtype)


def _decay_placement():
    nh, dh, parts = FOX_HEADS, FOX_HEAD_DIM, DECAY_PARTS
    width = nh * dh
    heads = jnp.arange(nh)
    base = (heads // 2) * 2 * dh + (1 - heads % 2) * dh
    ones_row = parts * nh + heads
    place = jnp.zeros((LANES, 2 * width), _F32)
    for j in range(parts):
        place = place.at[j * nh + heads, base + j].set(1.0)
        place = place.at[ones_row, base + parts + j].set(1.0)
        place = place.at[ones_row, width + base + j].set(1.0)
        place = place.at[j * nh + heads, width + base + parts + j].set(-1.0)
    return place.astype(_BF16)


def _fox_decay(f_pre, bias):
    b, s, _ = f_pre.shape
    nh, dh = FOX_HEADS, FOX_HEAD_DIM
    tm = min(ROW_TILE, s)
    bias_row = jnp.zeros((1, LANES), _F32).at[0, :nh].set(bias)
    out = jax.ShapeDtypeStruct((b, s, nh * dh), _BF16)
    return pl.pallas_call(
        _decay_kernel,
        grid=(b, s // tm),
        in_specs=[pl.BlockSpec((1, tm, LANES), lambda bi, ti: (bi, ti, 0)), _resident((1, LANES)),
                  _resident((LANES, 2 * nh * dh))],
        out_specs=[pl.BlockSpec((1, tm, nh * dh), lambda bi, ti: (bi, ti, 0))] * 2,
        out_shape=[out, out],
        scratch_shapes=[pltpu.VMEM((SUBLANES, LANES), _F32)],
        compiler_params=_params("parallel", "arbitrary"), name="fox_decay",
    )(f_pre, bias_row, _decay_placement())


def _attn_kernel(q_ref, cq_ref, g_ref, k_ref, ck_ref, vt_ref, o_ref,
                 sa_ref, sb_ref, sa_max_ref, sb_max_ref, m_ref, acc_ref):
    tq = q_ref.shape[1]
    tk = sa_ref.shape[1]
    dh = FOX_HEAD_DIM
    qi = pl.program_id(2)
    first_q = lax.broadcasted_iota(jnp.int32, (tq, 2 * dh), 1) < dh
    first_k = lax.broadcasted_iota(jnp.int32, (tk, 2 * dh), 1) < dh
    q_pair, cq_pair = q_ref[0], cq_ref[0]
    q_aug = (jnp.where(first_q, q_pair, cq_pair), jnp.where(first_q, cq_pair, q_pair))

    ones_blk = jnp.ones((ONES_ROWS, tk), _BF16)

    m_ref[...] = jnp.full_like(m_ref, -jnp.inf)
    acc_ref[...] = jnp.zeros_like(acc_ref)

    def keys(j):
        start = pl.multiple_of(j * tk, tk)
        k_pair = k_ref[0, pl.ds(start, tk), :]
        ck_pair = ck_ref[0, pl.ds(start, tk), :]
        return jnp.where(first_k, k_pair, ck_pair), jnp.where(first_k, ck_pair, k_pair)

    def scores(k_aug, bufs, c, hh, diagonal):
        s_ref, smax_ref = bufs
        cols = slice(c * tk, (c + 1) * tk)
        s = lax.dot_general(k_aug[hh], q_aug[hh][cols], (((1,), (1,)), ((), ())),
                            preferred_element_type=_F32)
        if diagonal:
            key = lax.broadcasted_iota(jnp.int32, (tk, tk), 0)
            qry = lax.broadcasted_iota(jnp.int32, (tk, tk), 1)
            s = jnp.where(key <= qry, s, -jnp.inf)
        s_ref[hh, :, cols] = s
        smax_ref[hh, :, cols] = jnp.max(s, axis=0, keepdims=True)

    def accumulate(j, bufs, c, hh):
        s_ref, smax_ref = bufs
        start = pl.multiple_of(j * tk, tk)
        cols = slice(c * tk, (c + 1) * tk)
        m_old = m_ref[hh, :, cols]
        m_new = jnp.maximum(m_old, smax_ref[hh, :, cols])
        alpha = jnp.exp2(m_old - m_new)
        p = jnp.exp2(s_ref[hh, :, cols] - m_new).astype(_BF16)
        vt = jnp.concatenate([vt_ref[hh * dh:(hh + 1) * dh, pl.ds(start, tk)], ones_blk], axis=0)
        acc_ref[hh, :, cols] = alpha * acc_ref[hh, :, cols] + jnp.dot(vt, p, preferred_element_type=_F32)
        m_ref[hh, :, cols] = m_new

    per_tile = tq // tk
    buf_a, buf_b = (sa_ref, sa_max_ref), (sb_ref, sb_max_ref)
    pieces = [(hh, c) for hh in range(2) for c in range(per_tile)]

    first_keys = keys(0)
    for hh, c in pieces:
        scores(first_keys, buf_a, c, hh, False)

    def body(i, carry):
        for u in range(per_tile):
            j = i * per_tile + u
            cur, nxt = (buf_a, buf_b) if u % 2 == 0 else (buf_b, buf_a)
            next_keys = keys(j + 1)
            for hh, c in pieces:
                scores(next_keys, nxt, c, hh, False)
                accumulate(j, cur, c, hh)
        return carry

    lax.fori_loop(0, qi, body, 0)
    for hh in range(2):
        key = lax.broadcasted_iota(jnp.int32, (tk, tk), 0)
        qry = lax.broadcasted_iota(jnp.int32, (tk, tk), 1)
        square = jnp.where(key <= qry, sa_ref[hh, :, :tk], -jnp.inf)
        sa_ref[hh, :, :tk] = square
        sa_max_ref[hh, :, :tk] = jnp.max(square, axis=0, keepdims=True)
    for u in range(per_tile):
        j = qi * per_tile + u
        cur, nxt = (buf_a, buf_b) if u % 2 == 0 else (buf_b, buf_a)
        next_keys = keys(j + 1) if u + 1 < per_tile else None
        for hh in range(2):
            accumulate(j, cur, u, hh)
            for c in range(u + 1, per_tile):
                scores(next_keys, nxt, c, hh, c == u + 1)
                accumulate(j, cur, c, hh)

    halves = []
    for hh in range(2):
        acc = acc_ref[hh]
        halves.append(acc[0:dh, :] / acc[dh:dh + 1, :])
    out = jnp.concatenate(halves, axis=0).T
    o_ref[0] = (out * jax.nn.sigmoid(g_ref[0].astype(_F32))).astype(o_ref.dtype)


def _fox_attention(q, cq, g, k, ck, vt):
    b, s, hd = q.shape
    pairs = hd // LANES
    tq = min(ATTN_TILE, s)
    tk = min(ATTN_KEYS, tq // 2)
    rows = FOX_HEAD_DIM + ONES_ROWS
    qspec = pl.BlockSpec((1, tq, LANES), lambda bi, pi, qi: (bi, qi, pi))
    kspec = pl.BlockSpec((1, s, LANES), lambda bi, pi, qi: (bi, 0, pi))
    vspec = pl.BlockSpec((LANES, s), lambda bi, pi, qi: (pi, bi))
    return pl.pallas_call(
        _attn_kernel,
        grid=(b, pairs, s // tq),
        in_specs=[qspec, qspec, qspec, kspec, kspec, vspec],
        out_specs=qspec,
        out_shape=jax.ShapeDtypeStruct((b, s, hd), _BF16),
        scratch_shapes=[pltpu.VMEM((2, tk, tq), _F32), pltpu.VMEM((2, tk, tq), _F32),
                        pltpu.VMEM((2, 1, tq), _F32), pltpu.VMEM((2, 1, tq), _F32),
                        pltpu.VMEM((2, 1, tq), _F32), pltpu.VMEM((2, rows, tq), _F32)],
        compiler_params=_params("parallel", "parallel", "arbitrary"), name="fox_attention",
    )(q, cq, g, k, ck, vt)


def _pad_cols(w, width):
    return jnp.pad(w, ((0, 0), (0, width - w.shape[1])))


def kernel(x, norm_mix_pre, norm_mix_post, norm_ffn_pre, norm_ffn_post, mlstm_w_in, mlstm_b_gate,
           mlstm_norm, mlstm_w_out, kv_norm, kv_w, kv_b_f, fox_w_qg, fox_w_out, ffn_w_up,
           ffn_conv_w, ffn_conv_b, ffn_w_down):
    b, s, d = x.shape
    depth = norm_mix_pre.shape[0]
    n_a = mlstm_w_in.shape[0]
    xt = x.reshape(b * s, d)
    hk = MLSTM_HEADS * MLSTM_QK_DIM
    hv = MLSTM_HEADS * MLSTM_V_DIM
    k_sh = vt_sh = cq_sh = ck_sh = None
    bf = lambda w: w.astype(_BF16)
    m_q = bf(mlstm_w_in[:, :, :hk] * MLSTM_QK_DIM ** -0.5)
    m_k = bf(mlstm_w_in[:, :, hk:2 * hk])
    m_vt = bf(jnp.swapaxes(mlstm_w_in[:, :, 2 * hk:2 * hk + hv], 1, 2))
    m_o = bf(mlstm_w_in[:, :, 2 * hk + hv:2 * hk + 2 * hv])
    m_gates = bf(jnp.pad(mlstm_w_in[:, :, 2 * hk + 2 * hv:], ((0, 0), (0, 0), (0, LANES - 2 * MLSTM_HEADS))))
    f_q = bf(fox_w_qg[:, :, :d] * (FOX_HEAD_DIM ** -0.5 * LOG2E))
    f_g = bf(fox_w_qg[:, :, d:])
    out_w = [bf(mlstm_w_out), bf(fox_w_out)]
    up_w, down_w = bf(ffn_w_up), bf(ffn_w_down)
    seq = lambda a: a.reshape(b, s, a.shape[-1])
    for layer in range(depth):
        if layer < n_a:
            weights = [m_q[layer], m_k[layer], m_vt[layer], m_o[layer], m_gates[layer]]
            q, k, vt, o, gates = _norm_proj(xt, norm_mix_pre[layer], weights,
                                            [_BF16, _BF16, _BF16, _BF16, _F32], "mlstm_in_proj",
                                            transposed=(False, False, True, False, False))
            hs = _mlstm(seq(q), seq(k), vt, seq(o), seq(gates), mlstm_b_gate[layer], mlstm_norm[layer])
            mixed, w_out = hs.reshape(b * s, hv), out_w[0][layer]
        else:
            j = layer - n_a
            q, g = _norm_proj(xt, norm_mix_pre[layer], [f_q[j], f_g[j]], [_BF16, _BF16], "fox_in_proj")
            att = _fox_attention(seq(q), cq_sh, seq(g), k_sh, ck_sh, vt_sh)
            mixed, w_out = att.reshape(b * s, d), out_w[1][j]
        xt = _mix_ffn(mixed, xt, w_out, norm_mix_post[layer], norm_ffn_pre[layer], up_w[layer],
                      ffn_conv_w[layer], ffn_conv_b[layer], down_w[layer], norm_ffn_post[layer], s)
        if layer == n_a - 1:
            weights = [bf(kv_w[:, :d]), bf(kv_w[:, d:2 * d].T), bf(_pad_cols(kv_w[:, 2 * d:], LANES))]
            k2, vt_sh, f_pre = _norm_proj(xt, kv_norm, weights, [_BF16, _BF16, _F32], "kv_proj",
                                          transposed=(False, True, False))
            k_sh = k2.reshape(b, s, d)
            cq_sh, ck_sh = _fox_decay(f_pre.reshape(b, s, LANES), kv_b_f)
    return xt.reshape(b, s, d)
```

```python
import functools
import math

import jax
import jax.numpy as jnp
from jax import lax
from jax.experimental import pallas as pl
from jax.experimental.pallas import tpu as pltpu

EPS = 1e-6
MLSTM_HEADS = 8
MLSTM_QK_DIM = 64
MLSTM_V_DIM = 128
FOX_HEADS = 16
FOX_HEAD_DIM = 64
CONV_WIDTH = 3

LANES = 128
SUBLANES = 8
VMEM_LIMIT_BYTES = 56 * 1024 * 1024
ROW_TILE = 512
PROJ_TILE = 1024
MLSTM_CHUNK = 128
MLSTM_ROWS = 2
ATTN_TILE = 2048
ATTN_KEYS = 256
FFN_CHUNK = 256
DECAY_PARTS = 3
LOG2E = math.log2(math.e)

_BF16 = jnp.bfloat16
_F32 = jnp.float32


def _params(*semantics):
    return pltpu.CompilerParams(dimension_semantics=semantics, vmem_limit_bytes=VMEM_LIMIT_BYTES)


def _resident(shape):
    return pl.BlockSpec(shape, lambda *_: (0,) * len(shape), pipeline_mode=pl.Buffered(1))


def _rms(x, gain):
    return x * lax.rsqrt(jnp.mean(x * x, axis=-1, keepdims=True) + EPS) * gain


def _norm_proj_kernel(x_ref, g_ref, *refs, transposed):
    n_out = len(transposed)
    w_refs, o_refs = refs[:n_out], refs[n_out:]
    h = _rms(x_ref[...], g_ref[...]).astype(_BF16)
    for w_ref, o_ref, tr in zip(w_refs, o_refs, transposed):
        n = w_ref.shape[0] if tr is True else w_ref.shape[1]
        for c0 in range(0, n, 512):
            c1 = min(c0 + 512, n)
            if tr is True:
                y = lax.dot_general(w_ref[c0:c1, :], h, (((1,), (1,)), ((), ())),
                                    preferred_element_type=_F32)
                o_ref[c0:c1, :] = y.astype(o_ref.dtype)
            else:
                y = jnp.dot(h, w_ref[:, c0:c1], preferred_element_type=_F32).astype(o_ref.dtype)
                if tr == "pairs":
                    for p in range((c1 - c0) // LANES):
                        o_ref[c0 // LANES + p] = y[:, p * LANES:(p + 1) * LANES]
                else:
                    o_ref[:, c0:c1] = y


def _norm_proj(x, gain, weights, out_dtypes, name, transposed=None):
    t, d = x.shape
    tm = min(PROJ_TILE, t)
    transposed = tuple(transposed or (False,) * len(weights))
    in_specs = [pl.BlockSpec((tm, d), lambda i: (i, 0)), _resident((1, d))]
    in_specs += [_resident(w.shape) for w in weights]
    out_specs, out_shape = [], []
    for w, dt, tr in zip(weights, out_dtypes, transposed):
        if tr is True:
            out_specs.append(pl.BlockSpec((w.shape[0], tm), lambda i: (0, i)))
            out_shape.append(jax.ShapeDtypeStruct((w.shape[0], t), dt))
        elif tr == "pairs":
            groups = w.shape[1] // LANES
            out_specs.append(pl.BlockSpec((groups, tm, LANES), lambda i: (0, i, 0)))
            out_shape.append(jax.ShapeDtypeStruct((groups, t, LANES), dt))
        else:
            out_specs.append(pl.BlockSpec((tm, w.shape[1]), lambda i: (i, 0)))
            out_shape.append(jax.ShapeDtypeStruct((t, w.shape[1]), dt))
    return pl.pallas_call(
        functools.partial(_norm_proj_kernel, transposed=transposed),
        grid=(t // tm,), in_specs=in_specs, out_specs=out_specs, out_shape=out_shape,
        compiler_params=_params("parallel"), name=name,
    )(x, gain.reshape(1, d), *weights)


def _log_sigmoid(z):
    return jnp.minimum(z, 0.0) - jnp.log1p(jnp.exp(-jnp.abs(z)))


ONES_ROWS = 16


def _mlstm_kernel(q_ref, k_ref, o_ref, gc_ref, gr_ref, bc_ref, br_ref, hn_ref, *rest, rows):
    nh, dk, dv = MLSTM_HEADS, MLSTM_QK_DIM, MLSTM_V_DIM
    L = q_ref.shape[1]
    vt_refs, (out_ref, ct_ref, m_ref) = rest[:rows], rest[rows:]

    @pl.when(pl.program_id(1) == 0)
    def _():
        ct_ref[...] = jnp.zeros_like(ct_ref)
        m_ref[...] = jnp.zeros_like(m_ref)

    row = lax.broadcasted_iota(jnp.int32, (L, L), 0)
    col = lax.broadcasted_iota(jnp.int32, (L, L), 1)
    source_before_target = row <= col
    tril = (row >= col).astype(_F32)
    triu = source_before_target.astype(_F32)

    heads = [(bb, h) for bb in range(rows) for h in range(nh)]
    lane_qk = lax.broadcasted_iota(jnp.int32, (L, 2 * dk), 1)
    ones_blk = jnp.ones((ONES_ROWS, L), _BF16)
    first_half = lane_qk < dk

    gates = {}
    for bb in range(rows):
        gc = gc_ref[bb] + bc_ref[...]
        gr = gr_ref[bb] + br_ref[:, 0:1]
        lane_g = lax.broadcasted_iota(jnp.int32, gc.shape, 1)
        sub_g = lax.broadcasted_iota(jnp.int32, gr.shape, 0)
        lf_c = jnp.where(lane_g >= nh, _log_sigmoid(gc), 0.0)
        lf_r = jnp.where(sub_g >= nh, _log_sigmoid(gr), 0.0)
        b_c_all = jnp.dot(tril, lf_c, preferred_element_type=_F32, precision=lax.Precision.HIGHEST)
        b_r_all = jnp.dot(lf_r, triu, preferred_element_type=_F32, precision=lax.Precision.HIGHEST)
        gates[bb] = (gc, gr, b_c_all, b_r_all)

    st = {}
    for bb, h in heads:
        pair = slice((h // 2) * 2 * dk, (h // 2 + 1) * 2 * dk)
        qp, kp = q_ref[bb, :, pair], k_ref[bb, :, pair]
        qm = jnp.where(first_half if h % 2 == 0 else ~first_half, qp, jnp.zeros_like(qp))
        ct = ct_ref[bb * nh + h]
        both = lax.dot_general(jnp.concatenate([kp, ct.astype(_BF16)], axis=0), qm,
                               (((1,), (1,)), ((), ())), preferred_element_type=_F32)
        st[bb, h] = dict(kp=kp, ct=ct, s_qk=both[:L], inter=both[L:])

    for bb, h in heads:
        gc, gr, b_c_all, b_r_all = gates[bb]
        e = st[bb, h]
        g_col = gc[:, h:h + 1] - b_c_all[:, nh + h:nh + h + 1]
        b_row = b_r_all[nh + h:nh + h + 1, :]
        i_row = gr[h:h + 1, :]
        m_prev = m_ref[bb * nh + h:bb * nh + h + 1, 0:1]
        dmat = jnp.where(source_before_target, g_col + b_row, -jnp.inf)
        m_inter = b_row + m_prev
        m_t = jnp.maximum(m_inter, jnp.max(dmat, axis=0, keepdims=True))
        sd = (e["s_qk"] * jnp.exp(dmat - m_t)).astype(_BF16)
        vt_aug = jnp.concatenate([vt_refs[bb][h * dv:(h + 1) * dv, :], ones_blk], axis=0)
        tot = jnp.dot(vt_aug, sd, preferred_element_type=_F32) + jnp.exp(m_inter - m_t) * e["inter"]
        num, den = tot[:dv], tot[dv:dv + 1]
        hv = num / jnp.maximum(jnp.abs(den), jnp.exp(-m_t))
        hv = hv * lax.rsqrt(jnp.mean(hv * hv, axis=0, keepdims=True) + EPS)
        hv = (hv * hn_ref[h * dv:(h + 1) * dv, :]).T
        hv = hv * jax.nn.sigmoid(o_ref[bb, :, h * dv:(h + 1) * dv].astype(_F32))
        out_ref[bb, :, h * dv:(h + 1) * dv] = hv.astype(out_ref.dtype)
        e.update(vt_aug=vt_aug, b_row=b_row, i_row=i_row, m_prev=m_prev)

    for bb, h in heads:
        e = st[bb, h]
        b_row, m_prev = e["b_row"], e["m_prev"]
        b_last = b_row[:, L - 1:L]
        g_row = b_last - b_row + e["i_row"]
        m_new = jnp.maximum(b_last + m_prev, jnp.max(g_row, axis=-1, keepdims=True))
        decay = jnp.exp(b_last + m_prev - m_new)
        vw = (e["vt_aug"].astype(_F32) * jnp.exp(g_row - m_new)).astype(_BF16)
        ct_ref[bb * nh + h] = decay * e["ct"] + jnp.dot(vw, e["kp"], preferred_element_type=_F32)
        m_ref[bb * nh + h:bb * nh + h + 1, :] = jnp.broadcast_to(m_new, (1, m_ref.shape[1]))


def _mlstm(q, k, vt, o, gates, b_gate, head_norm):
    b, s, _ = q.shape
    nh, dk, dv = MLSTM_HEADS, MLSTM_QK_DIM, MLSTM_V_DIM
    L = min(MLSTM_CHUNK, s)
    chunks = s // L
    gates_row = jnp.swapaxes(gates[:, :, :2 * nh], 1, 2)
    bias_col = jnp.zeros((1, LANES), _F32).at[0, :2 * nh].set(b_gate)
    bias_row = b_gate.reshape(2 * nh, 1)
    norm_cols = jnp.broadcast_to(head_norm.astype(_F32)[:, None], (nh * dv, L))
    rows = MLSTM_ROWS if b % MLSTM_ROWS == 0 else 1
    tok = lambda width: pl.BlockSpec((rows, L, width), lambda bi, ci: (bi, ci, 0))
    vt_spec = lambda r: pl.BlockSpec((nh * dv, L), lambda bi, ci: (0, (bi * rows + r) * chunks + ci))
    return pl.pallas_call(
        functools.partial(_mlstm_kernel, rows=rows),
        grid=(b // rows, chunks),
        in_specs=[tok(nh * dk), tok(nh * dk), tok(nh * dv), tok(LANES),
                  pl.BlockSpec((rows, 2 * nh, L), lambda bi, ci: (bi, 0, ci)),
                  _resident((1, LANES)), _resident((2 * nh, 1)), _resident((nh * dv, L))]
                 + [vt_spec(r) for r in range(rows)],
        out_specs=tok(nh * dv),
        out_shape=jax.ShapeDtypeStruct((b, s, nh * dv), _BF16),
        scratch_shapes=[pltpu.VMEM((rows * nh, dv + ONES_ROWS, 2 * dk), _F32),
                        pltpu.VMEM((rows * nh, LANES), _F32)],
        compiler_params=_params("parallel", "arbitrary"), name="mlstm_chunk",
    )(q, k, o, gates, gates_row, bias_col, bias_row, norm_cols, *([vt] * rows))


GELU_C = math.sqrt(2.0 / math.pi)
RUN_PITCH = 72


def _mix_ffn_kernel(a0_ref, x0_ref, a_ref, x_ref, wo_ref, gmix_ref, gpre_ref, wup_ref, cw_ref, cb_ref,
                    wdn_ref, gpost_ref, o_ref,
                    h_ref, xp_ref, y_ref, carry_ref, slab_in_ref, slab_out_ref, hid_ref, *, tiles_per_seq):
    tm, d = x_ref.shape
    f = wdn_ref.shape[0]
    run = tm // SUBLANES
    n_slabs = d // LANES
    step = pl.program_id(0)

    n_slices = SUBLANES
    rows_of = lambda i: slice(i * run, (i + 1) * run)

    def mix_rows(mixed, x_blk_ref, r, zero):
        x1 = x_blk_ref[rows_of(r), :] + _rms(mixed[rows_of(r)] + zero, gmix_ref[...])
        for c in range(n_slabs):
            slab_in_ref[c, r * RUN_PITCH:r * RUN_PITCH + run, :] = x1[:, c * LANES:(c + 1) * LANES]

    def gather_slice(j, slot, zero):
        xp = jnp.concatenate(
            [jnp.concatenate([slab_in_ref[c, pl.ds(k, SUBLANES, stride=RUN_PITCH), :]
                              for k in range(j * SUBLANES, (j + 1) * SUBLANES)], axis=0)
             for c in range(n_slabs)], axis=1) + zero
        xp_ref[slot, rows_of(j), :] = xp
        h_ref[slot, rows_of(j), :] = _rms(xp, gpre_ref[...]).astype(_BF16)

    def finish_slice(j, slot, zero):
        out = xp_ref[slot, rows_of(j), :] + _rms(y_ref[slot, rows_of(j), :] + zero, gpost_ref[...])
        for c in range(n_slabs):
            for kk in range(SUBLANES):
                slab_out_ref[c, pl.ds(j * SUBLANES + kk, SUBLANES, stride=RUN_PITCH), :] = (
                    out[kk * SUBLANES:(kk + 1) * SUBLANES, c * LANES:(c + 1) * LANES])

    def zero_after(value):
        bits = pltpu.bitcast(value[:SUBLANES, :LANES], jnp.uint32)
        bits = lax.shift_right_logical(lax.shift_right_logical(bits, jnp.uint32(16)), jnp.uint32(16))
        return jnp.concatenate([pltpu.bitcast(bits, _F32)[0:1, :]] * n_slabs, axis=1)

    no_wait = jnp.zeros((1, d), _F32)

    @pl.when(step == 0)
    def _():
        y_ref[...] = jnp.zeros_like(y_ref)
        xp_ref[...] = jnp.zeros_like(xp_ref)
        mixed0 = jnp.dot(a0_ref[...], wo_ref[...], preferred_element_type=_F32)
        for r in range(n_slices):
            mix_rows(mixed0, x0_ref, r, no_wait)
        for j in range(n_slices):
            gather_slice(j, 0, no_wait)

    @pl.when(step % tiles_per_seq == 0)
    def _():
        carry_ref[...] = jnp.zeros_like(carry_ref)

    cur = step % 2
    other = (step + 1) % 2

    mixed = jnp.dot(a_ref[...], wo_ref[...], preferred_element_type=_F32)

    h = h_ref[cur]
    first_sublane = lax.broadcasted_iota(jnp.int32, (SUBLANES, FFN_CHUNK), 0) == 0

    def conv(c0, width):
        cols = slice(c0, c0 + width)
        u = jnp.dot(h, wup_ref[:, cols], preferred_element_type=_F32)
        prev = carry_ref[:, cols]
        tail1, tail2 = u[tm - SUBLANES:], u[tm - 2 * SUBLANES:tm - SUBLANES]
        carry_ref[:, cols] = u[tm - 2 * SUBLANES:]
        mask = first_sublane[:, :width]
        wrap1 = jnp.where(mask, prev[2 * SUBLANES - 1:, :], pltpu.roll(tail1, 1, axis=0))
        wrap2 = jnp.where(mask, prev[SUBLANES - 1:SUBLANES, :], pltpu.roll(tail2, 1, axis=0))
        u1 = jnp.concatenate([wrap1, u[:tm - SUBLANES]], axis=0)
        u2 = jnp.concatenate([wrap2, wrap1, u[:tm - 2 * SUBLANES]], axis=0)
        return (cb_ref[:, cols] + cw_ref[2:3, cols] * u + cw_ref[1:2, cols] * u1
                + cw_ref[0:1, cols] * u2), zero_after(u)

    pieces = ([functools.partial(piece, j) for j in range(n_slices)
               for piece in (functools.partial(finish_slice, slot=other),
                             functools.partial(mix_rows, mixed, x_ref))]
              + [functools.partial(gather_slice, j, other) for j in range(n_slices)])
    chunks = list(range(0, f, FFN_CHUNK))
    last_tie = 2 * len(chunks) - 1
    for i, c0 in enumerate(chunks):
        width = min(FFN_CHUNK, f - c0)
        gate, after_gate = conv(c0, width)
        half_val, after_val = conv(f + c0, width)
        inner = gate * ((gate * gate) * (GELU_C * 0.044715) + GELU_C)
        w = gate * half_val
        hid_ref[:, c0:c0 + width] = (w * jnp.tanh(inner) + w).astype(_BF16)
        for tie, zero in ((2 * i, after_gate), (2 * i + 1, after_val)):
            for t, piece in enumerate(pieces):
                if min(t, last_tie) == tie:
                    piece(zero=zero)

    y_ref[cur] = jnp.dot(hid_ref[...], wdn_ref[...], preferred_element_type=_F32)
    for c in range(n_slabs):
        for r in range(SUBLANES):
            o_ref[r * run:(r + 1) * run, c * LANES:(c + 1) * LANES] = (
                slab_out_ref[c, r * RUN_PITCH:r * RUN_PITCH + run, :])


def _mix_ffn(a, x, w_out, g_mix, g_pre, w_up, conv_w, conv_b, w_down, g_post, seq_len):
    t, d = x.shape
    k = a.shape[1]
    f = w_down.shape[0]
    tm = min(ROW_TILE, seq_len)
    n = t // tm
    half = jnp.concatenate([jnp.ones((f,), _F32), jnp.full((f,), 0.5, _F32)])
    first = lambda width: pl.BlockSpec((tm, width), lambda i: (0, 0), pipeline_mode=pl.Buffered(1))
    ahead = lambda width: pl.BlockSpec((tm, width), lambda i: (jnp.minimum(i + 1, n - 1), 0))
    row = lambda width: _resident((1, width))
    return pl.pallas_call(
        functools.partial(_mix_ffn_kernel, tiles_per_seq=seq_len // tm),
        grid=(n + 1,),
        in_specs=[first(k), first(d), ahead(k), ahead(d), _resident((k, d)), row(d), row(d),
                  _resident((d, 2 * f)), _resident((CONV_WIDTH, 2 * f)), row(2 * f), _resident((f, d)),
                  row(d)],
        out_specs=pl.BlockSpec((tm, d), lambda i: (jnp.maximum(i - 1, 0), 0)),
        out_shape=jax.ShapeDtypeStruct((t, d), _F32),
        scratch_shapes=[pltpu.VMEM((2, tm, d), _BF16), pltpu.VMEM((2, tm, d), _F32),
                        pltpu.VMEM((2, tm, d), _F32), pltpu.VMEM((2 * SUBLANES, 2 * f), _F32),
                        pltpu.VMEM((d // LANES, SUBLANES * RUN_PITCH, LANES), _F32),
                        pltpu.VMEM((d // LANES, SUBLANES * RUN_PITCH, LANES), _F32),
                        pltpu.VMEM((tm, f), _BF16)],
        compiler_params=_params("arbitrary"), name="mix_ffn",
    )(a, x, a, x, w_out, g_mix.reshape(1, d), g_pre.reshape(1, d), w_up, conv_w * half,
      (conv_b * half).reshape(1, 2 * f), w_down, g_post.reshape(1, d))


def _decay_kernel(f_ref, bias_ref, place_ref, cq_ref, ck_ref, carry_ref):
    tm = f_ref.shape[1]

    @pl.when(pl.program_id(1) == 0)
    def _():
        carry_ref[...] = jnp.zeros_like(carry_ref)

    lane = lax.broadcasted_iota(jnp.int32, (tm, LANES), 1)
    row = lax.broadcasted_iota(jnp.int32, (tm, LANES), 0)
    c = jnp.where(lane < FOX_HEADS, _log_sigmoid(f_ref[0] + bias_ref[...]), 0.0)
    shift = 1
    while shift < tm:
        c = c + jnp.where(row >= shift, pltpu.roll(c, shift, axis=0), 0.0)
        shift *= 2
    c = carry_ref[0:1, :] + c
    carry_ref[...] = jnp.broadcast_to(c[tm - 1:tm, :], carry_ref.shape)
    c = c * LOG2E

    packed = jnp.where(lane < FOX_HEADS, 1.0, 0.0)
    packed = pltpu.roll(packed, DECAY_PARTS * FOX_HEADS, axis=1)
    rest = c
    for j in range(DECAY_PARTS):
        part = rest.astype(_BF16).astype(_F32)
        rest = rest - part
        packed = packed + (part if j == 0 else pltpu.roll(part, j * FOX_HEADS, axis=1))
    placed = jnp.dot(packed.astype(_BF16), place_ref[...], preferred_element_type=_F32)
    pairs = cq_ref.shape[0]
    for p in range(pairs):
        cq_ref[p, 0] = placed[:, p * LANES:(p + 1) * LANES].astype(cq_ref.dtype)
        ck_ref[p, 0] = placed[:, (pairs + p) * LANES:(pairs + p + 1) * LANES].astype(ck_ref.dtype)


def _decay_placement():
    nh, dh, parts = FOX_HEADS, FOX_HEAD_DIM, DECAY_PARTS
    width = nh * dh
    heads = jnp.arange(nh)
    base = (heads // 2) * 2 * dh + (1 - heads % 2) * dh
    ones_row = parts * nh + heads
    place = jnp.zeros((LANES, 2 * width), _F32)
    for j in range(parts):
        place = place.at[j * nh + heads, base + j].set(1.0)
        place = place.at[ones_row, base + parts + j].set(1.0)
        place = place.at[ones_row, width + base + j].set(1.0)
        place = place.at[j * nh + heads, width + base + parts + j].set(-1.0)
    return place.astype(_BF16)


def _fox_decay(f_pre, bias):
    b, s, _ = f_pre.shape
    nh, dh = FOX_HEADS, FOX_HEAD_DIM
    tm = min(ROW_TILE, s)
    bias_row = jnp.zeros((1, LANES), _F32).at[0, :nh].set(bias)
    pairs = nh * dh // LANES
    out = jax.ShapeDtypeStruct((pairs, b, s, LANES), _BF16)
    return pl.pallas_call(
        _decay_kernel,
        grid=(b, s // tm),
        in_specs=[pl.BlockSpec((1, tm, LANES), lambda bi, ti: (bi, ti, 0)), _resident((1, LANES)),
                  _resident((LANES, 2 * nh * dh))],
        out_specs=[pl.BlockSpec((pairs, 1, tm, LANES), lambda bi, ti: (0, bi, ti, 0))] * 2,
        out_shape=[out, out],
        scratch_shapes=[pltpu.VMEM((SUBLANES, LANES), _F32)],
        compiler_params=_params("parallel", "arbitrary"), name="fox_decay",
    )(f_pre, bias_row, _decay_placement())


def _attn_kernel(q_ref, cq_ref, g_ref, k_ref, ck_ref, vt_ref, o_ref,
                 sa_ref, sb_ref, sa_max_ref, sb_max_ref, m_ref, acc_ref):
    tq = q_ref.shape[2]
    tk = sa_ref.shape[1]
    dh = FOX_HEAD_DIM
    qi = pl.program_id(2)
    first_q = lax.broadcasted_iota(jnp.int32, (tq, 2 * dh), 1) < dh
    first_k = lax.broadcasted_iota(jnp.int32, (tk, 2 * dh), 1) < dh
    q_pair, cq_pair = q_ref[0, 0], cq_ref[0, 0]
    q_aug = (jnp.where(first_q, q_pair, cq_pair), jnp.where(first_q, cq_pair, q_pair))

    ones_blk = jnp.ones((ONES_ROWS, tk), _BF16)

    m_ref[...] = jnp.full_like(m_ref, -jnp.inf)
    acc_ref[...] = jnp.zeros_like(acc_ref)

    def keys(j):
        start = pl.multiple_of(j * tk, tk)
        k_pair = k_ref[0, 0, pl.ds(start, tk), :]
        ck_pair = ck_ref[0, 0, pl.ds(start, tk), :]
        return jnp.where(first_k, k_pair, ck_pair), jnp.where(first_k, ck_pair, k_pair)

    def scores(k_aug, bufs, c, hh, diagonal):
        s_ref, smax_ref = bufs
        cols = slice(c * tk, (c + 1) * tk)
        s = lax.dot_general(k_aug[hh], q_aug[hh][cols], (((1,), (1,)), ((), ())),
                            preferred_element_type=_F32)
        if diagonal:
            key = lax.broadcasted_iota(jnp.int32, (tk, tk), 0)
            qry = lax.broadcasted_iota(jnp.int32, (tk, tk), 1)
            s = jnp.where(key <= qry, s, -jnp.inf)
        s_ref[hh, :, cols] = s
        smax_ref[hh, :, cols] = jnp.max(s, axis=0, keepdims=True)

    def accumulate(j, bufs, c, hh):
        s_ref, smax_ref = bufs
        start = pl.multiple_of(j * tk, tk)
        cols = slice(c * tk, (c + 1) * tk)
        m_old = m_ref[hh, :, cols]
        m_new = jnp.maximum(m_old, smax_ref[hh, :, cols])
        alpha = jnp.exp2(m_old - m_new)
        p = jnp.exp2(s_ref[hh, :, cols] - m_new).astype(_BF16)
        vt = jnp.concatenate([vt_ref[hh * dh:(hh + 1) * dh, pl.ds(start, tk)], ones_blk], axis=0)
        acc_ref[hh, :, cols] = alpha * acc_ref[hh, :, cols] + jnp.dot(vt, p, preferred_element_type=_F32)
        m_ref[hh, :, cols] = m_new

    per_tile = tq // tk
    buf_a, buf_b = (sa_ref, sa_max_ref), (sb_ref, sb_max_ref)
    pieces = [(hh, c) for hh in range(2) for c in range(per_tile)]

    first_keys = keys(0)
    for hh, c in pieces:
        scores(first_keys, buf_a, c, hh, False)

    def body(i, carry):
        for u in range(per_tile):
            j = i * per_tile + u
            cur, nxt = (buf_a, buf_b) if u % 2 == 0 else (buf_b, buf_a)
            next_keys = keys(j + 1)
            for hh, c in pieces:
                scores(next_keys, nxt, c, hh, False)
                accumulate(j, cur, c, hh)
        return carry

    lax.fori_loop(0, qi, body, 0)
    for hh in range(2):
        key = lax.broadcasted_iota(jnp.int32, (tk, tk), 0)
        qry = lax.broadcasted_iota(jnp.int32, (tk, tk), 1)
        square = jnp.where(key <= qry, sa_ref[hh, :, :tk], -jnp.inf)
        sa_ref[hh, :, :tk] = square
        sa_max_ref[hh, :, :tk] = jnp.max(square, axis=0, keepdims=True)
    for u in range(per_tile):
        j = qi * per_tile + u
        cur, nxt = (buf_a, buf_b) if u % 2 == 0 else (buf_b, buf_a)
        next_keys = keys(j + 1) if u + 1 < per_tile else None
        for hh in range(2):
            accumulate(j, cur, u, hh)
            for c in range(u + 1, per_tile):
                scores(next_keys, nxt, c, hh, c == u + 1)
                accumulate(j, cur, c, hh)

    halves = []
    for hh in range(2):
        acc = acc_ref[hh]
        halves.append(acc[0:dh, :] / acc[dh:dh + 1, :])
    out = jnp.concatenate(halves, axis=0).T
    o_ref[0] = (out * jax.nn.sigmoid(g_ref[0, 0].astype(_F32))).astype(o_ref.dtype)


def _fox_attention(q, cq, g, k, ck, vt):
    pairs, b, s, _ = q.shape
    hd = pairs * LANES
    tq = min(ATTN_TILE, s)
    tk = min(ATTN_KEYS, tq // 2)
    rows = FOX_HEAD_DIM + ONES_ROWS
    qspec = pl.BlockSpec((1, 1, tq, LANES), lambda bi, pi, qi: (pi, bi, qi, 0))
    kspec = pl.BlockSpec((1, 1, s, LANES), lambda bi, pi, qi: (pi, bi, 0, 0))
    ospec = pl.BlockSpec((1, tq, LANES), lambda bi, pi, qi: (bi, qi, pi))
    vspec = pl.BlockSpec((LANES, s), lambda bi, pi, qi: (pi, bi))
    return pl.pallas_call(
        _attn_kernel,
        grid=(b, pairs, s // tq),
        in_specs=[qspec, qspec, qspec, kspec, kspec, vspec],
        out_specs=ospec,
        out_shape=jax.ShapeDtypeStruct((b, s, hd), _BF16),
        scratch_shapes=[pltpu.VMEM((2, tk, tq), _F32), pltpu.VMEM((2, tk, tq), _F32),
                        pltpu.VMEM((2, 1, tq), _F32), pltpu.VMEM((2, 1, tq), _F32),
                        pltpu.VMEM((2, 1, tq), _F32), pltpu.VMEM((2, rows, tq), _F32)],
        compiler_params=_params("parallel", "parallel", "arbitrary"), name="fox_attention",
    )(q, cq, g, k, ck, vt)


def _pad_cols(w, width):
    return jnp.pad(w, ((0, 0), (0, width - w.shape[1])))


def kernel(x, norm_mix_pre, norm_mix_post, norm_ffn_pre, norm_ffn_post, mlstm_w_in, mlstm_b_gate,
           mlstm_norm, mlstm_w_out, kv_norm, kv_w, kv_b_f, fox_w_qg, fox_w_out, ffn_w_up,
           ffn_conv_w, ffn_conv_b, ffn_w_down):
    b, s, d = x.shape
    depth = norm_mix_pre.shape[0]
    n_a = mlstm_w_in.shape[0]
    xt = x.reshape(b * s, d)
    hk = MLSTM_HEADS * MLSTM_QK_DIM
    hv = MLSTM_HEADS * MLSTM_V_DIM
    k_sh = vt_sh = cq_sh = ck_sh = None
    bf = lambda w: w.astype(_BF16)
    m_q = bf(mlstm_w_in[:, :, :hk] * MLSTM_QK_DIM ** -0.5)
    m_k = bf(mlstm_w_in[:, :, hk:2 * hk])
    m_vt = bf(jnp.swapaxes(mlstm_w_in[:, :, 2 * hk:2 * hk + hv], 1, 2))
    m_o = bf(mlstm_w_in[:, :, 2 * hk + hv:2 * hk + 2 * hv])
    m_gates = bf(jnp.pad(mlstm_w_in[:, :, 2 * hk + 2 * hv:], ((0, 0), (0, 0), (0, LANES - 2 * MLSTM_HEADS))))
    f_q = bf(fox_w_qg[:, :, :d] * (FOX_HEAD_DIM ** -0.5 * LOG2E))
    f_g = bf(fox_w_qg[:, :, d:])
    out_w = [bf(mlstm_w_out), bf(fox_w_out)]
    up_w, down_w = bf(ffn_w_up), bf(ffn_w_down)
    seq = lambda a: a.reshape(b, s, a.shape[-1])
    for layer in range(depth):
        if layer < n_a:
            weights = [m_q[layer], m_k[layer], m_vt[layer], m_o[layer], m_gates[layer]]
            q, k, vt, o, gates = _norm_proj(xt, norm_mix_pre[layer], weights,
                                            [_BF16, _BF16, _BF16, _BF16, _F32], "mlstm_in_proj",
                                            transposed=(False, False, True, False, False))
            hs = _mlstm(seq(q), seq(k), vt, seq(o), seq(gates), mlstm_b_gate[layer], mlstm_norm[layer])
            mixed, w_out = hs.reshape(b * s, hv), out_w[0][layer]
        else:
            j = layer - n_a
            q, g = _norm_proj(xt, norm_mix_pre[layer], [f_q[j], f_g[j]], [_BF16, _BF16], "fox_in_proj",
                              transposed=("pairs", "pairs"))
            slabs = lambda a: a.reshape(a.shape[0], b, s, LANES)
            att = _fox_attention(slabs(q), cq_sh, slabs(g), k_sh, ck_sh, vt_sh)
            mixed, w_out = att.reshape(b * s, d), out_w[1][j]
        xt = _mix_ffn(mixed, xt, w_out, norm_mix_post[layer], norm_ffn_pre[layer], up_w[layer],
                      ffn_conv_w[layer], ffn_conv_b[layer], down_w[layer], norm_ffn_post[layer], s)
        if layer == n_a - 1:
            weights = [bf(kv_w[:, :d]), bf(kv_w[:, d:2 * d].T), bf(_pad_cols(kv_w[:, 2 * d:], LANES))]
            k2, vt_sh, f_pre = _norm_proj(xt, kv_norm, weights, [_BF16, _BF16, _F32], "kv_proj",
                                          transposed=("pairs", True, False))
            k_sh = k2.reshape(k2.shape[0], b, s, LANES)
            cq_sh, ck_sh = _fox_decay(f_pre.reshape(b, s, LANES), kv_b_f)
    return xt.reshape(b, s, d)
```

```python
import functools
import math

import jax
import jax.numpy as jnp
from jax import lax
from jax.experimental import pallas as pl
from jax.experimental.pallas import tpu as pltpu

EPS = 1e-6
MLSTM_HEADS = 8
MLSTM_QK_DIM = 64
MLSTM_V_DIM = 128
FOX_HEADS = 16
FOX_HEAD_DIM = 64
CONV_WIDTH = 3

LANES = 128
SUBLANES = 8
VMEM_LIMIT_BYTES = 56 * 1024 * 1024
ROW_TILE = 512
PROJ_TILE = 1024
MLSTM_CHUNK = 128
MLSTM_ROWS = 2
ATTN_TILE = 2048
ATTN_KEYS = 256
FFN_CHUNK = 256
DECAY_PARTS = 3
LOG2E = math.log2(math.e)

_BF16 = jnp.bfloat16
_F32 = jnp.float32


def _params(*semantics):
    return pltpu.CompilerParams(dimension_semantics=semantics, vmem_limit_bytes=VMEM_LIMIT_BYTES)


def _resident(shape):
    return pl.BlockSpec(shape, lambda *_: (0,) * len(shape), pipeline_mode=pl.Buffered(1))


def _rms(x, gain):
    return x * lax.rsqrt(jnp.mean(x * x, axis=-1, keepdims=True) + EPS) * gain


def _norm_proj_kernel(x_ref, g_ref, *refs, transposed):
    n_out = len(transposed)
    w_refs, o_refs = refs[:n_out], refs[n_out:]
    h = _rms(x_ref[...], g_ref[...]).astype(_BF16)
    for w_ref, o_ref, tr in zip(w_refs, o_refs, transposed):
        n = w_ref.shape[0] if tr else w_ref.shape[1]
        for c0 in range(0, n, 512):
            c1 = min(c0 + 512, n)
            if tr:
                y = lax.dot_general(w_ref[c0:c1, :], h, (((1,), (1,)), ((), ())),
                                    preferred_element_type=_F32)
                o_ref[c0:c1, :] = y.astype(o_ref.dtype)
            else:
                y = jnp.dot(h, w_ref[:, c0:c1], preferred_element_type=_F32)
                o_ref[:, c0:c1] = y.astype(o_ref.dtype)


def _norm_proj(x, gain, weights, out_dtypes, name, transposed=None):
    t, d = x.shape
    tm = min(PROJ_TILE, t)
    transposed = tuple(transposed or (False,) * len(weights))
    in_specs = [pl.BlockSpec((tm, d), lambda i: (i, 0)), _resident((1, d))]
    in_specs += [_resident(w.shape) for w in weights]
    out_specs, out_shape = [], []
    for w, dt, tr in zip(weights, out_dtypes, transposed):
        if tr:
            out_specs.append(pl.BlockSpec((w.shape[0], tm), lambda i: (0, i)))
            out_shape.append(jax.ShapeDtypeStruct((w.shape[0], t), dt))
        else:
            out_specs.append(pl.BlockSpec((tm, w.shape[1]), lambda i: (i, 0)))
            out_shape.append(jax.ShapeDtypeStruct((t, w.shape[1]), dt))
    return pl.pallas_call(
        functools.partial(_norm_proj_kernel, transposed=transposed),
        grid=(t // tm,), in_specs=in_specs, out_specs=out_specs, out_shape=out_shape,
        compiler_params=_params("parallel"), name=name,
    )(x, gain.reshape(1, d), *weights)


def _log_sigmoid(z):
    return jnp.minimum(z, 0.0) - jnp.log1p(jnp.exp(-jnp.abs(z)))


ONES_ROWS = 16


def _mlstm_kernel(q_ref, k_ref, o_ref, gc_ref, gr_ref, bc_ref, br_ref, hn_ref, *rest, rows):
    nh, dk, dv = MLSTM_HEADS, MLSTM_QK_DIM, MLSTM_V_DIM
    L = q_ref.shape[1]
    vt_refs, (out_ref, ct_ref, m_ref) = rest[:rows], rest[rows:]

    @pl.when(pl.program_id(1) == 0)
    def _():
        ct_ref[...] = jnp.zeros_like(ct_ref)
        m_ref[...] = jnp.zeros_like(m_ref)

    row = lax.broadcasted_iota(jnp.int32, (L, L), 0)
    col = lax.broadcasted_iota(jnp.int32, (L, L), 1)
    source_before_target = row <= col
    tril = (row >= col).astype(_F32)
    triu = source_before_target.astype(_F32)

    heads = [(bb, h) for bb in range(rows) for h in range(nh)]
    lane_qk = lax.broadcasted_iota(jnp.int32, (L, 2 * dk), 1)
    ones_blk = jnp.ones((ONES_ROWS, L), _BF16)
    first_half = lane_qk < dk

    gates = {}
    for bb in range(rows):
        gc = gc_ref[bb] + bc_ref[...]
        gr = gr_ref[bb] + br_ref[:, 0:1]
        lane_g = lax.broadcasted_iota(jnp.int32, gc.shape, 1)
        sub_g = lax.broadcasted_iota(jnp.int32, gr.shape, 0)
        lf_c = jnp.where(lane_g >= nh, _log_sigmoid(gc), 0.0)
        lf_r = jnp.where(sub_g >= nh, _log_sigmoid(gr), 0.0)
        b_c_all = jnp.dot(tril, lf_c, preferred_element_type=_F32, precision=lax.Precision.HIGHEST)
        b_r_all = jnp.dot(lf_r, triu, preferred_element_type=_F32, precision=lax.Precision.HIGHEST)
        gates[bb] = (gc, gr, b_c_all, b_r_all)

    st = {}
    for bb, h in heads:
        pair = slice((h // 2) * 2 * dk, (h // 2 + 1) * 2 * dk)
        qp, kp = q_ref[bb, :, pair], k_ref[bb, :, pair]
        qm = jnp.where(first_half if h % 2 == 0 else ~first_half, qp, jnp.zeros_like(qp))
        ct = ct_ref[bb * nh + h]
        both = lax.dot_general(jnp.concatenate([kp, ct.astype(_BF16)], axis=0), qm,
                               (((1,), (1,)), ((), ())), preferred_element_type=_F32)
        st[bb, h] = dict(kp=kp, ct=ct, s_qk=both[:L], inter=both[L:])

    for bb, h in heads:
        gc, gr, b_c_all, b_r_all = gates[bb]
        e = st[bb, h]
        g_col = gc[:, h:h + 1] - b_c_all[:, nh + h:nh + h + 1]
        b_row = b_r_all[nh + h:nh + h + 1, :]
        i_row = gr[h:h + 1, :]
        m_prev = m_ref[bb * nh + h:bb * nh + h + 1, 0:1]
        dmat = jnp.where(source_before_target, g_col + b_row, -jnp.inf)
        m_inter = b_row + m_prev
        m_t = jnp.maximum(m_inter, jnp.max(dmat, axis=0, keepdims=True))
        sd = (e["s_qk"] * jnp.exp(dmat - m_t)).astype(_BF16)
        vt_aug = jnp.concatenate([vt_refs[bb][h * dv:(h + 1) * dv, :], ones_blk], axis=0)
        tot = jnp.dot(vt_aug, sd, preferred_element_type=_F32) + jnp.exp(m_inter - m_t) * e["inter"]
        num, den = tot[:dv], tot[dv:dv + 1]
        hv = num / jnp.maximum(jnp.abs(den), jnp.exp(-m_t))
        hv = hv * lax.rsqrt(jnp.mean(hv * hv, axis=0, keepdims=True) + EPS)
        hv = (hv * hn_ref[h * dv:(h + 1) * dv, :]).T
        hv = hv * jax.nn.sigmoid(o_ref[bb, :, h * dv:(h + 1) * dv].astype(_F32))
        out_ref[bb, :, h * dv:(h + 1) * dv] = hv.astype(out_ref.dtype)
        e.update(vt_aug=vt_aug, b_row=b_row, i_row=i_row, m_prev=m_prev)

    for bb, h in heads:
        e = st[bb, h]
        b_row, m_prev = e["b_row"], e["m_prev"]
        b_last = b_row[:, L - 1:L]
        g_row = b_last - b_row + e["i_row"]
        m_new = jnp.maximum(b_last + m_prev, jnp.max(g_row, axis=-1, keepdims=True))
        decay = jnp.exp(b_last + m_prev - m_new)
        vw = (e["vt_aug"].astype(_F32) * jnp.exp(g_row - m_new)).astype(_BF16)
        ct_ref[bb * nh + h] = decay * e["ct"] + jnp.dot(vw, e["kp"], preferred_element_type=_F32)
        m_ref[bb * nh + h:bb * nh + h + 1, :] = jnp.broadcast_to(m_new, (1, m_ref.shape[1]))


def _mlstm(q, k, vt, o, gates, b_gate, head_norm):
    b, s, _ = q.shape
    nh, dk, dv = MLSTM_HEADS, MLSTM_QK_DIM, MLSTM_V_DIM
    L = min(MLSTM_CHUNK, s)
    chunks = s // L
    gates_row = jnp.swapaxes(gates[:, :, :2 * nh], 1, 2)
    bias_col = jnp.zeros((1, LANES), _F32).at[0, :2 * nh].set(b_gate)
    bias_row = b_gate.reshape(2 * nh, 1)
    norm_cols = jnp.broadcast_to(head_norm.astype(_F32)[:, None], (nh * dv, L))
    rows = MLSTM_ROWS if b % MLSTM_ROWS == 0 else 1
    tok = lambda width: pl.BlockSpec((rows, L, width), lambda bi, ci: (bi, ci, 0))
    vt_spec = lambda r: pl.BlockSpec((nh * dv, L), lambda bi, ci: (0, (bi * rows + r) * chunks + ci))
    return pl.pallas_call(
        functools.partial(_mlstm_kernel, rows=rows),
        grid=(b // rows, chunks),
        in_specs=[tok(nh * dk), tok(nh * dk), tok(nh * dv), tok(LANES),
                  pl.BlockSpec((rows, 2 * nh, L), lambda bi, ci: (bi, 0, ci)),
                  _resident((1, LANES)), _resident((2 * nh, 1)), _resident((nh * dv, L))]
                 + [vt_spec(r) for r in range(rows)],
        out_specs=tok(nh * dv),
        out_shape=jax.ShapeDtypeStruct((b, s, nh * dv), _BF16),
        scratch_shapes=[pltpu.VMEM((rows * nh, dv + ONES_ROWS, 2 * dk), _F32),
                        pltpu.VMEM((rows * nh, LANES), _F32)],
        compiler_params=_params("parallel", "arbitrary"), name="mlstm_chunk",
    )(q, k, o, gates, gates_row, bias_col, bias_row, norm_cols, *([vt] * rows))


GELU_C = math.sqrt(2.0 / math.pi)
RUN_PITCH = 72


def _mix_ffn_kernel(a0_ref, x0_ref, a_ref, x_ref, wo_ref, gmix_ref, gpre_ref, wup_ref, cw_ref, cb_ref,
                    wdn_ref, gpost_ref, o_ref,
                    h_ref, xp_ref, y_ref, carry_ref, slab_in_ref, slab_out_ref, hid_ref, *, tiles_per_seq):
    tm, d = x_ref.shape
    f = wdn_ref.shape[0]
    run = tm // SUBLANES
    n_slabs = d // LANES
    step = pl.program_id(0)

    n_slices = SUBLANES
    rows_of = lambda i: slice(i * run, (i + 1) * run)

    def mix_rows(mixed, x_blk_ref, r, zero):
        x1 = x_blk_ref[rows_of(r), :] + _rms(mixed[rows_of(r)] + zero, gmix_ref[...])
        for c in range(n_slabs):
            slab_in_ref[c, r * RUN_PITCH:r * RUN_PITCH + run, :] = x1[:, c * LANES:(c + 1) * LANES]

    def gather_slice(j, slot, zero):
        xp = jnp.concatenate(
            [jnp.concatenate([slab_in_ref[c, pl.ds(k, SUBLANES, stride=RUN_PITCH), :]
                              for k in range(j * SUBLANES, (j + 1) * SUBLANES)], axis=0)
             for c in range(n_slabs)], axis=1) + zero
        xp_ref[slot, rows_of(j), :] = xp
        h_ref[slot, rows_of(j), :] = _rms(xp, gpre_ref[...]).astype(_BF16)

    def finish_slice(j, slot, zero):
        out = xp_ref[slot, rows_of(j), :] + _rms(y_ref[slot, rows_of(j), :] + zero, gpost_ref[...])
        for c in range(n_slabs):
            for kk in range(SUBLANES):
                slab_out_ref[c, pl.ds(j * SUBLANES + kk, SUBLANES, stride=RUN_PITCH), :] = (
                    out[kk * SUBLANES:(kk + 1) * SUBLANES, c * LANES:(c + 1) * LANES])

    def zero_after(value):
        bits = pltpu.bitcast(value[:SUBLANES, :LANES], jnp.uint32)
        bits = lax.shift_right_logical(lax.shift_right_logical(bits, jnp.uint32(16)), jnp.uint32(16))
        return jnp.concatenate([pltpu.bitcast(bits, _F32)[0:1, :]] * n_slabs, axis=1)

    no_wait = jnp.zeros((1, d), _F32)

    @pl.when(step == 0)
    def _():
        y_ref[...] = jnp.zeros_like(y_ref)
        xp_ref[...] = jnp.zeros_like(xp_ref)
        mixed0 = jnp.dot(a0_ref[...], wo_ref[...], preferred_element_type=_F32)
        for r in range(n_slices):
            mix_rows(mixed0, x0_ref, r, no_wait)
        for j in range(n_slices):
            gather_slice(j, 0, no_wait)

    @pl.when(step % tiles_per_seq == 0)
    def _():
        carry_ref[...] = jnp.zeros_like(carry_ref)

    cur = step % 2
    other = (step + 1) % 2

    mixed = jnp.dot(a_ref[...], wo_ref[...], preferred_element_type=_F32)

    h = h_ref[cur]
    first_sublane = lax.broadcasted_iota(jnp.int32, (SUBLANES, FFN_CHUNK), 0) == 0

    def conv(c0, width):
        cols = slice(c0, c0 + width)
        u = jnp.dot(h, wup_ref[:, cols], preferred_element_type=_F32)
        prev = carry_ref[:, cols]
        tail1, tail2 = u[tm - SUBLANES:], u[tm - 2 * SUBLANES:tm - SUBLANES]
        carry_ref[:, cols] = u[tm - 2 * SUBLANES:]
        mask = first_sublane[:, :width]
        wrap1 = jnp.where(mask, prev[2 * SUBLANES - 1:, :], pltpu.roll(tail1, 1, axis=0))
        wrap2 = jnp.where(mask, prev[SUBLANES - 1:SUBLANES, :], pltpu.roll(tail2, 1, axis=0))
        u1 = jnp.concatenate([wrap1, u[:tm - SUBLANES]], axis=0)
        u2 = jnp.concatenate([wrap2, wrap1, u[:tm - 2 * SUBLANES]], axis=0)
        return (cb_ref[:, cols] + cw_ref[2:3, cols] * u + cw_ref[1:2, cols] * u1
                + cw_ref[0:1, cols] * u2), zero_after(u)

    pieces = ([functools.partial(piece, j) for j in range(n_slices)
               for piece in (functools.partial(finish_slice, slot=other),
                             functools.partial(mix_rows, mixed, x_ref))]
              + [functools.partial(gather_slice, j, other) for j in range(n_slices)])
    chunks = list(range(0, f, FFN_CHUNK))
    last_tie = 2 * len(chunks) - 1
    for i, c0 in enumerate(chunks):
        width = min(FFN_CHUNK, f - c0)
        gate, after_gate = conv(c0, width)
        half_val, after_val = conv(f + c0, width)
        inner = gate * ((gate * gate) * (GELU_C * 0.044715) + GELU_C)
        w = gate * half_val
        hid_ref[:, c0:c0 + width] = (w * jnp.tanh(inner) + w).astype(_BF16)
        for tie, zero in ((2 * i, after_gate), (2 * i + 1, after_val)):
            for t, piece in enumerate(pieces):
                if min(t, last_tie) == tie:
                    piece(zero=zero)

    y_ref[cur] = jnp.dot(hid_ref[...], wdn_ref[...], preferred_element_type=_F32)
    for c in range(n_slabs):
        for r in range(SUBLANES):
            o_ref[r * run:(r + 1) * run, c * LANES:(c + 1) * LANES] = (
                slab_out_ref[c, r * RUN_PITCH:r * RUN_PITCH + run, :])


def _mix_ffn(a, x, w_out, g_mix, g_pre, w_up, conv_w, conv_b, w_down, g_post, seq_len):
    t, d = x.shape
    k = a.shape[1]
    f = w_down.shape[0]
    tm = min(ROW_TILE, seq_len)
    n = t // tm
    half = jnp.concatenate([jnp.ones((f,), _F32), jnp.full((f,), 0.5, _F32)])
    first = lambda width: pl.BlockSpec((tm, width), lambda i: (0, 0), pipeline_mode=pl.Buffered(1))
    ahead = lambda width: pl.BlockSpec((tm, width), lambda i: (jnp.minimum(i + 1, n - 1), 0))
    row = lambda width: _resident((1, width))
    return pl.pallas_call(
        functools.partial(_mix_ffn_kernel, tiles_per_seq=seq_len // tm),
        grid=(n + 1,),
        in_specs=[first(k), first(d), ahead(k), ahead(d), _resident((k, d)), row(d), row(d),
                  _resident((d, 2 * f)), _resident((CONV_WIDTH, 2 * f)), row(2 * f), _resident((f, d)),
                  row(d)],
        out_specs=pl.BlockSpec((tm, d), lambda i: (jnp.maximum(i - 1, 0), 0)),
        out_shape=jax.ShapeDtypeStruct((t, d), _F32),
        scratch_shapes=[pltpu.VMEM((2, tm, d), _BF16), pltpu.VMEM((2, tm, d), _F32),
                        pltpu.VMEM((2, tm, d), _F32), pltpu.VMEM((2 * SUBLANES, 2 * f), _F32),
                        pltpu.VMEM((d // LANES, SUBLANES * RUN_PITCH, LANES), _F32),
                        pltpu.VMEM((d // LANES, SUBLANES * RUN_PITCH, LANES), _F32),
                        pltpu.VMEM((tm, f), _BF16)],
        compiler_params=_params("arbitrary"), name="mix_ffn",
    )(a, x, a, x, w_out, g_mix.reshape(1, d), g_pre.reshape(1, d), w_up, conv_w * half,
      (conv_b * half).reshape(1, 2 * f), w_down, g_post.reshape(1, d))


def _decay_kernel(f_ref, bias_ref, place_ref, cq_ref, ck_ref, carry_ref):
    tm = f_ref.shape[1]

    @pl.when(pl.program_id(1) == 0)
    def _():
        carry_ref[...] = jnp.zeros_like(carry_ref)

    lane = lax.broadcasted_iota(jnp.int32, (tm, LANES), 1)
    row = lax.broadcasted_iota(jnp.int32, (tm, LANES), 0)
    c = jnp.where(lane < FOX_HEADS, _log_sigmoid(f_ref[0] + bias_ref[...]), 0.0)
    shift = 1
    while shift < tm:
        c = c + jnp.where(row >= shift, pltpu.roll(c, shift, axis=0), 0.0)
        shift *= 2
    c = carry_ref[0:1, :] + c
    carry_ref[...] = jnp.broadcast_to(c[tm - 1:tm, :], carry_ref.shape)
    c = c * LOG2E

    packed = jnp.where(lane < FOX_HEADS, 1.0, 0.0)
    packed = pltpu.roll(packed, DECAY_PARTS * FOX_HEADS, axis=1)
    rest = c
    for j in range(DECAY_PARTS):
        part = rest.astype(_BF16).astype(_F32)
        rest = rest - part
        packed = packed + (part if j == 0 else pltpu.roll(part, j * FOX_HEADS, axis=1))
    placed = jnp.dot(packed.astype(_BF16), place_ref[...], preferred_element_type=_F32)
    width = cq_ref.shape[2]
    cq_ref[0] = placed[:, :width].astype(cq_ref.dtype)
    ck_ref[0] = placed[:, width:].astype(ck_ref.dtype)


def _decay_placement():
    nh, dh, parts = FOX_HEADS, FOX_HEAD_DIM, DECAY_PARTS
    width = nh * dh
    heads = jnp.arange(nh)
    base = (heads // 2) * 2 * dh + (1 - heads % 2) * dh
    ones_row = parts * nh + heads
    place = jnp.zeros((LANES, 2 * width), _F32)
    for j in range(parts):
        place = place.at[j * nh + heads, base + j].set(1.0)
        place = place.at[ones_row, base + parts + j].set(1.0)
        place = place.at[ones_row, width + base + j].set(1.0)
        place = place.at[j * nh + heads, width + base + parts + j].set(-1.0)
    return place.astype(_BF16)


def _fox_decay(f_pre, bias):
    b, s, _ = f_pre.shape
    nh, dh = FOX_HEADS, FOX_HEAD_DIM
    tm = min(ROW_TILE, s)
    bias_row = jnp.zeros((1, LANES), _F32).at[0, :nh].set(bias)
    out = jax.ShapeDtypeStruct((b, s, nh * dh), _BF16)
    return pl.pallas_call(
        _decay_kernel,
        grid=(b, s // tm),
        in_specs=[pl.BlockSpec((1, tm, LANES), lambda bi, ti: (bi, ti, 0)), _resident((1, LANES)),
                  _resident((LANES, 2 * nh * dh))],
        out_specs=[pl.BlockSpec((1, tm, nh * dh), lambda bi, ti: (bi, ti, 0))] * 2,
        out_shape=[out, out],
        scratch_shapes=[pltpu.VMEM((SUBLANES, LANES), _F32)],
        compiler_params=_params("parallel", "arbitrary"), name="fox_decay",
    )(f_pre, bias_row, _decay_placement())


def _attn_kernel(q_ref, cq_ref, g_ref, k_ref, ck_ref, vt_ref, o_ref,
                 sa_ref, sb_ref, sa_max_ref, sb_max_ref, m_ref, acc_ref):
    tq = q_ref.shape[1]
    tk = sa_ref.shape[1]
    dh = FOX_HEAD_DIM
    qi = pl.program_id(2)
    first_q = lax.broadcasted_iota(jnp.int32, (tq, 2 * dh), 1) < dh
    first_k = lax.broadcasted_iota(jnp.int32, (tk, 2 * dh), 1) < dh
    q_pair, cq_pair = q_ref[0], cq_ref[0]
    q_aug = (jnp.where(first_q, q_pair, cq_pair), jnp.where(first_q, cq_pair, q_pair))

    ones_blk = jnp.ones((ONES_ROWS, tk), _BF16)

    m_ref[...] = jnp.full_like(m_ref, -jnp.inf)
    acc_ref[...] = jnp.zeros_like(acc_ref)

    def keys(j):
        start = pl.multiple_of(j * tk, tk)
        k_pair = k_ref[0, pl.ds(start, tk), :]
        ck_pair = ck_ref[0, pl.ds(start, tk), :]
        return jnp.where(first_k, k_pair, ck_pair), jnp.where(first_k, ck_pair, k_pair)

    def scores(k_aug, bufs, c, hh, diagonal):
        s_ref, smax_ref = bufs
        cols = slice(c * tk, (c + 1) * tk)
        s = lax.dot_general(k_aug[hh], q_aug[hh][cols], (((1,), (1,)), ((), ())),
                            preferred_element_type=_F32)
        if diagonal:
            key = lax.broadcasted_iota(jnp.int32, (tk, tk), 0)
            qry = lax.broadcasted_iota(jnp.int32, (tk, tk), 1)
            s = jnp.where(key <= qry, s, -jnp.inf)
        s_ref[hh, :, cols] = s
        smax_ref[hh, :, cols] = jnp.max(s, axis=0, keepdims=True)

    def accumulate(j, bufs, c, hh):
        s_ref, smax_ref = bufs
        start = pl.multiple_of(j * tk, tk)
        cols = slice(c * tk, (c + 1) * tk)
        m_old = m_ref[hh, :, cols]
        m_new = jnp.maximum(m_old, smax_ref[hh, :, cols])
        alpha = jnp.exp2(m_old - m_new)
        p = jnp.exp2(s_ref[hh, :, cols] - m_new).astype(_BF16)
        vt = jnp.concatenate([vt_ref[hh * dh:(hh + 1) * dh, pl.ds(start, tk)], ones_blk], axis=0)
        acc_ref[hh, :, cols] = alpha * acc_ref[hh, :, cols] + jnp.dot(vt, p, preferred_element_type=_F32)
        m_ref[hh, :, cols] = m_new

    per_tile = tq // tk
    buf_a, buf_b = (sa_ref, sa_max_ref), (sb_ref, sb_max_ref)
    pieces = [(hh, c) for c in range(per_tile) for hh in range(2)]

    first_keys = keys(0)
    for hh, c in pieces:
        scores(first_keys, buf_a, c, hh, False)

    def body(i, carry):
        for u in range(per_tile):
            j = i * per_tile + u
            cur, nxt = (buf_a, buf_b) if u % 2 == 0 else (buf_b, buf_a)
            next_keys = keys(j + 1)
            for hh, c in pieces:
                scores(next_keys, nxt, c, hh, False)
                accumulate(j, cur, c, hh)
        return carry

    lax.fori_loop(0, qi, body, 0)
    for hh in range(2):
        key = lax.broadcasted_iota(jnp.int32, (tk, tk), 0)
        qry = lax.broadcasted_iota(jnp.int32, (tk, tk), 1)
        square = jnp.where(key <= qry, sa_ref[hh, :, :tk], -jnp.inf)
        sa_ref[hh, :, :tk] = square
        sa_max_ref[hh, :, :tk] = jnp.max(square, axis=0, keepdims=True)
    for u in range(per_tile):
        j = qi * per_tile + u
        cur, nxt = (buf_a, buf_b) if u % 2 == 0 else (buf_b, buf_a)
        next_keys = keys(j + 1) if u + 1 < per_tile else None
        for hh in range(2):
            accumulate(j, cur, u, hh)
            for c in range(u + 1, per_tile):
                scores(next_keys, nxt, c, hh, c == u + 1)
                accumulate(j, cur, c, hh)

    halves = []
    for hh in range(2):
        acc = acc_ref[hh]
        halves.append(acc[0:dh, :] / acc[dh:dh + 1, :])
    out = jnp.concatenate(halves, axis=0).T
    o_ref[0] = (out * jax.nn.sigmoid(g_ref[0].astype(_F32))).astype(o_ref.dtype)


def _fox_attention(q, cq, g, k, ck, vt):
    b, s, hd = q.shape
    pairs = hd // LANES
    tq = min(ATTN_TILE, s)
    tk = min(ATTN_KEYS, tq // 2)
    rows = FOX_HEAD_DIM + ONES_ROWS
    qspec = pl.BlockSpec((1, tq, LANES), lambda bi, pi, qi: (bi, qi, pi))
    kspec = pl.BlockSpec((1, s, LANES), lambda bi, pi, qi: (bi, 0, pi))
    vspec = pl.BlockSpec((LANES, s), lambda bi, pi, qi: (pi, bi))
    return pl.pallas_call(
        _attn_kernel,
        grid=(b, pairs, s // tq),
        in_specs=[qspec, qspec, qspec, kspec, kspec, vspec],
        out_specs=qspec,
        out_shape=jax.ShapeDtypeStruct((b, s, hd), _BF16),
        scratch_shapes=[pltpu.VMEM((2, tk, tq), _F32), pltpu.VMEM((2, tk, tq), _F32),
                        pltpu.VMEM((2, 1, tq), _F32), pltpu.VMEM((2, 1, tq), _F32),
                        pltpu.VMEM((2, 1, tq), _F32), pltpu.VMEM((2, rows, tq), _F32)],
        compiler_params=_params("parallel", "parallel", "arbitrary"), name="fox_attention",
    )(q, cq, g, k, ck, vt)


def _pad_cols(w, width):
    return jnp.pad(w, ((0, 0), (0, width - w.shape[1])))


def kernel(x, norm_mix_pre, norm_mix_post, norm_ffn_pre, norm_ffn_post, mlstm_w_in, mlstm_b_gate,
           mlstm_norm, mlstm_w_out, kv_norm, kv_w, kv_b_f, fox_w_qg, fox_w_out, ffn_w_up,
           ffn_conv_w, ffn_conv_b, ffn_w_down):
    b, s, d = x.shape
    depth = norm_mix_pre.shape[0]
    n_a = mlstm_w_in.shape[0]
    xt = x.reshape(b * s, d)
    hk = MLSTM_HEADS * MLSTM_QK_DIM
    hv = MLSTM_HEADS * MLSTM_V_DIM
    k_sh = vt_sh = cq_sh = ck_sh = None
    bf = lambda w: w.astype(_BF16)
    m_q = bf(mlstm_w_in[:, :, :hk] * MLSTM_QK_DIM ** -0.5)
    m_k = bf(mlstm_w_in[:, :, hk:2 * hk])
    m_vt = bf(jnp.swapaxes(mlstm_w_in[:, :, 2 * hk:2 * hk + hv], 1, 2))
    m_o = bf(mlstm_w_in[:, :, 2 * hk + hv:2 * hk + 2 * hv])
    m_gates = bf(jnp.pad(mlstm_w_in[:, :, 2 * hk + 2 * hv:], ((0, 0), (0, 0), (0, LANES - 2 * MLSTM_HEADS))))
    f_q = bf(fox_w_qg[:, :, :d] * (FOX_HEAD_DIM ** -0.5 * LOG2E))
    f_g = bf(fox_w_qg[:, :, d:])
    out_w = [bf(mlstm_w_out), bf(fox_w_out)]
    up_w, down_w = bf(ffn_w_up), bf(ffn_w_down)
    seq = lambda a: a.reshape(b, s, a.shape[-1])
    for layer in range(depth):
        if layer < n_a:
            weights = [m_q[layer], m_k[layer], m_vt[layer], m_o[layer], m_gates[layer]]
            q, k, vt, o, gates = _norm_proj(xt, norm_mix_pre[layer], weights,
                                            [_BF16, _BF16, _BF16, _BF16, _F32], "mlstm_in_proj",
                                            transposed=(False, False, True, False, False))
            hs = _mlstm(seq(q), seq(k), vt, seq(o), seq(gates), mlstm_b_gate[layer], mlstm_norm[layer])
            mixed, w_out = hs.reshape(b * s, hv), out_w[0][layer]
        else:
            j = layer - n_a
            q, g = _norm_proj(xt, norm_mix_pre[layer], [f_q[j], f_g[j]], [_BF16, _BF16], "fox_in_proj")
            att = _fox_attention(seq(q), cq_sh, seq(g), k_sh, ck_sh, vt_sh)
            mixed, w_out = att.reshape(b * s, d), out_w[1][j]
        xt = _mix_ffn(mixed, xt, w_out, norm_mix_post[layer], norm_ffn_pre[layer], up_w[layer],
                      ffn_conv_w[layer], ffn_conv_b[layer], down_w[layer], norm_ffn_post[layer], s)
        if layer == n_a - 1:
            weights = [bf(kv_w[:, :d]), bf(kv_w[:, d:2 * d].T), bf(_pad_cols(kv_w[:, 2 * d:], LANES))]
            k2, vt_sh, f_pre = _norm_proj(xt, kv_norm, weights, [_BF16, _BF16, _F32], "kv_proj",
                                          transposed=(False, True, False))
            k_sh = k2.reshape(b, s, d)
            cq_sh, ck_sh = _fox_decay(f_pre.reshape(b, s, LANES), kv_b_f)
    return xt.reshape(b, s, d)
```
